```python
import math
import jax, jax.numpy as jnp
from jax import lax
import numpy as np

D_MODEL = 1024
BATCH = 4
SEQ = 4096
DEPTH = 2

GRID_W = 64
BLOCK = 128
N_MEM = 256
EPS = 1e-6
ROPE_THETA = 10000.0
NEG_INF = -1e30
F32 = jnp.float32

A_HEADS = 4
A_DK = 64
A_DV = 2 * A_DK
B_HEADS = 8
B_Q_LORA = 256
B_KV_LORA = 128
B_NOPE = 64
B_ROPE = 32
B_DV = 64
C_HEADS = 8
C_KV_HEADS = 2
C_DH = 64
WINDOW = 128
D_HEADS = 8
D_KV_HEADS = 2
D_DH = 64
X_HEADS = 4
X_DH = 128
D_FF = 4 * D_MODEL
REL_BUCKETS = 32
REL_MAX_DIST = 128

N_BRANCH = 4
BRANCH_W = 512

IN_SIZES = (
    A_HEADS * 2 * A_DK,
    A_HEADS * 2 * A_DK,
    A_HEADS * A_DV,
    B_Q_LORA,
    B_KV_LORA,
    B_ROPE,
    C_HEADS * C_DH,
    C_KV_HEADS * C_DH,
    C_KV_HEADS * C_DH,
    D_HEADS * D_DH,
    D_KV_HEADS * D_DH,
    D_KV_HEADS * D_DH,
    N_BRANCH * D_MODEL,
)
IN_WIDTH = 7584

kernel_name = "hybrid_gated_parallel_encoder"


def rms(x, g):
    xf = x.astype(F32)
    y = xf * lax.rsqrt(jnp.mean(xf * xf, axis=-1, keepdims=True) + EPS)
    return (y * g.astype(F32)).astype(x.dtype)


def rope(x, pos):
    d = x.shape[-1]
    inv = ROPE_THETA ** (-jnp.arange(0, d, 2, dtype=F32) / d)
    ang = pos.astype(F32)[:, None] * inv[None, :]
    cos = jnp.concatenate([jnp.cos(ang)] * 2, -1)[:, None, :].astype(x.dtype)
    sin = jnp.concatenate([jnp.sin(ang)] * 2, -1)[:, None, :].astype(x.dtype)
    x1, x2 = x[..., : d // 2], x[..., d // 2:]
    return x * cos + jnp.concatenate([-x2, x1], -1) * sin


def axial_rope(x, row, col):
    h = x.shape[-1] // 2
    return jnp.concatenate([rope(x[..., :h], row), rope(x[..., h:], col)], -1)


def t5_bucket(rel):
    nb = REL_BUCKETS // 2
    max_exact = nb // 2
    n = jnp.abs(rel)
    nf = jnp.maximum(n, 1).astype(F32)
    large = max_exact + (jnp.log(nf / max_exact) / math.log(REL_MAX_DIST / max_exact)
                         * (nb - max_exact)).astype(jnp.int32)
    large = jnp.minimum(large, nb - 1)
    return jnp.where(rel > 0, nb, 0) + jnp.where(n < max_exact, n, large)


def to_blocks(t):
    b, s = t.shape[:2]
    return jnp.moveaxis(t.reshape(b, s // BLOCK, BLOCK, *t.shape[2:]), 1, 0)


def from_blocks(t):
    t = jnp.moveaxis(t, 0, 1)
    return t.reshape(t.shape[0], t.shape[1] * t.shape[2], *t.shape[3:])


def diff_attention(q, k, v, lam, g_qk, g_sub, rel_table, lam_init):
    b, s = q.shape[:2]
    q = rms(q, g_qk[0])
    k = rms(k, g_qk[1])
    lamf = lam.astype(F32)
    lam_full = jnp.exp(jnp.sum(lamf[0] * lamf[1])) - jnp.exp(jnp.sum(lamf[2] * lamf[3])) + lam_init
    scale = A_DK ** -0.5
    kpos = jnp.arange(s)

    def block(args):
        qb, i = args
        qpos = i * BLOCK + jnp.arange(BLOCK)
        bias = rel_table[t5_bucket(kpos[None, :] - qpos[:, None])]
        bias = jnp.transpose(bias, (2, 0, 1)).astype(F32)
        sc = jnp.einsum('bqhmd,bkhmd->bmhqk', qb, k).astype(F32) * scale + bias
        p = jax.nn.softmax(sc, axis=-1)
        w = p[:, 0] - lam_full * p[:, 1]
        return jnp.einsum('bhqk,bkhe->bqhe', w.astype(v.dtype), v)

    o = from_blocks(lax.map(block, (to_blocks(q), jnp.arange(s // BLOCK))))
    o = rms(o, g_sub) * (1.0 - lam_init)
    return o.reshape(b, s, A_HEADS * A_DV)


def mla(c_q, c_kv, k_rope, g_cq, g_ckv, w_uq, w_ukv, g_qk, pos):
    b, s = c_q.shape[:2]
    q = (rms(c_q, g_cq) @ w_uq).reshape(b, s, B_HEADS, B_NOPE + B_ROPE)
    kv = (rms(c_kv, g_ckv) @ w_ukv).reshape(b, s, B_HEADS, B_NOPE + B_DV)
    q_nope = rms(q[..., :B_NOPE], g_qk[0, :B_NOPE])
    q_rope = rope(rms(q[..., B_NOPE:], g_qk[0, B_NOPE:]), pos)
    k_nope = rms(kv[..., :B_NOPE], g_qk[1, :B_NOPE])
    v = kv[..., B_NOPE:]
    k_r = rope(rms(k_rope, g_qk[1, B_NOPE:])[:, :, None, :], pos)[:, :, 0, :]
    scale = (B_NOPE + B_ROPE) ** -0.5

    def block(args):
        qn, qr = args
        sc = (jnp.einsum('bqhd,bkhd->bhqk', qn, k_nope)
              + jnp.einsum('bqhr,bkr->bhqk', qr, k_r)).astype(F32) * scale
        p = jax.nn.softmax(sc, axis=-1)
        return jnp.einsum('bhqk,bkhe->bqhe', p.astype(v.dtype), v)

    o = from_blocks(lax.map(block, (to_blocks(q_nope), to_blocks(q_rope))))
    return o.reshape(b, s, B_HEADS * B_DV)


def window_gqa(q, k, v, g_qk, sink, rel_table):
    b, s = q.shape[:2]
    nb = s // BLOCK
    grp = C_HEADS // C_KV_HEADS
    q = rms(q, g_qk[0])
    k = rms(k, g_qk[1])
    qb = q.reshape(b, nb, BLOCK, C_KV_HEADS, grp, C_DH)

    def band(t):
        tp = jnp.pad(t, ((0, 0), (BLOCK, BLOCK), (0, 0), (0, 0)))
        tp = tp.reshape(b, nb + 2, BLOCK, *t.shape[2:])
        return jnp.concatenate([tp[:, :-2], tp[:, 1:-1], tp[:, 2:]], axis=2)

    kb, vb = band(k), band(v)
    a = jnp.arange(BLOCK)
    c = jnp.arange(3 * BLOCK)
    rel = c[None, :] - BLOCK - a[:, None]
    bias = jnp.transpose(rel_table[t5_bucket(rel)], (2, 0, 1))
    bias = bias.reshape(C_KV_HEADS, grp, BLOCK, 3 * BLOCK).astype(F32)
    kpos = jnp.arange(nb)[:, None] * BLOCK - BLOCK + c[None, :]
    valid = ((jnp.abs(rel) <= WINDOW)[None]
             & (kpos >= 0)[:, None, :] & (kpos < s)[:, None, :])
    sc = jnp.einsum('bnqgrd,bnkgd->bngrqk', qb, kb).astype(F32) * C_DH ** -0.5 + bias
    sc = jnp.where(valid[None, :, None, None], sc, NEG_INF)
    sk = jnp.broadcast_to(sink.reshape(C_KV_HEADS, grp, 1, 1).astype(F32), sc.shape[:-1] + (1,))
    p = jax.nn.softmax(jnp.concatenate([sc, sk], axis=-1), axis=-1)[..., :-1]
    o = jnp.einsum('bngrqk,bnkgd->bnqgrd', p.astype(v.dtype), vb)
    return o.reshape(b, s, C_HEADS * C_DH)


def axial_gqa(q, k, v, g_qk, row, col):
    b, s = q.shape[:2]
    grp = D_HEADS // D_KV_HEADS
    q = axial_rope(rms(q, g_qk[0]), row, col)
    k = axial_rope(rms(k, g_qk[1]), row, col)
    scale = D_DH ** -0.5

    def block(qb):
        qg = qb.reshape(b, BLOCK, D_KV_HEADS, grp, D_DH)
        sc = jnp.einsum('bqgrd,bkgd->bgrqk', qg, k).astype(F32) * scale
        p = jax.nn.softmax(sc, axis=-1)
        o = jnp.einsum('bgrqk,bkgd->bqgrd', p.astype(v.dtype), v)
        return o.reshape(b, BLOCK, D_HEADS * D_DH)

    return from_blocks(lax.map(block, to_blocks(q)))


def memory_xattn(h, mem_n, w_xq, w_xkv, g_qk, w_xo):
    b, s = h.shape[:2]
    m = mem_n.shape[1]
    q = rms((h @ w_xq).reshape(b, s, X_HEADS, X_DH), g_qk[0])
    kv = (mem_n @ w_xkv).reshape(b, m, 2, X_HEADS, X_DH)
    k = rms(kv[:, :, 0], g_qk[1])
    v = kv[:, :, 1]
    sc = jnp.einsum('bqhd,bkhd->bhqk', q, k).astype(F32) * X_DH ** -0.5
    p = jax.nn.softmax(sc, axis=-1)
    o = jnp.einsum('bhqk,bkhd->bqhd', p.astype(v.dtype), v)
    return o.reshape(b, s, X_HEADS * X_DH) @ w_xo


def setup_inputs(seed: int = 0) -> dict:
    key = jax.random.key(seed)
    ks = iter(jax.random.split(key, 40))
    L = DEPTH

    def w(shape, fan_in):
        return jax.random.normal(next(ks), shape, F32) * fan_in ** -0.5

    def gain(shape):
        return 1.0 + 0.02 * jax.random.normal(next(ks), shape, F32)

    def small(shape, sc):
        return sc * jax.random.normal(next(ks), shape, F32)

    return {
        "x": jax.random.normal(next(ks), (BATCH, SEQ, D_MODEL), F32),
        "mem": jax.random.normal(next(ks), (BATCH, N_MEM, D_MODEL), F32),
        "rel_bias": small((REL_BUCKETS, A_HEADS + C_HEADS), 0.1),
        "g_mix": gain((L, D_MODEL)),
        "w_in": w((L, D_MODEL, IN_WIDTH), D_MODEL),
        "lam": small((L, 4, A_DK), 0.1),
        "g_qk_a": gain((L, 2, A_DK)),
        "g_sub_a": gain((L, A_DV)),
        "g_cq": gain((L, B_Q_LORA)),
        "g_ckv": gain((L, B_KV_LORA)),
        "w_uq": w((L, B_Q_LORA, B_HEADS * (B_NOPE + B_ROPE)), B_Q_LORA),
        "w_ukv": w((L, B_KV_LORA, B_HEADS * (B_NOPE + B_DV)), B_KV_LORA),
        "g_qk_b": gain((L, 2, B_NOPE + B_ROPE)),
        "g_qk_c": gain((L, 2, C_DH)),
        "sink_c": small((L, C_HEADS), 0.5),
        "g_qk_d": gain((L, 2, D_DH)),
        "w_branch": w((L, N_BRANCH, BRANCH_W, D_MODEL), BRANCH_W),
        "w_out": w((L, D_MODEL, D_MODEL), D_MODEL),
        "g_x": gain((L, D_MODEL)),
        "g_mem": gain((L, D_MODEL)),
        "w_xq": w((L, D_MODEL, X_HEADS * X_DH), D_MODEL),
        "w_xkv": w((L, D_MODEL, 2 * X_HEADS * X_DH), D_MODEL),
        "g_qk_x": gain((L, 2, X_DH)),
        "w_xo": w((L, X_HEADS * X_DH, D_MODEL), X_HEADS * X_DH),
        "g_mlp": gain((L, D_MODEL)),
        "w_up": w((L, D_MODEL, D_FF), D_MODEL),
        "w_down": w((L, D_FF, D_MODEL), D_FF),
    }


def reference(x, mem, rel_bias, g_mix, w_in, lam, g_qk_a, g_sub_a, g_cq, g_ckv, w_uq, w_ukv,
              g_qk_b, g_qk_c, sink_c, g_qk_d, w_branch, w_out, g_x, g_mem, w_xq, w_xkv,
              g_qk_x, w_xo, g_mlp, w_up, w_down):
    b, s, _ = x.shape
    rows = s // GRID_W
    pos = jnp.arange(s)
    row = jnp.repeat(jnp.arange(rows), GRID_W)
    col = jnp.tile(jnp.arange(GRID_W), rows)
    split_points = np.cumsum(IN_SIZES)[:-1].tolist()
    rel_a = rel_bias[:, :A_HEADS]
    rel_c = rel_bias[:, A_HEADS:]

    for layer in range(DEPTH):
        lam_init = 0.8 - 0.6 * math.exp(-0.3 * layer)
        h = rms(x, g_mix[layer])
        (aq, ak, av, bcq, bckv, bkr, cq, ck, cv, dq, dk, dv, gate) = jnp.split(
            h @ w_in[layer], split_points, axis=-1)
        o_a = diff_attention(aq.reshape(b, s, A_HEADS, 2, A_DK), ak.reshape(b, s, A_HEADS, 2, A_DK),
                             av.reshape(b, s, A_HEADS, A_DV), lam[layer], g_qk_a[layer],
                             g_sub_a[layer], rel_a, lam_init)
        o_b = mla(bcq, bckv, bkr, g_cq[layer], g_ckv[layer], w_uq[layer], w_ukv[layer],
                  g_qk_b[layer], pos)
        o_c = window_gqa(cq.reshape(b, s, C_HEADS, C_DH), ck.reshape(b, s, C_KV_HEADS, C_DH),
                         cv.reshape(b, s, C_KV_HEADS, C_DH), g_qk_c[layer], sink_c[layer], rel_c)
        o_d = axial_gqa(dq.reshape(b, s, D_HEADS, D_DH), dk.reshape(b, s, D_KV_HEADS, D_DH),
                        dv.reshape(b, s, D_KV_HEADS, D_DH), g_qk_d[layer], row, col)
        o = jnp.stack([o_a, o_b, o_c, o_d], axis=2)
        br = jnp.einsum('bsmc,mcd->bsmd', o, w_branch[layer])
        g = jax.nn.sigmoid(gate.reshape(b, s, N_BRANCH, D_MODEL))
        x = x + jnp.einsum('bsmd,bsmd->bsd', g, br) @ w_out[layer]
        x = x + memory_xattn(rms(x, g_x[layer]), rms(mem, g_mem[layer]), w_xq[layer],
                             w_xkv[layer], g_qk_x[layer], w_xo[layer])
        u = rms(x, g_mlp[layer]) @ w_up[layer]
        x = x + jnp.square(jax.nn.relu(u)) @ w_down[layer]
    return x
```

```python
import functools
import math

import numpy as np
import jax
import jax.numpy as jnp
from jax import lax
from jax.experimental import pallas as pl
from jax.experimental.pallas import tpu as pltpu

F32 = jnp.float32
BF16 = jnp.bfloat16

D_MODEL = 1024
GRID_W = 64
N_MEM = 256
EPS = 1e-6
ROPE_THETA = 10000.0
NEG_INF = -1e30
LOG2E = 1.4426950408889634

A_HEADS, A_DK, A_DV = 4, 64, 128
B_HEADS, B_Q_LORA, B_KV_LORA, B_NOPE, B_ROPE, B_DV = 8, 256, 128, 64, 32, 64
C_HEADS, C_KV_HEADS, C_DH, WINDOW = 8, 2, 64, 128
D_HEADS, D_KV_HEADS, D_DH = 8, 2, 64
X_HEADS, X_DH = 4, 128
D_FF = 4 * D_MODEL
REL_BUCKETS, REL_MAX_DIST = 32, 128
N_BRANCH, BRANCH_W = 4, 512

IN_SIZES = (512, 512, 512, B_Q_LORA, B_KV_LORA, B_ROPE, 512, 128, 128, 512, 128, 128,
            N_BRANCH * D_MODEL)
GROUP_W = 512
N_QKV_GROUPS = 7
N_GATE_GROUPS = N_BRANCH * D_MODEL // GROUP_W
IN_WIDTH_PADDED = (N_QKV_GROUPS + N_GATE_GROUPS) * GROUP_W

LANES = 128
HALF = 64
SEG = 256
VMEM_CAP = 60 * 1024 * 1024

(V_GMIX, V_AQ, V_AK, V_GCQ, V_GCKV, V_QB, V_QB_CNT, V_KB, V_KR, V_CQ, V_G5, V_DQ,
 N_VROWS) = range(13)


def _vmem_limit(*nbytes):
    return int(min(VMEM_CAP, sum(nbytes) + (6 << 20)))


def _const_spec(shape):
    zeros = (0,) * len(shape)
    return pl.BlockSpec(shape, lambda *_: zeros, pipeline_mode=pl.Buffered(1))


def _nbytes(shape, dtype):
    return int(np.prod(shape)) * jnp.dtype(dtype).itemsize


def _dot(a, b):
    return jnp.dot(a, b, preferred_element_type=F32)


def _dot_nt(a, b):
    return lax.dot_general(a, b, (((1,), (1,)), ((), ())), preferred_element_type=F32)


def _rms_rows(x, g):
    return x * lax.rsqrt(jnp.mean(x * x, axis=-1, keepdims=True) + EPS) * g


def _seg_rms(y, bd, inv_cnt, gain):
    outs = []
    for c in range(y.shape[1] // SEG):
        sl = slice(c * SEG, (c + 1) * SEG)
        yc = y[:, sl]
        y2 = yc * yc
        hi = y2.astype(BF16)
        lo = (y2 - hi.astype(F32)).astype(BF16)
        ssq = _dot(hi, bd) + _dot(lo, bd)
        outs.append(yc * lax.rsqrt(ssq * inv_cnt[:, sl] + EPS) * gain[:, sl])
    return outs[0] if len(outs) == 1 else jnp.concatenate(outs, axis=1)


def _rope32(x, cos, sin):
    w = x.shape[1]
    lane = lax.broadcasted_iota(jnp.int32, x.shape, 1)
    first = (lane & 16) == 0
    rot = jnp.where(first, -pltpu.roll(x, w - 16, 1), pltpu.roll(x, 16, 1))
    return x * cos + rot * sin


def _tile_lanes(v, n):
    return v if n == 1 else jnp.concatenate([v] * n, axis=1)


def _pair_variants(x):
    lane = lax.broadcasted_iota(jnp.int32, x.shape, 1)
    low = lane < HALF
    sw = pltpu.roll(x, HALF, 1)
    zero = jnp.zeros_like(x)
    return (jnp.where(low, x, zero), jnp.where(low, zero, sw), jnp.where(low, sw, zero),
            jnp.where(low, zero, x), jnp.where(low, x, sw), jnp.where(low, sw, x))


def _front_kernel(x_ref, vec_ref, win_ref, wuq_ref, wukvk_ref, wukvv_ref, bd64_ref, bdb_ref,
                  cosb_ref, sinb_ref, cosd_ref, sind_ref,
                  qa_ref, ka1_ref, ka2_ref, va_ref, qb_ref, kb_ref, vb_ref,
                  qc_ref, kc_ref, vc_ref, qd_ref, kd_ref, vd_ref, gate_ref):
    def row(r, w):
        return vec_ref[r:r + 1, :w]

    ones = jnp.ones((1, D_MODEL), F32)
    inv64 = ones * (1.0 / HALF)
    bd64 = bd64_ref[...]
    h = _rms_rows(x_ref[...], row(V_GMIX, D_MODEL)).astype(BF16)

    def group(g):
        return _dot(h, win_ref[:, g * GROUP_W:(g + 1) * GROUP_W])

    lane512 = lax.broadcasted_iota(jnp.int32, (x_ref.shape[0], GROUP_W), 1)

    qa_ref[...] = _seg_rms(group(0), bd64, inv64, row(V_AQ, GROUP_W)).astype(BF16)
    ka = _seg_rms(group(1), bd64, inv64, row(V_AK, GROUP_W))
    first_map = (lane512 & HALF) == 0
    ka1_ref[...] = jnp.where(first_map, ka, 0.0).astype(BF16)
    ka2_ref[...] = jnp.where(first_map, 0.0, ka).astype(BF16)
    va_ref[...] = group(2).astype(BF16)

    y = group(3)
    cosb = cosb_ref[...]
    sinb = sinb_ref[...]
    cq = _rms_rows(y[:, :B_Q_LORA], row(V_GCQ, B_Q_LORA)).astype(BF16)
    qb = _seg_rms(_dot(cq, wuq_ref[...]), bdb_ref[...], row(V_QB_CNT, D_MODEL), row(V_QB, D_MODEL))
    qb_ref[...] = _rope32(qb, _tile_lanes(cosb, B_HEADS), _tile_lanes(sinb, B_HEADS)).astype(BF16)
    ckv = _rms_rows(y[:, B_Q_LORA:B_Q_LORA + B_KV_LORA], row(V_GCKV, B_KV_LORA)).astype(BF16)
    k_nope = _seg_rms(_dot(ckv, wukvk_ref[...]), bd64, inv64, row(V_KB, D_MODEL))
    kr = y[:, B_Q_LORA + B_KV_LORA:]
    kr = kr * lax.rsqrt(jnp.sum(kr * kr, axis=-1, keepdims=True) * (1.0 / B_ROPE) + EPS)
    kr = _rope32(kr * row(V_KR, LANES), cosb, sinb)
    kb_ref[...] = (k_nope + _tile_lanes(kr, B_HEADS)).astype(BF16)
    vb_ref[...] = _dot(ckv, wukvv_ref[...]).astype(BF16)

    cosd = cosd_ref[...]
    sind = sind_ref[...]
    y = group(5)
    yn = _seg_rms(y, bd64, inv64, row(V_G5, GROUP_W))
    kc = yn[:, 0:LANES]
    vc = y[:, LANES:2 * LANES]
    kd = _rope32(yn[:, 2 * LANES:3 * LANES], cosd, sind)
    vd = y[:, 3 * LANES:]
    for k_ref, v_ref, k, v in ((kc_ref, vc_ref, kc, vc), (kd_ref, vd_ref, kd, vd)):
        e0, o0, e1, o1, _, _ = _pair_variants(k)
        _, _, _, _, d0, d1 = _pair_variants(v)
        k_ref[...] = jnp.concatenate([e0, o0, e1, o1], axis=1).astype(BF16)
        v_ref[...] = jnp.concatenate([d0, d1], axis=1).astype(BF16)

    qc_ref[...] = _seg_rms(group(4), bd64, inv64, row(V_CQ, GROUP_W)).astype(BF16)
    qd = _seg_rms(group(6), bd64, inv64, row(V_DQ, GROUP_W))
    qd_ref[...] = _rope32(qd, _tile_lanes(cosd, 4), _tile_lanes(sind, 4)).astype(BF16)

    for c in range(N_GATE_GROUPS):
        z = group(N_QKV_GROUPS + c)
        gate_ref[:, c * GROUP_W:(c + 1) * GROUP_W] = (1.0 / (1.0 + jnp.exp(-z))).astype(BF16)


def _front(x2d, vec, win, wuq, wukvk, wukvv, bd64, bdb, cosb, sinb, cosd, sind, seq, tm):
    t_tokens = x2d.shape[0]
    n_pos = seq // tm
    widths = (512, 512, 512, 512, 1024, 1024, 512, 512, 512, 256, 512, 512, 256, 4096)

    def tile(w):
        return pl.BlockSpec((tm, w), lambda i: (i, 0))

    def pos(w):
        return pl.BlockSpec((tm, w), lambda i: (i % n_pos, 0))

    consts = (vec, win, wuq, wukvk, wukvv, bd64, bdb)
    in_specs = [tile(D_MODEL)] + [_const_spec(c.shape) for c in consts] + [pos(LANES)] * 4
    limit = _vmem_limit(sum(_nbytes(c.shape, c.dtype) for c in consts),
                        2 * tm * (D_MODEL * 4 + 4 * LANES * 4 + sum(widths) * 2),
                        12 * tm * D_MODEL * 4)
    return pl.pallas_call(
        _front_kernel,
        grid=(t_tokens // tm,),
        in_specs=in_specs,
        out_specs=[tile(w) for w in widths],
        out_shape=[jax.ShapeDtypeStruct((t_tokens, w), BF16) for w in widths],
        compiler_params=pltpu.CompilerParams(dimension_semantics=("arbitrary",),
                                             vmem_limit_bytes=limit),
        name="front",
    )(x2d, *consts, cosb, sinb, cosd, sind)


def _t5_bucket(rel):
    nb = REL_BUCKETS // 2
    max_exact = nb // 2
    n = jnp.abs(rel)
    nf = jnp.maximum(n, 1).astype(F32)
    large = max_exact + (jnp.log(nf / max_exact) / math.log(REL_MAX_DIST / max_exact)
                         * (nb - max_exact)).astype(jnp.int32)
    large = jnp.minimum(large, nb - 1)
    return jnp.where(rel > 0, nb, 0) + jnp.where(n < max_exact, n, large)


def _table_lookup(bucket, tbl_ref, col):
    acc = jnp.zeros(bucket.shape, F32)
    for b in range(REL_BUCKETS):
        acc = jnp.where(bucket == b, tbl_ref[b, col], acc)
    return acc


def _bias_dense_kernel(tbl_ref, o_ref, *, t):
    h = pl.program_id(0)
    d = pl.program_id(1) - 2
    rel = (d * t + lax.broadcasted_iota(jnp.int32, (t, t), 1)
           - lax.broadcasted_iota(jnp.int32, (t, t), 0))
    o_ref[...] = _table_lookup(_t5_bucket(rel), tbl_ref, h) * LOG2E


def _bias_window_kernel(tbl_ref, o_ref):
    h = pl.program_id(0)
    shape = (LANES, 3 * LANES)
    rel = (lax.broadcasted_iota(jnp.int32, shape, 1) - LANES
           - lax.broadcasted_iota(jnp.int32, shape, 0))
    bias = _table_lookup(_t5_bucket(rel), tbl_ref, A_HEADS + h) * LOG2E
    o_ref[...] = jnp.where(jnp.abs(rel) <= WINDOW, bias, NEG_INF)


def _bias_tiles(rel_bias, t):
    assert t >= LANES
    smem = pl.BlockSpec(memory_space=pltpu.SMEM)
    dense = pl.pallas_call(
        functools.partial(_bias_dense_kernel, t=t),
        grid=(A_HEADS, 5),
        in_specs=[smem],
        out_specs=pl.BlockSpec((None, None, t, t), lambda h, d: (h, d, 0, 0)),
        out_shape=jax.ShapeDtypeStruct((A_HEADS, 5, t, t), F32),
        name="bias_dense",
    )(rel_bias)
    window = pl.pallas_call(
        _bias_window_kernel,
        grid=(C_HEADS,),
        in_specs=[smem],
        out_specs=pl.BlockSpec((None, LANES, 3 * LANES), lambda h: (h, 0, 0)),
        out_shape=jax.ShapeDtypeStruct((C_HEADS, LANES, 3 * LANES), F32),
        name="bias_window",
    )(rel_bias)
    return dense, window


def _pair_attn_kernel(*refs, t, n_chunks, mode, lam_init):
    if mode == "diff":
        (qa_ref, qb_ref, ka_ref, kb_ref, va_ref, vb_ref, bias_ref, lam_ref, gsub_ref, o_ref,
         ma, la, acca, mb, lb, accb) = refs
    else:
        (qa_ref, qb_ref, ka_ref, kb_ref, va_ref, vb_ref, o_ref,
         ma, la, acca, mb, lb, accb) = refs
        bias_ref = None
    i = pl.program_id(2)
    streams = ((qa_ref[...], ka_ref, va_ref, ma, la, acca),
               (qb_ref[...], kb_ref, vb_ref, mb, lb, accb))
    for _, _, _, m, l, acc in streams:
        m[...] = jnp.full(m.shape, NEG_INF, F32)
        l[...] = jnp.zeros(l.shape, F32)
        acc[...] = jnp.zeros(acc.shape, F32)

    def body(c, carry):
        off = pl.multiple_of(c * t, t)
        bias = None if bias_ref is None else bias_ref[jnp.clip(c - i, -2, 2) + 2]
        for q, k_ref, v_ref, m, l, acc in streams:
            s = _dot_nt(q, k_ref[pl.ds(off, t), :])
            if bias is not None:
                s = s + bias
            m_prev = m[...]
            m_new = jnp.maximum(m_prev, jnp.max(s, axis=1, keepdims=True))
            alpha = jnp.exp2(m_prev - m_new)
            p = jnp.exp2(s - m_new)
            l[...] = alpha * l[...] + jnp.sum(p, axis=1, keepdims=True)
            acc[...] = alpha * acc[...] + _dot(p.astype(BF16), v_ref[pl.ds(off, t), :])
            m[...] = m_new
        return carry

    lax.fori_loop(0, n_chunks, body, 0)
    oa = acca[...] / la[...]
    ob = accb[...] / lb[...]
    if mode == "diff":
        lam = lam_ref[...]
        lam_full = (jnp.exp(jnp.sum(lam[0:1] * lam[1:2], keepdims=True))
                    - jnp.exp(jnp.sum(lam[2:3] * lam[3:4], keepdims=True)) + lam_init)
        o = oa - lam_full * ob
        o_ref[...] = (_rms_rows(o, gsub_ref[...]) * (1.0 - lam_init)).astype(BF16)
    else:
        lane = lax.broadcasted_iota(jnp.int32, oa.shape, 1)
        o_ref[...] = jnp.where(lane < HALF, oa, ob).astype(BF16)


def _pair_attention(q, k_a, k_b, v, cols, n_pairs, t, mode, bias=None, lam=None, gsub=None,
                    lam_init=0.0):
    batch, seq, _ = q.shape

    def q_spec(which):
        return pl.BlockSpec((None, t, LANES), lambda b, p, i: (b, i, cols(p)[which]))

    def kv_spec(which):
        return pl.BlockSpec((None, seq, LANES), lambda b, p, i: (b, 0, cols(p)[which]))

    in_specs = [q_spec(0), q_spec(1), kv_spec(2), kv_spec(3), kv_spec(4), kv_spec(4)]
    args = [q, q, k_a, k_b, v, v]
    resident = 4 * 2 * seq * LANES * 2
    if mode == "diff":
        in_specs += [pl.BlockSpec((None, 5, t, t), lambda b, p, i: (p, 0, 0, 0)),
                     _const_spec(lam.shape), _const_spec(gsub.shape)]
        args += [bias, lam, gsub]
        resident += 2 * 5 * t * t * 4
    limit = _vmem_limit(resident, 6 * t * LANES * 2, 4 * t * LANES * 4 + 2 * t * LANES * 4,
                        8 * t * t * 4)
    return pl.pallas_call(
        functools.partial(_pair_attn_kernel, t=t, n_chunks=seq // t, mode=mode,
                          lam_init=lam_init),
        grid=(batch, n_pairs, seq // t),
        in_specs=in_specs,
        out_specs=pl.BlockSpec((None, t, LANES), lambda b, p, i: (b, i, p)),
        out_shape=jax.ShapeDtypeStruct((batch, seq, n_pairs * LANES), BF16),
        scratch_shapes=[pltpu.VMEM((t, 1), F32), pltpu.VMEM((t, 1), F32),
                        pltpu.VMEM((t, LANES), F32)] * 2,
        compiler_params=pltpu.CompilerParams(
            dimension_semantics=("arbitrary", "arbitrary", "arbitrary"),
            vmem_limit_bytes=limit),
        name="attn_" + mode,
    )(*args)


def _window_kernel(sink_ref, q_ref, kp_ref, km_ref, kn_ref, vp_ref, vm_ref, vn_ref, bias_ref,
                   o_ref, *, rows):
    i = pl.program_id(1)
    n_steps = pl.num_programs(1)
    nblk = rows // LANES
    lane = lax.broadcasted_iota(jnp.int32, (LANES, LANES), 1)

    def kv_block(j):
        if j < 0:
            return kp_ref[...], vp_ref[...]
        if j >= nblk:
            return kn_ref[...], vn_ref[...]
        return km_ref[j * LANES:(j + 1) * LANES, :], vm_ref[j * LANES:(j + 1) * LANES, :]

    for r in range(nblk):
        blocks = [kv_block(r + d) for d in (-1, 0, 1)]
        edge = [None, None, None]
        if r == 0:
            edge[0] = jnp.where(i > 0, 0.0, NEG_INF)
        if r == nblk - 1:
            edge[2] = jnp.where(i < n_steps - 1, 0.0, NEG_INF)
        for p in range(C_HEADS // 2):
            g = p // (C_HEADS // C_KV_HEADS // 2)
            q = q_ref[r * LANES:(r + 1) * LANES, p * LANES:(p + 1) * LANES]
            outs = []
            for e in range(2):
                head = 2 * p + e
                kcol = (2 * g + e) * LANES
                pieces = []
                for d in range(3):
                    s = _dot_nt(q, blocks[d][0][:, kcol:kcol + LANES])
                    if edge[d] is not None:
                        s = s + edge[d]
                    pieces.append(s)
                s = jnp.concatenate(pieces, axis=1) + bias_ref[head]
                sink = sink_ref[head] * LOG2E
                m = jnp.maximum(jnp.max(s, axis=1, keepdims=True), sink)
                pr = jnp.exp2(s - m)
                denom = jnp.sum(pr, axis=1, keepdims=True) + jnp.exp2(sink - m)
                pr = pr.astype(BF16)
                o = _dot(pr[:, 0:LANES], blocks[0][1][:, g * LANES:(g + 1) * LANES])
                o = o + _dot(pr[:, LANES:2 * LANES], blocks[1][1][:, g * LANES:(g + 1) * LANES])
                o = o + _dot(pr[:, 2 * LANES:], blocks[2][1][:, g * LANES:(g + 1) * LANES])
                outs.append(o / denom)
            o_ref[r * LANES:(r + 1) * LANES, p * LANES:(p + 1) * LANES] = (
                jnp.where(lane < HALF, outs[0], outs[1]).astype(BF16))


def _window_attention(q, k, v, bias, sink, rows):
    batch, seq, _ = q.shape
    nblk = rows // LANES
    last = seq // LANES - 1
    kw, vw = k.shape[2], v.shape[2]

    def main(w):
        return pl.BlockSpec((None, rows, w), lambda b, i: (b, i, 0))

    def prev(w):
        return pl.BlockSpec((None, LANES, w), lambda b, i: (b, jnp.maximum(i * nblk - 1, 0), 0))

    def nxt(w):
        return pl.BlockSpec((None, LANES, w), lambda b, i: (b, jnp.minimum((i + 1) * nblk, last), 0))

    limit = _vmem_limit(_nbytes(bias.shape, F32) * 2,
                        2 * (rows + 2 * LANES) * (kw + vw) * 2 + 4 * rows * 512 * 2,
                        16 * LANES * 3 * LANES * 4)
    return pl.pallas_call(
        functools.partial(_window_kernel, rows=rows),
        grid=(batch, seq // rows),
        in_specs=[pl.BlockSpec(memory_space=pltpu.SMEM), main(q.shape[2]),
                  prev(kw), main(kw), nxt(kw), prev(vw), main(vw), nxt(vw),
                  _const_spec(bias.shape)],
        out_specs=main(q.shape[2]),
        out_shape=jax.ShapeDtypeStruct(q.shape, BF16),
        compiler_params=pltpu.CompilerParams(dimension_semantics=("arbitrary", "arbitrary"),
                                             vmem_limit_bytes=limit),
        name="attn_window",
    )(sink, q, k, k, k, v, v, v, bias)


def _memkv_kernel(mem_ref, gmem_ref, gk_ref, w_ref, k_ref, v_ref):
    hn = _rms_rows(mem_ref[...], gmem_ref[...]).astype(BF16)
    kv = _dot(hn, w_ref[...])
    width = X_HEADS * X_DH
    gk = gk_ref[...]
    for h in range(X_HEADS):
        sl = slice(h * X_DH, (h + 1) * X_DH)
        k_ref[:, sl] = _rms_rows(kv[:, sl], gk).astype(BF16)
    v_ref[...] = kv[:, width:].astype(BF16)


def _memkv(mem, gmem, gk, w):
    batch, n_mem, _ = mem.shape
    width = X_HEADS * X_DH
    out = pl.BlockSpec((None, n_mem, width), lambda b: (b, 0, 0))
    return pl.pallas_call(
        _memkv_kernel,
        grid=(batch,),
        in_specs=[pl.BlockSpec((None, n_mem, D_MODEL), lambda b: (b, 0, 0)),
                  _const_spec(gmem.shape), _const_spec(gk.shape), _const_spec(w.shape)],
        out_specs=[out, out],
        out_shape=[jax.ShapeDtypeStruct((batch, n_mem, width), BF16)] * 2,
        compiler_params=pltpu.CompilerParams(
            dimension_semantics=("arbitrary",),
            vmem_limit_bytes=_vmem_limit(_nbytes(w.shape, BF16), 4 * n_mem * D_MODEL * 4,
                                         4 * n_mem * 2 * width * 4)),
        name="memkv",
    )(mem, gmem, gk, w)


def _mix_kernel(oa_ref, ob_ref, oc_ref, od_ref, gate_ref, x_ref, wb_ref, wout_ref, gx_ref,
                gq_ref, wxq_ref, kx_ref, vx_ref, wxo_ref, o_ref):
    mix = None
    for m, o_m in enumerate((oa_ref, ob_ref, oc_ref, od_ref)):
        br = _dot(o_m[...], wb_ref[m])
        term = gate_ref[:, m * D_MODEL:(m + 1) * D_MODEL].astype(F32) * br
        mix = term if mix is None else mix + term
    x1 = x_ref[...] + _dot(mix.astype(BF16), wout_ref[...])
    hn = _rms_rows(x1, gx_ref[...]).astype(BF16)
    qx = _dot(hn, wxq_ref[...])
    gq = gq_ref[...]
    heads = []
    for h in range(X_HEADS):
        sl = slice(h * X_DH, (h + 1) * X_DH)
        qh = _rms_rows(qx[:, sl], gq).astype(BF16)
        s = _dot_nt(qh, kx_ref[:, sl])
        p = jnp.exp2(s - jnp.max(s, axis=1, keepdims=True))
        denom = jnp.sum(p, axis=1, keepdims=True)
        heads.append((_dot(p.astype(BF16), vx_ref[:, sl]) / denom).astype(BF16))
    ox = jnp.concatenate(heads, axis=1)
    o_ref[...] = x1 + _dot(ox, wxo_ref[...])


def _mix(o_a, o_b, o_c, o_d, gates, x2d, wb, wout, gx, gq, wxq, kx, vx, wxo, seq, tm):
    t_tokens = x2d.shape[0]
    per_batch = seq // tm
    n_mem, width = kx.shape[1], kx.shape[2]

    def tile(w):
        return pl.BlockSpec((tm, w), lambda i: (i, 0))

    mem_spec = pl.BlockSpec((None, n_mem, width), lambda i: (i // per_batch, 0, 0))
    consts = (wb, wout, gx, gq, wxq, wxo)
    limit = _vmem_limit(sum(_nbytes(c.shape, c.dtype) for c in consts),
                        2 * tm * (4 * BRANCH_W * 2 + 4 * D_MODEL * 2 + 2 * D_MODEL * 4),
                        4 * n_mem * width * 2, 10 * tm * D_MODEL * 4)
    return pl.pallas_call(
        _mix_kernel,
        grid=(t_tokens // tm,),
        in_specs=[tile(BRANCH_W)] * 4 + [tile(N_BRANCH * D_MODEL), tile(D_MODEL),
                  _const_spec(wb.shape), _const_spec(wout.shape), _const_spec(gx.shape),
                  _const_spec(gq.shape), _const_spec(wxq.shape), mem_spec, mem_spec,
                  _const_spec(wxo.shape)],
        out_specs=tile(D_MODEL),
        out_shape=jax.ShapeDtypeStruct((t_tokens, D_MODEL), F32),
        compiler_params=pltpu.CompilerParams(dimension_semantics=("arbitrary",),
                                             vmem_limit_bytes=limit),
        name="mix",
    )(o_a, o_b, o_c, o_d, gates, x2d, wb, wout, gx, gq, wxq, kx, vx, wxo)


def _mlp_kernel(x_ref, g_ref, wup_ref, wdown_ref, o_ref, *, chunk):
    x = x_ref[...]
    hn = _rms_rows(x, g_ref[...]).astype(BF16)
    acc = x
    for c in range(D_FF // chunk):
        u = jnp.maximum(_dot(hn, wup_ref[:, c * chunk:(c + 1) * chunk]), 0.0)
        acc = acc + _dot((u * u).astype(BF16), wdown_ref[c * chunk:(c + 1) * chunk, :])
    o_ref[...] = acc


def _mlp(x2d, g, wup, wdown, tm, chunk=512):
    t_tokens = x2d.shape[0]
    tile = pl.BlockSpec((tm, D_MODEL), lambda i: (i, 0))
    limit = _vmem_limit(_nbytes(wup.shape, BF16) + _nbytes(wdown.shape, BF16),
                        4 * tm * D_MODEL * 4, 8 * tm * max(chunk, D_MODEL) * 4)
    return pl.pallas_call(
        functools.partial(_mlp_kernel, chunk=chunk),
        grid=(t_tokens // tm,),
        in_specs=[tile, _const_spec(g.shape), _const_spec(wup.shape), _const_spec(wdown.shape)],
        out_specs=tile,
        out_shape=jax.ShapeDtypeStruct((t_tokens, D_MODEL), F32),
        compiler_params=pltpu.CompilerParams(dimension_semantics=("arbitrary",),
                                             vmem_limit_bytes=limit),
        name="mlp",
    )(x2d, g, wup, wdown)


def _block_diag_ones(sizes):
    m = np.zeros((SEG, SEG), np.float32)
    start = 0
    for size, on in sizes:
        if on:
            m[start:start + size, start:start + size] = 1.0
        start += size
    assert start == SEG
    return jnp.asarray(m, BF16)


def _rope_tables(seq):
    pos = jnp.arange(seq)
    inv = ROPE_THETA ** (-jnp.arange(0, B_ROPE, 2, dtype=F32) / B_ROPE)

    def cs(p):
        ang = p.astype(F32)[:, None] * inv[None, :]
        return (jnp.concatenate([jnp.cos(ang)] * 2, -1), jnp.concatenate([jnp.sin(ang)] * 2, -1))

    cos_p, sin_p = cs(pos)
    one = jnp.ones((seq, 1), F32)
    zero = jnp.zeros((seq, 1), F32)
    cosb = jnp.concatenate([one * jnp.ones((1, HALF)), cos_p, one * jnp.ones((1, 32))], 1)
    sinb = jnp.concatenate([zero * jnp.ones((1, HALF)), sin_p, zero * jnp.ones((1, 32))], 1)
    cos_r, sin_r = cs(pos // GRID_W)
    cos_c, sin_c = cs(pos % GRID_W)
    cosd = jnp.concatenate([cos_r, cos_c] * 2, 1)
    sind = jnp.concatenate([sin_r, sin_c] * 2, 1)
    return cosb, sinb, cosd, sind


def _layout_w_in(w):
    off = np.concatenate([[0], np.cumsum(IN_SIZES)])

    def seg(i):
        return w[:, off[i]:off[i + 1]]

    def zeros(n):
        return jnp.zeros((w.shape[0], n), w.dtype)

    cols = [seg(0), seg(1), seg(2), seg(3), seg(4), zeros(HALF), seg(5), zeros(32),
            seg(6), seg(7), seg(8), seg(10), seg(11), seg(9), seg(12)]
    out = jnp.concatenate(cols, axis=1).astype(BF16)
    assert out.shape[1] == IN_WIDTH_PADDED
    return out


def _pad_row(v):
    return jnp.pad(v, (0, D_MODEL - v.shape[0]))


def _vector_table(g_mix, g_qk_a, g_cq, g_ckv, g_qk_b, g_qk_c, g_qk_d):
    z32 = jnp.zeros((32,), F32)
    z64 = jnp.zeros((HALF,), F32)
    rows = [None] * N_VROWS
    rows[V_GMIX] = g_mix
    rows[V_AQ] = jnp.tile(g_qk_a[0], 8) * (A_DK ** -0.5 * LOG2E)
    rows[V_AK] = jnp.tile(g_qk_a[1], 8)
    rows[V_GCQ] = g_cq
    rows[V_GCKV] = g_ckv
    rows[V_QB] = jnp.tile(jnp.concatenate([g_qk_b[0], z32]), B_HEADS) * (
        (B_NOPE + B_ROPE) ** -0.5 * LOG2E)
    rows[V_QB_CNT] = jnp.tile(jnp.concatenate(
        [jnp.full((B_NOPE,), 1.0 / B_NOPE), jnp.full((B_ROPE,), 1.0 / B_ROPE), jnp.ones((32,))]),
        B_HEADS)
    rows[V_KB] = jnp.tile(jnp.concatenate([g_qk_b[1, :B_NOPE], z64]), B_HEADS)
    rows[V_KR] = jnp.concatenate([z64, g_qk_b[1, B_NOPE:], z32])
    rows[V_CQ] = jnp.tile(g_qk_c[0], 8) * (C_DH ** -0.5 * LOG2E)
    ones = jnp.ones((LANES,), F32)
    rows[V_G5] = jnp.concatenate([jnp.tile(g_qk_c[1], 2), ones, jnp.tile(g_qk_d[1], 2), ones])
    rows[V_DQ] = jnp.tile(g_qk_d[0], 8) * (D_DH ** -0.5 * LOG2E)
    return jnp.stack([_pad_row(r.astype(F32)) for r in rows])


def _forward(x, mem, rel_bias, g_mix, w_in, lam, g_qk_a, g_sub_a, g_cq, g_ckv, w_uq, w_ukv,
             g_qk_b, g_qk_c, sink_c, g_qk_d, w_branch, w_out, g_x, g_mem, w_xq, w_xkv,
             g_qk_x, w_xo, g_mlp, w_up, w_down, *, t_attn, tm, rows_c):
    batch, seq, _ = x.shape
    depth = w_in.shape[0]
    assert seq % t_attn == 0 and seq % tm == 0 and seq % rows_c == 0 and seq % GRID_W == 0

    bd64 = _block_diag_ones([(HALF, True)] * 4)
    bdb = _block_diag_ones([(B_NOPE, True), (B_ROPE, True), (32, False)] * 2)
    cosb, sinb, cosd, sind = _rope_tables(seq)
    bias_a, bias_c = _bias_tiles(rel_bias, t_attn)

    def cols_a(p):
        return (p, p, p, p, p)

    def cols_b(p):
        return (2 * p, 2 * p + 1, 2 * p, 2 * p + 1, p)

    def cols_d(p):
        g = p // 2
        return (p, p, 2 * g, 2 * g + 1, g)

    x2d = x.reshape(batch * seq, D_MODEL)
    for layer in range(depth):
        lam_init = 0.8 - 0.6 * math.exp(-0.3 * layer)
        vec = _vector_table(g_mix[layer], g_qk_a[layer], g_cq[layer], g_ckv[layer],
                            g_qk_b[layer], g_qk_c[layer], g_qk_d[layer])
        wuq = jnp.pad(w_uq[layer].reshape(B_Q_LORA, B_HEADS, B_NOPE + B_ROPE),
                      ((0, 0), (0, 0), (0, 32))).reshape(B_Q_LORA, B_HEADS * LANES).astype(BF16)
        wukv = w_ukv[layer].reshape(B_KV_LORA, B_HEADS, B_NOPE + B_DV)
        wukvk = jnp.pad(wukv[:, :, :B_NOPE], ((0, 0), (0, 0), (0, HALF))).reshape(
            B_KV_LORA, B_HEADS * LANES).astype(BF16)
        wukvv = wukv[:, :, B_NOPE:].reshape(B_KV_LORA, B_HEADS * B_DV).astype(BF16)
        (qa, ka1, ka2, va, qb, kb, vb, qc, kc, vc, qd, kd, vd, gates) = _front(
            x2d, vec, _layout_w_in(w_in[layer]), wuq, wukvk, wukvv, bd64, bdb,
            cosb, sinb, cosd, sind, seq, tm)

        def b3(a):
            return a.reshape(batch, seq, a.shape[1])

        o_a = _pair_attention(b3(qa), b3(ka1), b3(ka2), b3(va), cols_a, A_HEADS, t_attn, "diff",
                              bias=bias_a, lam=lam[layer], gsub=g_sub_a[layer][None, :],
                              lam_init=lam_init)
        o_b = _pair_attention(b3(qb), b3(kb), b3(kb), b3(vb), cols_b, B_HEADS // 2, t_attn,
                              "merge")
        o_c = _window_attention(b3(qc), b3(kc), b3(vc), bias_c, sink_c[layer], rows_c)
        o_d = _pair_attention(b3(qd), b3(kd), b3(kd), b3(vd), cols_d, D_HEADS // 2, t_attn,
                              "merge")

        kx, vx = _memkv(mem, g_mem[layer][None, :], g_qk_x[layer][1][None, :],
                        w_xkv[layer].astype(BF16))
        t2 = batch * seq
        x2d = _mix(o_a.reshape(t2, -1), o_b.reshape(t2, -1), o_c.reshape(t2, -1),
                   o_d.reshape(t2, -1), gates, x2d, w_branch[layer].astype(BF16),
                   w_out[layer].astype(BF16), g_x[layer][None, :],
                   (g_qk_x[layer][0] * (X_DH ** -0.5 * LOG2E))[None, :],
                   w_xq[layer].astype(BF16), kx, vx, w_xo[layer].astype(BF16), seq, tm)
        x2d = _mlp(x2d, g_mlp[layer][None, :], w_up[layer].astype(BF16),
                   w_down[layer].astype(BF16), tm)
    return x2d.reshape(batch, seq, D_MODEL)


def kernel(x, mem, rel_bias, g_mix, w_in, lam, g_qk_a, g_sub_a, g_cq, g_ckv, w_uq, w_ukv, g_qk_b, g_qk_c, sink_c, g_qk_d, w_branch, w_out, g_x, g_mem, w_xq, w_xkv, g_qk_x, w_xo, g_mlp, w_up, w_down):
    return _forward(x, mem, rel_bias, g_mix, w_in, lam, g_qk_a, g_sub_a, g_cq, g_ckv, w_uq, w_ukv,
                    g_qk_b, g_qk_c, sink_c, g_qk_d, w_branch, w_out, g_x, g_mem, w_xq, w_xkv,
                    g_qk_x, w_xo, g_mlp, w_up, w_down, t_attn=512, tm=256, rows_c=512)
```

```python
import functools
import math

import numpy as np
import jax
import jax.numpy as jnp
from jax import lax
from jax.experimental import pallas as pl
from jax.experimental.pallas import tpu as pltpu

F32 = jnp.float32
BF16 = jnp.bfloat16

D_MODEL = 1024
GRID_W = 64
N_MEM = 256
EPS = 1e-6
ROPE_THETA = 10000.0
NEG_INF = -1e30
LOG2E = 1.4426950408889634
SAFE_SCORE = 60.0

A_HEADS, A_DK, A_DV = 4, 64, 128
B_HEADS, B_Q_LORA, B_KV_LORA, B_NOPE, B_ROPE, B_DV = 8, 256, 128, 64, 32, 64
C_HEADS, C_KV_HEADS, C_DH, WINDOW = 8, 2, 64, 128
D_HEADS, D_KV_HEADS, D_DH = 8, 2, 64
X_HEADS, X_DH = 4, 128
D_FF = 4 * D_MODEL
REL_BUCKETS, REL_MAX_DIST = 32, 128
N_BRANCH, BRANCH_W = 4, 512

IN_SIZES = (512, 512, 512, B_Q_LORA, B_KV_LORA, B_ROPE, 512, 128, 128, 512, 128, 128,
            N_BRANCH * D_MODEL)
GROUP_W = 512
N_QKV_GROUPS = 7
N_GATE_GROUPS = N_BRANCH * D_MODEL // GROUP_W
IN_WIDTH_PADDED = (N_QKV_GROUPS + N_GATE_GROUPS) * GROUP_W

LANES = 128
HALF = 64
SEG = 256
VMEM_CAP = 60 * 1024 * 1024

(V_GMIX, V_AQ, V_AK, V_GCQ, V_GCKV, V_QB, V_QB_CNT, V_KB, V_KR, V_CQ, V_G5, V_DQ,
 N_VROWS) = range(13)


def _vmem_limit(*nbytes):
    return int(min(VMEM_CAP, sum(nbytes) + (6 << 20)))


def _const_spec(shape):
    zeros = (0,) * len(shape)
    return pl.BlockSpec(shape, lambda *_: zeros, pipeline_mode=pl.Buffered(1))


def _nbytes(shape, dtype):
    return int(np.prod(shape)) * jnp.dtype(dtype).itemsize


def _dot(a, b):
    return jnp.dot(a, b, preferred_element_type=F32)


def _dot_nt(a, b):
    return lax.dot_general(a, b, (((1,), (1,)), ((), ())), preferred_element_type=F32)


def _rms_rows(x, g):
    return x * lax.rsqrt(jnp.mean(x * x, axis=-1, keepdims=True) + EPS) * g


def _seg_rms(y, bd, inv_cnt, gain):
    outs = []
    for c in range(y.shape[1] // SEG):
        sl = slice(c * SEG, (c + 1) * SEG)
        yc = y[:, sl]
        y2 = yc * yc
        hi = y2.astype(BF16)
        lo = (y2 - hi.astype(F32)).astype(BF16)
        ssq = _dot(hi, bd) + _dot(lo, bd)
        outs.append(yc * lax.rsqrt(ssq * inv_cnt[:, sl] + EPS) * gain[:, sl])
    return outs[0] if len(outs) == 1 else jnp.concatenate(outs, axis=1)


def _rope32(x, cos, sin):
    w = x.shape[1]
    lane = lax.broadcasted_iota(jnp.int32, x.shape, 1)
    first = (lane & 16) == 0
    rot = jnp.where(first, -pltpu.roll(x, w - 16, 1), pltpu.roll(x, 16, 1))
    return x * cos + rot * sin


def _tile_lanes(v, n):
    return v if n == 1 else jnp.concatenate([v] * n, axis=1)


def _pair_split(x, fill):
    lane = lax.broadcasted_iota(jnp.int32, x.shape, 1)
    low = lane < HALF
    sw = pltpu.roll(x, HALF, 1)
    return (jnp.where(low, x, fill), jnp.where(low, fill, sw), jnp.where(low, sw, fill),
            jnp.where(low, fill, x))


def _front_kernel(x_ref, vec_ref, win_ref, wuq_ref, wukvk_ref, wukvv_ref, bd64_ref, bdb_ref,
                  cosb_ref, sinb_ref, cosd_ref, sind_ref,
                  qa_ref, ka1_ref, ka2_ref, va_ref, qb_ref, kb_ref, vb_ref,
                  qc_ref, kc_ref, vc_ref, qd_ref, kd_ref, vd_ref, gate_ref):
    def row(r, w):
        return vec_ref[r:r + 1, :w]

    ones = jnp.ones((1, D_MODEL), F32)
    inv64 = ones * (1.0 / HALF)
    bd64 = bd64_ref[...]
    h = _rms_rows(x_ref[...], row(V_GMIX, D_MODEL)).astype(BF16)

    def group(g):
        return _dot(h, win_ref[:, g * GROUP_W:(g + 1) * GROUP_W])

    lane512 = lax.broadcasted_iota(jnp.int32, (x_ref.shape[0], GROUP_W), 1)

    qa_ref[...] = _seg_rms(group(0), bd64, inv64, row(V_AQ, GROUP_W)).astype(BF16)
    ka = _seg_rms(group(1), bd64, inv64, row(V_AK, GROUP_W))
    first_map = (lane512 & HALF) == 0
    ka1_ref[...] = jnp.where(first_map, ka, 0.0).astype(BF16)
    ka2_ref[...] = jnp.where(first_map, 0.0, ka).astype(BF16)
    va = group(2)
    ones_blk = jnp.ones((va.shape[0], LANES), F32)
    va_ref[...] = jnp.concatenate(
        [blk for hh in range(A_HEADS) for blk in (va[:, hh * LANES:(hh + 1) * LANES], ones_blk)],
        axis=1).astype(BF16)

    y = group(3)
    cosb = cosb_ref[...]
    sinb = sinb_ref[...]
    cq = _rms_rows(y[:, :B_Q_LORA], row(V_GCQ, B_Q_LORA)).astype(BF16)
    qb = _seg_rms(_dot(cq, wuq_ref[...]), bdb_ref[...], row(V_QB_CNT, D_MODEL), row(V_QB, D_MODEL))
    qb_ref[...] = _rope32(qb, _tile_lanes(cosb, B_HEADS), _tile_lanes(sinb, B_HEADS)).astype(BF16)
    ckv = _rms_rows(y[:, B_Q_LORA:B_Q_LORA + B_KV_LORA], row(V_GCKV, B_KV_LORA)).astype(BF16)
    k_nope = _seg_rms(_dot(ckv, wukvk_ref[...]), bd64, inv64, row(V_KB, D_MODEL))
    kr = y[:, B_Q_LORA + B_KV_LORA:]
    kr = kr * lax.rsqrt(jnp.sum(kr * kr, axis=-1, keepdims=True) * (1.0 / B_ROPE) + EPS)
    kr = _rope32(kr * row(V_KR, LANES), cosb, sinb)
    kb_ref[...] = (k_nope + _tile_lanes(kr, B_HEADS)).astype(BF16)
    vb = _dot(ckv, wukvv_ref[...])
    lane512 = lax.broadcasted_iota(jnp.int32, vb.shape, 1)
    low512 = (lane512 & HALF) == 0
    vb_ref[...] = jnp.concatenate(
        [blk for pp in range(B_HEADS // 2) for blk in (
            jnp.where(low512, vb, 1.0)[:, pp * LANES:(pp + 1) * LANES],
            jnp.where(low512, 1.0, vb)[:, pp * LANES:(pp + 1) * LANES])], axis=1).astype(BF16)

    cosd = cosd_ref[...]
    sind = sind_ref[...]
    y = group(5)
    yn = _seg_rms(y, bd64, inv64, row(V_G5, GROUP_W))
    kc = yn[:, 0:LANES]
    vc = y[:, LANES:2 * LANES]
    kd = _rope32(yn[:, 2 * LANES:3 * LANES], cosd, sind)
    vd = y[:, 3 * LANES:]
    kc_ref[...] = jnp.concatenate(_pair_split(kc, 0.0), axis=1).astype(BF16)
    kd_ref[...] = jnp.concatenate(_pair_split(kd, 0.0), axis=1).astype(BF16)
    low = lax.broadcasted_iota(jnp.int32, vc.shape, 1) < HALF
    vc_sw = pltpu.roll(vc, HALF, 1)
    vc_ref[...] = jnp.concatenate([jnp.where(low, vc, vc_sw), jnp.where(low, vc_sw, vc)],
                                  axis=1).astype(BF16)
    vd_ref[...] = jnp.concatenate(_pair_split(vd, 1.0), axis=1).astype(BF16)

    qc_ref[...] = _seg_rms(group(4), bd64, inv64, row(V_CQ, GROUP_W)).astype(BF16)
    qd = _seg_rms(group(6), bd64, inv64, row(V_DQ, GROUP_W))
    qd_ref[...] = _rope32(qd, _tile_lanes(cosd, 4), _tile_lanes(sind, 4)).astype(BF16)

    for c in range(N_GATE_GROUPS):
        z = group(N_QKV_GROUPS + c)
        gate_ref[:, c * GROUP_W:(c + 1) * GROUP_W] = (1.0 / (1.0 + jnp.exp(-z))).astype(BF16)


def _front(x2d, vec, win, wuq, wukvk, wukvv, bd64, bdb, cosb, sinb, cosd, sind, seq, tm):
    t_tokens = x2d.shape[0]
    n_pos = seq // tm
    widths = (512, 512, 512, 1024, 1024, 1024, 1024, 512, 512, 256, 512, 512, 512, 4096)

    def tile(w):
        return pl.BlockSpec((tm, w), lambda i: (i, 0))

    def pos(w):
        return pl.BlockSpec((tm, w), lambda i: (i % n_pos, 0))

    consts = (vec, win, wuq, wukvk, wukvv, bd64, bdb)
    in_specs = [tile(D_MODEL)] + [_const_spec(c.shape) for c in consts] + [pos(LANES)] * 4
    limit = _vmem_limit(sum(_nbytes(c.shape, c.dtype) for c in consts),
                        2 * tm * (D_MODEL * 4 + 4 * LANES * 4 + sum(widths) * 2),
                        12 * tm * D_MODEL * 4)
    return pl.pallas_call(
        _front_kernel,
        grid=(t_tokens // tm,),
        in_specs=in_specs,
        out_specs=[tile(w) for w in widths],
        out_shape=[jax.ShapeDtypeStruct((t_tokens, w), BF16) for w in widths],
        compiler_params=pltpu.CompilerParams(dimension_semantics=("arbitrary",),
                                             vmem_limit_bytes=limit),
        name="front",
    )(x2d, *consts, cosb, sinb, cosd, sind)


def _t5_bucket(rel):
    nb = REL_BUCKETS // 2
    max_exact = nb // 2
    n = jnp.abs(rel)
    nf = jnp.maximum(n, 1).astype(F32)
    large = max_exact + (jnp.log(nf / max_exact) / math.log(REL_MAX_DIST / max_exact)
                         * (nb - max_exact)).astype(jnp.int32)
    large = jnp.minimum(large, nb - 1)
    return jnp.where(rel > 0, nb, 0) + jnp.where(n < max_exact, n, large)


def _table_lookup(bucket, tbl_ref, col):
    acc = jnp.zeros(bucket.shape, F32)
    for b in range(REL_BUCKETS):
        acc = jnp.where(bucket == b, tbl_ref[b, col], acc)
    return acc


def _bias_dense_kernel(tbl_ref, o_ref, *, t):
    h = pl.program_id(0)
    d = pl.program_id(1) - 2
    rel = (d * t + lax.broadcasted_iota(jnp.int32, (t, t), 1)
           - lax.broadcasted_iota(jnp.int32, (t, t), 0))
    o_ref[...] = _table_lookup(_t5_bucket(rel), tbl_ref, h) * LOG2E


def _bias_window_kernel(tbl_ref, o_ref):
    h = pl.program_id(0)
    shape = (LANES, 3 * LANES)
    rel = (lax.broadcasted_iota(jnp.int32, shape, 1) - LANES
           - lax.broadcasted_iota(jnp.int32, shape, 0))
    bias = _table_lookup(_t5_bucket(rel), tbl_ref, A_HEADS + h) * LOG2E
    o_ref[...] = jnp.where(jnp.abs(rel) <= WINDOW, bias, NEG_INF)


def _bias_tiles(rel_bias, t):
    assert t >= LANES
    smem = pl.BlockSpec(memory_space=pltpu.SMEM)
    dense = pl.pallas_call(
        functools.partial(_bias_dense_kernel, t=t),
        grid=(A_HEADS, 5),
        in_specs=[smem],
        out_specs=pl.BlockSpec((None, None, t, t), lambda h, d: (h, d, 0, 0)),
        out_shape=jax.ShapeDtypeStruct((A_HEADS, 5, t, t), F32),
        name="bias_dense",
    )(rel_bias)
    window = pl.pallas_call(
        _bias_window_kernel,
        grid=(C_HEADS,),
        in_specs=[smem],
        out_specs=pl.BlockSpec((None, LANES, 3 * LANES), lambda h: (h, 0, 0)),
        out_shape=jax.ShapeDtypeStruct((C_HEADS, LANES, 3 * LANES), F32),
        name="bias_window",
    )(rel_bias)
    return dense, window


def _pair_attn_kernel(*refs, t, n_chunks, mode, lam_init):
    if mode == "diff":
        (qa_ref, qb_ref, ka_ref, kb_ref, va_ref, vb_ref, bias_ref, lam_ref, gsub_ref, o_ref,
         ma, la, acca, mb, lb, accb) = refs
    else:
        (qa_ref, qb_ref, ka_ref, kb_ref, va_ref, vb_ref, o_ref,
         ma, la, acca, mb, lb, accb) = refs
        bias_ref = None
    i = pl.program_id(2)
    streams = ((qa_ref[...], ka_ref, va_ref, ma, la, acca),
               (qb_ref[...], kb_ref, vb_ref, mb, lb, accb))
    for _, _, _, m, l, acc in streams:
        m[...] = jnp.full(m.shape, NEG_INF, F32)
        l[...] = jnp.zeros(l.shape, F32)
        acc[...] = jnp.zeros(acc.shape, F32)

    def body(c, carry):
        off = pl.multiple_of(c * t, t)
        bias = None if bias_ref is None else bias_ref[jnp.clip(c - i, -2, 2) + 2]
        for q, k_ref, v_ref, m, l, acc in streams:
            s = _dot_nt(q, k_ref[pl.ds(off, t), :])
            if bias is not None:
                s = s + bias
            m_prev = m[...]
            m_new = jnp.maximum(m_prev, jnp.max(s, axis=1, keepdims=True))
            alpha = jnp.exp2(m_prev - m_new)
            p = jnp.exp2(s - m_new)
            l[...] = alpha * l[...] + jnp.sum(p, axis=1, keepdims=True)
            acc[...] = alpha * acc[...] + _dot(p.astype(BF16), v_ref[pl.ds(off, t), :])
            m[...] = m_new
        return carry

    lax.fori_loop(0, n_chunks, body, 0)
    _finish_pair(acca[...] / la[...], accb[...] / lb[...], mode, refs, lam_init)


def _finish_pair(oa, ob, mode, refs, lam_init):
    if mode == "diff":
        lam_ref, gsub_ref, o_ref = refs[7:10]
        lam = lam_ref[...]
        lam_full = (jnp.exp(jnp.sum(lam[0:1] * lam[1:2], keepdims=True))
                    - jnp.exp(jnp.sum(lam[2:3] * lam[3:4], keepdims=True)) + lam_init)
        o = oa - lam_full * ob
        o_ref[...] = (_rms_rows(o, gsub_ref[...]) * (1.0 - lam_init)).astype(BF16)
    else:
        o_ref = refs[6]
        lane = lax.broadcasted_iota(jnp.int32, oa.shape, 1)
        o_ref[...] = jnp.where(lane < HALF, oa, ob).astype(BF16)


def _pair_attn_fast_kernel(*refs, t, n_chunks, mode, lam_init):
    qa_ref, qb_ref, ka_ref, kb_ref, va_ref, vb_ref = refs[:6]
    bias_ref = refs[6] if mode == "diff" else None
    acca, accb = refs[-2:]
    i = pl.program_id(2)
    streams = ((qa_ref[...], ka_ref, va_ref, acca), (qb_ref[...], kb_ref, vb_ref, accb))
    for _, _, _, acc in streams:
        acc[...] = jnp.zeros(acc.shape, F32)

    def body(c, carry):
        off = pl.multiple_of(c * t, t)
        bias = None if bias_ref is None else bias_ref[jnp.clip(c - i, -2, 2) + 2]
        for q, k_ref, v_ref, acc in streams:
            s = _dot_nt(q, k_ref[pl.ds(off, t), :])
            if bias is not None:
                s = s + bias
            acc[...] += _dot(jnp.exp2(s).astype(BF16), v_ref[pl.ds(off, t), :])
        return carry

    lax.fori_loop(0, n_chunks, body, 0)
    a = acca[...]
    b = accb[...]
    if mode == "diff":
        oa = a[:, :LANES] / a[:, LANES:]
        ob = b[:, :LANES] / b[:, LANES:]
    else:
        oa = a / pltpu.roll(a, HALF, 1)
        ob = b / pltpu.roll(b, HALF, 1)
    _finish_pair(oa, ob, mode, refs, lam_init)


def _pair_attention(q, k_a, k_b, v, cols, n_pairs, t, mode, fast, bias=None, lam=None,
                    gsub=None, lam_init=0.0):
    batch, seq, _ = q.shape
    v_w = 2 * LANES if (fast and mode == "diff") else LANES

    def q_spec(which):
        return pl.BlockSpec((None, t, LANES), lambda b, p, i: (b, i, cols(p)[which]))

    def kv_spec(which, w=LANES):
        return pl.BlockSpec((None, seq, w), lambda b, p, i: (b, 0, cols(p)[which] * LANES // w))

    in_specs = [q_spec(0), q_spec(1), kv_spec(2), kv_spec(3), kv_spec(4, v_w), kv_spec(5, v_w)]
    args = [q, q, k_a, k_b, v, v]
    resident = 2 * 2 * seq * (LANES + v_w) * 2
    if mode == "diff":
        in_specs += [pl.BlockSpec((None, 5, t, t), lambda b, p, i: (p, 0, 0, 0)),
                     _const_spec(lam.shape), _const_spec(gsub.shape)]
        args += [bias, lam, gsub]
        resident += 2 * 5 * t * t * 4
    if fast:
        body = _pair_attn_fast_kernel
        scratch = [pltpu.VMEM((t, v_w), F32)] * 2
    else:
        body = _pair_attn_kernel
        scratch = [pltpu.VMEM((t, 1), F32), pltpu.VMEM((t, 1), F32),
                   pltpu.VMEM((t, LANES), F32)] * 2
    limit = _vmem_limit(resident, 6 * t * LANES * 2, 6 * t * v_w * 4, 8 * t * t * 4)
    return pl.pallas_call(
        functools.partial(body, t=t, n_chunks=seq // t, mode=mode, lam_init=lam_init),
        grid=(batch, n_pairs, seq // t),
        in_specs=in_specs,
        out_specs=pl.BlockSpec((None, t, LANES), lambda b, p, i: (b, i, p)),
        out_shape=jax.ShapeDtypeStruct((batch, seq, n_pairs * LANES), BF16),
        scratch_shapes=scratch,
        compiler_params=pltpu.CompilerParams(
            dimension_semantics=("arbitrary", "arbitrary", "arbitrary"),
            vmem_limit_bytes=limit),
        name=("attn_fast_" if fast else "attn_") + mode,
    )(*args)


def _dense_attention(score_bound, *args, **kwargs):
    return lax.cond(score_bound <= SAFE_SCORE,
                    lambda: _pair_attention(*args, fast=True, **kwargs),
                    lambda: _pair_attention(*args, fast=False, **kwargs))


def _window_kernel(sink_ref, q_ref, kp_ref, km_ref, kn_ref, vp_ref, vm_ref, vn_ref, bias_ref,
                   o_ref, *, rows):
    i = pl.program_id(1)
    n_steps = pl.num_programs(1)
    nblk = rows // LANES
    lane = lax.broadcasted_iota(jnp.int32, (LANES, LANES), 1)

    def kv_block(j):
        if j < 0:
            return kp_ref[...], vp_ref[...]
        if j >= nblk:
            return kn_ref[...], vn_ref[...]
        return km_ref[j * LANES:(j + 1) * LANES, :], vm_ref[j * LANES:(j + 1) * LANES, :]

    for r in range(nblk):
        blocks = [kv_block(r + d) for d in (-1, 0, 1)]
        edge = [None, None, None]
        if r == 0:
            edge[0] = jnp.where(i > 0, 0.0, NEG_INF)
        if r == nblk - 1:
            edge[2] = jnp.where(i < n_steps - 1, 0.0, NEG_INF)
        for p in range(C_HEADS // 2):
            g = p // (C_HEADS // C_KV_HEADS // 2)
            q = q_ref[r * LANES:(r + 1) * LANES, p * LANES:(p + 1) * LANES]
            outs = []
            for e in range(2):
                head = 2 * p + e
                kcol = (2 * g + e) * LANES
                pieces = []
                for d in range(3):
                    s = _dot_nt(q, blocks[d][0][:, kcol:kcol + LANES])
                    if edge[d] is not None:
                        s = s + edge[d]
                    pieces.append(s)
                s = jnp.concatenate(pieces, axis=1) + bias_ref[head]
                sink = sink_ref[head] * LOG2E
                m = jnp.maximum(jnp.max(s, axis=1, keepdims=True), sink)
                pr = jnp.exp2(s - m)
                denom = jnp.sum(pr, axis=1, keepdims=True) + jnp.exp2(sink - m)
                pr = pr.astype(BF16)
                o = _dot(pr[:, 0:LANES], blocks[0][1][:, g * LANES:(g + 1) * LANES])
                o = o + _dot(pr[:, LANES:2 * LANES], blocks[1][1][:, g * LANES:(g + 1) * LANES])
                o = o + _dot(pr[:, 2 * LANES:], blocks[2][1][:, g * LANES:(g + 1) * LANES])
                outs.append(o / denom)
            o_ref[r * LANES:(r + 1) * LANES, p * LANES:(p + 1) * LANES] = (
                jnp.where(lane < HALF, outs[0], outs[1]).astype(BF16))


def _window_attention(q, k, v, bias, sink, rows):
    batch, seq, _ = q.shape
    nblk = rows // LANES
    last = seq // LANES - 1
    kw, vw = k.shape[2], v.shape[2]

    def main(w):
        return pl.BlockSpec((None, rows, w), lambda b, i: (b, i, 0))

    def prev(w):
        return pl.BlockSpec((None, LANES, w), lambda b, i: (b, jnp.maximum(i * nblk - 1, 0), 0))

    def nxt(w):
        return pl.BlockSpec((None, LANES, w), lambda b, i: (b, jnp.minimum((i + 1) * nblk, last), 0))

    limit = _vmem_limit(_nbytes(bias.shape, F32) * 2,
                        2 * (rows + 2 * LANES) * (kw + vw) * 2 + 4 * rows * 512 * 2,
                        16 * LANES * 3 * LANES * 4)
    return pl.pallas_call(
        functools.partial(_window_kernel, rows=rows),
        grid=(batch, seq // rows),
        in_specs=[pl.BlockSpec(memory_space=pltpu.SMEM), main(q.shape[2]),
                  prev(kw), main(kw), nxt(kw), prev(vw), main(vw), nxt(vw),
                  _const_spec(bias.shape)],
        out_specs=main(q.shape[2]),
        out_shape=jax.ShapeDtypeStruct(q.shape, BF16),
        compiler_params=pltpu.CompilerParams(dimension_semantics=("arbitrary", "arbitrary"),
                                             vmem_limit_bytes=limit),
        name="attn_window",
    )(sink, q, k, k, k, v, v, v, bias)


def _memkv_kernel(mem_ref, gmem_ref, gk_ref, w_ref, k_ref, v_ref):
    hn = _rms_rows(mem_ref[...], gmem_ref[...]).astype(BF16)
    kv = _dot(hn, w_ref[...])
    width = X_HEADS * X_DH
    gk = gk_ref[...]
    for h in range(X_HEADS):
        sl = slice(h * X_DH, (h + 1) * X_DH)
        k_ref[:, sl] = _rms_rows(kv[:, sl], gk).astype(BF16)
    v_ref[...] = kv[:, width:].astype(BF16)


def _memkv(mem, gmem, gk, w):
    batch, n_mem, _ = mem.shape
    width = X_HEADS * X_DH
    out = pl.BlockSpec((None, n_mem, width), lambda b: (b, 0, 0))
    return pl.pallas_call(
        _memkv_kernel,
        grid=(batch,),
        in_specs=[pl.BlockSpec((None, n_mem, D_MODEL), lambda b: (b, 0, 0)),
                  _const_spec(gmem.shape), _const_spec(gk.shape), _const_spec(w.shape)],
        out_specs=[out, out],
        out_shape=[jax.ShapeDtypeStruct((batch, n_mem, width), BF16)] * 2,
        compiler_params=pltpu.CompilerParams(
            dimension_semantics=("arbitrary",),
            vmem_limit_bytes=_vmem_limit(_nbytes(w.shape, BF16), 4 * n_mem * D_MODEL * 4,
                                         4 * n_mem * 2 * width * 4)),
        name="memkv",
    )(mem, gmem, gk, w)


def _mix_kernel(oa_ref, ob_ref, oc_ref, od_ref, gate_ref, x_ref, wb_ref, wout_ref, gx_ref,
                gq_ref, wxq_ref, kx_ref, vx_ref, wxo_ref, o_ref):
    mix = None
    for m, o_m in enumerate((oa_ref, ob_ref, oc_ref, od_ref)):
        br = _dot(o_m[...], wb_ref[m])
        term = gate_ref[:, m * D_MODEL:(m + 1) * D_MODEL].astype(F32) * br
        mix = term if mix is None else mix + term
    x1 = x_ref[...] + _dot(mix.astype(BF16), wout_ref[...])
    hn = _rms_rows(x1, gx_ref[...]).astype(BF16)
    qx = _dot(hn, wxq_ref[...])
    gq = gq_ref[...]
    heads = []
    for h in range(X_HEADS):
        sl = slice(h * X_DH, (h + 1) * X_DH)
        qh = _rms_rows(qx[:, sl], gq).astype(BF16)
        s = _dot_nt(qh, kx_ref[:, sl])
        p = jnp.exp2(s - jnp.max(s, axis=1, keepdims=True))
        denom = jnp.sum(p, axis=1, keepdims=True)
        heads.append((_dot(p.astype(BF16), vx_ref[:, sl]) / denom).astype(BF16))
    ox = jnp.concatenate(heads, axis=1)
    o_ref[...] = x1 + _dot(ox, wxo_ref[...])


def _mix(o_a, o_b, o_c, o_d, gates, x2d, wb, wout, gx, gq, wxq, kx, vx, wxo, seq, tm):
    t_tokens = x2d.shape[0]
    per_batch = seq // tm
    n_mem, width = kx.shape[1], kx.shape[2]

    def tile(w):
        return pl.BlockSpec((tm, w), lambda i: (i, 0))

    mem_spec = pl.BlockSpec((None, n_mem, width), lambda i: (i // per_batch, 0, 0))
    consts = (wb, wout, gx, gq, wxq, wxo)
    limit = _vmem_limit(sum(_nbytes(c.shape, c.dtype) for c in consts),
                        2 * tm * (4 * BRANCH_W * 2 + 4 * D_MODEL * 2 + 2 * D_MODEL * 4),
                        4 * n_mem * width * 2, 10 * tm * D_MODEL * 4)
    return pl.pallas_call(
        _mix_kernel,
        grid=(t_tokens // tm,),
        in_specs=[tile(BRANCH_W)] * 4 + [tile(N_BRANCH * D_MODEL), tile(D_MODEL),
                  _const_spec(wb.shape), _const_spec(wout.shape), _const_spec(gx.shape),
                  _const_spec(gq.shape), _const_spec(wxq.shape), mem_spec, mem_spec,
                  _const_spec(wxo.shape)],
        out_specs=tile(D_MODEL),
        out_shape=jax.ShapeDtypeStruct((t_tokens, D_MODEL), F32),
        compiler_params=pltpu.CompilerParams(dimension_semantics=("arbitrary",),
                                             vmem_limit_bytes=limit),
        name="mix",
    )(o_a, o_b, o_c, o_d, gates, x2d, wb, wout, gx, gq, wxq, kx, vx, wxo)


def _mlp_kernel(x_ref, g_ref, wup_ref, wdown_ref, o_ref, *, chunk):
    x = x_ref[...]
    hn = _rms_rows(x, g_ref[...]).astype(BF16)
    acc = x
    for c in range(D_FF // chunk):
        u = jnp.maximum(_dot(hn, wup_ref[:, c * chunk:(c + 1) * chunk]), 0.0)
        acc = acc + _dot((u * u).astype(BF16), wdown_ref[c * chunk:(c + 1) * chunk, :])
    o_ref[...] = acc


def _mlp(x2d, g, wup, wdown, tm, chunk=512):
    t_tokens = x2d.shape[0]
    tile = pl.BlockSpec((tm, D_MODEL), lambda i: (i, 0))
    limit = _vmem_limit(_nbytes(wup.shape, BF16) + _nbytes(wdown.shape, BF16),
                        4 * tm * D_MODEL * 4, 8 * tm * max(chunk, D_MODEL) * 4)
    return pl.pallas_call(
        functools.partial(_mlp_kernel, chunk=chunk),
        grid=(t_tokens // tm,),
        in_specs=[tile, _const_spec(g.shape), _const_spec(wup.shape), _const_spec(wdown.shape)],
        out_specs=tile,
        out_shape=jax.ShapeDtypeStruct((t_tokens, D_MODEL), F32),
        compiler_params=pltpu.CompilerParams(dimension_semantics=("arbitrary",),
                                             vmem_limit_bytes=limit),
        name="mlp",
    )(x2d, g, wup, wdown)


def _block_diag_ones(sizes):
    m = np.zeros((SEG, SEG), np.float32)
    start = 0
    for size, on in sizes:
        if on:
            m[start:start + size, start:start + size] = 1.0
        start += size
    assert start == SEG
    return jnp.asarray(m, BF16)


def _rope_tables(seq):
    pos = jnp.arange(seq)
    inv = ROPE_THETA ** (-jnp.arange(0, B_ROPE, 2, dtype=F32) / B_ROPE)

    def cs(p):
        ang = p.astype(F32)[:, None] * inv[None, :]
        return (jnp.concatenate([jnp.cos(ang)] * 2, -1), jnp.concatenate([jnp.sin(ang)] * 2, -1))

    cos_p, sin_p = cs(pos)
    one = jnp.ones((seq, 1), F32)
    zero = jnp.zeros((seq, 1), F32)
    cosb = jnp.concatenate([one * jnp.ones((1, HALF)), cos_p, one * jnp.ones((1, 32))], 1)
    sinb = jnp.concatenate([zero * jnp.ones((1, HALF)), sin_p, zero * jnp.ones((1, 32))], 1)
    cos_r, sin_r = cs(pos // GRID_W)
    cos_c, sin_c = cs(pos % GRID_W)
    cosd = jnp.concatenate([cos_r, cos_c] * 2, 1)
    sind = jnp.concatenate([sin_r, sin_c] * 2, 1)
    return cosb, sinb, cosd, sind


def _layout_w_in(w):
    off = np.concatenate([[0], np.cumsum(IN_SIZES)])

    def seg(i):
        return w[:, off[i]:off[i + 1]]

    def zeros(n):
        return jnp.zeros((w.shape[0], n), w.dtype)

    cols = [seg(0), seg(1), seg(2), seg(3), seg(4), zeros(HALF), seg(5), zeros(32),
            seg(6), seg(7), seg(8), seg(10), seg(11), seg(9), seg(12)]
    out = jnp.concatenate(cols, axis=1).astype(BF16)
    assert out.shape[1] == IN_WIDTH_PADDED
    return out


def _pad_row(v):
    return jnp.pad(v, (0, D_MODEL - v.shape[0]))


def _vector_table(g_mix, g_qk_a, g_cq, g_ckv, g_qk_b, g_qk_c, g_qk_d):
    z32 = jnp.zeros((32,), F32)
    z64 = jnp.zeros((HALF,), F32)
    rows = [None] * N_VROWS
    rows[V_GMIX] = g_mix
    rows[V_AQ] = jnp.tile(g_qk_a[0], 8) * (A_DK ** -0.5 * LOG2E)
    rows[V_AK] = jnp.tile(g_qk_a[1], 8)
    rows[V_GCQ] = g_cq
    rows[V_GCKV] = g_ckv
    rows[V_QB] = jnp.tile(jnp.concatenate([g_qk_b[0], z32]), B_HEADS) * (
        (B_NOPE + B_ROPE) ** -0.5 * LOG2E)
    rows[V_QB_CNT] = jnp.tile(jnp.concatenate(
        [jnp.full((B_NOPE,), 1.0 / B_NOPE), jnp.full((B_ROPE,), 1.0 / B_ROPE), jnp.ones((32,))]),
        B_HEADS)
    rows[V_KB] = jnp.tile(jnp.concatenate([g_qk_b[1, :B_NOPE], z64]), B_HEADS)
    rows[V_KR] = jnp.concatenate([z64, g_qk_b[1, B_NOPE:], z32])
    rows[V_CQ] = jnp.tile(g_qk_c[0], 8) * (C_DH ** -0.5 * LOG2E)
    ones = jnp.ones((LANES,), F32)
    rows[V_G5] = jnp.concatenate([jnp.tile(g_qk_c[1], 2), ones, jnp.tile(g_qk_d[1], 2), ones])
    rows[V_DQ] = jnp.tile(g_qk_d[0], 8) * (D_DH ** -0.5 * LOG2E)
    return jnp.stack([_pad_row(r.astype(F32)) for r in rows])


def _forward(x, mem, rel_bias, g_mix, w_in, lam, g_qk_a, g_sub_a, g_cq, g_ckv, w_uq, w_ukv,
             g_qk_b, g_qk_c, sink_c, g_qk_d, w_branch, w_out, g_x, g_mem, w_xq, w_xkv,
             g_qk_x, w_xo, g_mlp, w_up, w_down, *, t_attn, tm, rows_c):
    batch, seq, _ = x.shape
    depth = w_in.shape[0]
    assert seq % t_attn == 0 and seq % tm == 0 and seq % rows_c == 0 and seq % GRID_W == 0

    bd64 = _block_diag_ones([(HALF, True)] * 4)
    bdb = _block_diag_ones([(B_NOPE, True), (B_ROPE, True), (32, False)] * 2)
    cosb, sinb, cosd, sind = _rope_tables(seq)
    bias_a, bias_c = _bias_tiles(rel_bias, t_attn)

    def cols_a(p):
        return (p, p, p, p, 2 * p, 2 * p)

    def cols_b(p):
        return (2 * p, 2 * p + 1, 2 * p, 2 * p + 1, 2 * p, 2 * p + 1)

    def cols_d(p):
        g = p // 2
        return (p, p, 2 * g, 2 * g + 1, 2 * g, 2 * g + 1)

    def amax(v):
        return jnp.max(jnp.abs(v))

    x2d = x.reshape(batch * seq, D_MODEL)
    for layer in range(depth):
        lam_init = 0.8 - 0.6 * math.exp(-0.3 * layer)
        vec = _vector_table(g_mix[layer], g_qk_a[layer], g_cq[layer], g_ckv[layer],
                            g_qk_b[layer], g_qk_c[layer], g_qk_d[layer])
        wuq = jnp.pad(w_uq[layer].reshape(B_Q_LORA, B_HEADS, B_NOPE + B_ROPE),
                      ((0, 0), (0, 0), (0, 32))).reshape(B_Q_LORA, B_HEADS * LANES).astype(BF16)
        wukv = w_ukv[layer].reshape(B_KV_LORA, B_HEADS, B_NOPE + B_DV)
        wukvk = jnp.pad(wukv[:, :, :B_NOPE], ((0, 0), (0, 0), (0, HALF))).reshape(
            B_KV_LORA, B_HEADS * LANES).astype(BF16)
        wukvv = wukv[:, :, B_NOPE:].reshape(B_KV_LORA, B_HEADS * B_DV).astype(BF16)
        (qa, ka1, ka2, va, qb, kb, vb, qc, kc, vc, qd, kd, vd, gates) = _front(
            x2d, vec, _layout_w_in(w_in[layer]), wuq, wukvk, wukvv, bd64, bdb,
            cosb, sinb, cosd, sind, seq, tm)

        def b3(a):
            return a.reshape(batch, seq, a.shape[1])

        ga, gb, gd = g_qk_a[layer], g_qk_b[layer], g_qk_d[layer]
        bound_a = LOG2E * (A_DK ** 0.5 * amax(ga[0]) * amax(ga[1]) + amax(rel_bias[:, :A_HEADS]))
        norm_b = [jnp.sqrt(B_NOPE * amax(gb[s, :B_NOPE]) ** 2 + B_ROPE * amax(gb[s, B_NOPE:]) ** 2)
                  for s in range(2)]
        bound_b = LOG2E * (B_NOPE + B_ROPE) ** -0.5 * norm_b[0] * norm_b[1]
        bound_d = LOG2E * D_DH ** 0.5 * amax(gd[0]) * amax(gd[1])

        o_a = _dense_attention(bound_a, b3(qa), b3(ka1), b3(ka2), b3(va), cols_a, A_HEADS,
                               t_attn, "diff", bias=bias_a, lam=lam[layer],
                               gsub=g_sub_a[layer][None, :], lam_init=lam_init)
        o_b = _dense_attention(bound_b, b3(qb), b3(kb), b3(kb), b3(vb), cols_b, B_HEADS // 2,
                               t_attn, "merge")
        o_c = _window_attention(b3(qc), b3(kc), b3(vc), bias_c, sink_c[layer], rows_c)
        o_d = _dense_attention(bound_d, b3(qd), b3(kd), b3(kd), b3(vd), cols_d, D_HEADS // 2,
                               t_attn, "merge")

        kx, vx = _memkv(mem, g_mem[layer][None, :], g_qk_x[layer][1][None, :],
                        w_xkv[layer].astype(BF16))
        t2 = batch * seq
        x2d = _mix(o_a.reshape(t2, -1), o_b.reshape(t2, -1), o_c.reshape(t2, -1),
                   o_d.reshape(t2, -1), gates, x2d, w_branch[layer].astype(BF16),
                   w_out[layer].astype(BF16), g_x[layer][None, :],
                   (g_qk_x[layer][0] * (X_DH ** -0.5 * LOG2E))[None, :],
                   w_xq[layer].astype(BF16), kx, vx, w_xo[layer].astype(BF16), seq, tm)
        x2d = _mlp(x2d, g_mlp[layer][None, :], w_up[layer].astype(BF16),
                   w_down[layer].astype(BF16), tm)
    return x2d.reshape(batch, seq, D_MODEL)


def kernel(x, mem, rel_bias, g_mix, w_in, lam, g_qk_a, g_sub_a, g_cq, g_ckv, w_uq, w_ukv, g_qk_b, g_qk_c, sink_c, g_qk_d, w_branch, w_out, g_x, g_mem, w_xq, w_xkv, g_qk_x, w_xo, g_mlp, w_up, w_down):
    return _forward(x, mem, rel_bias, g_mix, w_in, lam, g_qk_a, g_sub_a, g_cq, g_ckv, w_uq, w_ukv,
                    g_qk_b, g_qk_c, sink_c, g_qk_d, w_branch, w_out, g_x, g_mem, w_xq, w_xkv,
                    g_qk_x, w_xo, g_mlp, w_up, w_down, t_attn=512, tm=256, rows_c=512)
```

```python
import functools
import math

import numpy as np
import jax
import jax.numpy as jnp
from jax import lax
from jax.experimental import pallas as pl
from jax.experimental.pallas import tpu as pltpu

F32 = jnp.float32
BF16 = jnp.bfloat16

D_MODEL = 1024
GRID_W = 64
N_MEM = 256
EPS = 1e-6
ROPE_THETA = 10000.0
NEG_INF = -1e30
LOG2E = 1.4426950408889634
SAFE_SCORE = 60.0

A_HEADS, A_DK, A_DV = 4, 64, 128
B_HEADS, B_Q_LORA, B_KV_LORA, B_NOPE, B_ROPE, B_DV = 8, 256, 128, 64, 32, 64
C_HEADS, C_KV_HEADS, C_DH, WINDOW = 8, 2, 64, 128
D_HEADS, D_KV_HEADS, D_DH = 8, 2, 64
X_HEADS, X_DH = 4, 128
D_FF = 4 * D_MODEL
REL_BUCKETS, REL_MAX_DIST = 32, 128
N_BRANCH, BRANCH_W = 4, 512

IN_SIZES = (512, 512, 512, B_Q_LORA, B_KV_LORA, B_ROPE, 512, 128, 128, 512, 128, 128,
            N_BRANCH * D_MODEL)
GROUP_W = 512
N_QKV_GROUPS = 7
N_GATE_GROUPS = N_BRANCH * D_MODEL // GROUP_W
IN_WIDTH_PADDED = (N_QKV_GROUPS + N_GATE_GROUPS) * GROUP_W

LANES = 128
HALF = 64
ONES_ROWS = 16
SEG = 256
VMEM_CAP = 60 * 1024 * 1024

(V_GMIX, V_AQ, V_AK, V_GCQ, V_GCKV, V_QB, V_QB_CNT, V_KB, V_KR, V_CQ, V_G5, V_DQ,
 N_VROWS) = range(13)


def _vmem_limit(*nbytes):
    return int(min(VMEM_CAP, sum(nbytes) + (6 << 20)))


def _const_spec(shape):
    zeros = (0,) * len(shape)
    return pl.BlockSpec(shape, lambda *_: zeros, pipeline_mode=pl.Buffered(1))


def _nbytes(shape, dtype):
    return int(np.prod(shape)) * jnp.dtype(dtype).itemsize


def _dot(a, b):
    return jnp.dot(a, b, preferred_element_type=F32)


def _dot_nt(a, b):
    return lax.dot_general(a, b, (((1,), (1,)), ((), ())), preferred_element_type=F32)


def _rms_rows(x, g):
    return x * lax.rsqrt(jnp.mean(x * x, axis=-1, keepdims=True) + EPS) * g


def _seg_rms(y, bd, inv_cnt, gain):
    outs = []
    for c in range(y.shape[1] // SEG):
        sl = slice(c * SEG, (c + 1) * SEG)
        yc = y[:, sl]
        ssq = _dot((yc * yc).astype(BF16), bd)
        outs.append(yc * lax.rsqrt(ssq * inv_cnt[:, sl] + EPS) * gain[:, sl])
    return outs[0] if len(outs) == 1 else jnp.concatenate(outs, axis=1)


def _rope32(x, cos, sin):
    w = x.shape[1]
    lane = lax.broadcasted_iota(jnp.int32, x.shape, 1)
    first = (lane & 16) == 0
    rot = jnp.where(first, -pltpu.roll(x, w - 16, 1), pltpu.roll(x, 16, 1))
    return x * cos + rot * sin


def _tile_lanes(v, n):
    return v if n == 1 else jnp.concatenate([v] * n, axis=1)


def _pair_split(x, fill):
    lane = lax.broadcasted_iota(jnp.int32, x.shape, 1)
    low = lane < HALF
    sw = pltpu.roll(x, HALF, 1)
    return (jnp.where(low, x, fill), jnp.where(low, fill, sw), jnp.where(low, sw, fill),
            jnp.where(low, fill, x))


def _with_ones_rows(vt, head_rows):
    ones = jnp.ones((ONES_ROWS, vt.shape[1]), vt.dtype)
    parts = []
    for hh in range(vt.shape[0] // head_rows):
        parts += [vt[hh * head_rows:(hh + 1) * head_rows], ones]
    return jnp.concatenate(parts, axis=0)


def _front_kernel(x_ref, vec_ref, win_ref, wuq_ref, wukvk_ref, wukvv_ref, bd64_ref, bdb_ref,
                  cosb_ref, sinb_ref, cosd_ref, sind_ref,
                  qa_ref, ka1_ref, ka2_ref, va_ref, qb_ref, kb_ref, vb_ref,
                  qc_ref, kc_ref, vc_ref, qd_ref, kd_ref, vd_ref, gate_ref,
                  vat_ref, vbt_ref, vdt_ref):
    def row(r, w):
        return vec_ref[r:r + 1, :w]

    ones = jnp.ones((1, D_MODEL), F32)
    inv64 = ones * (1.0 / HALF)
    bd64 = bd64_ref[...]
    h = _rms_rows(x_ref[...], row(V_GMIX, D_MODEL)).astype(BF16)

    def group(g):
        return _dot(h, win_ref[:, g * GROUP_W:(g + 1) * GROUP_W])

    lane512 = lax.broadcasted_iota(jnp.int32, (x_ref.shape[0], GROUP_W), 1)

    qa_ref[...] = _seg_rms(group(0), bd64, inv64, row(V_AQ, GROUP_W)).astype(BF16)
    ka = _seg_rms(group(1), bd64, inv64, row(V_AK, GROUP_W))
    first_map = (lane512 & HALF) == 0
    ka1_ref[...] = jnp.where(first_map, ka, 0.0).astype(BF16)
    ka2_ref[...] = jnp.where(first_map, 0.0, ka).astype(BF16)
    va = group(2)
    va_ref[...] = va.astype(BF16)
    vat_ref[...] = _with_ones_rows(va.T, A_DV).astype(BF16)

    y = group(3)
    cosb = cosb_ref[...]
    sinb = sinb_ref[...]
    cq = _rms_rows(y[:, :B_Q_LORA], row(V_GCQ, B_Q_LORA)).astype(BF16)
    qb = _seg_rms(_dot(cq, wuq_ref[...]), bdb_ref[...], row(V_QB_CNT, D_MODEL), row(V_QB, D_MODEL))
    qb_ref[...] = _rope32(qb, _tile_lanes(cosb, B_HEADS), _tile_lanes(sinb, B_HEADS)).astype(BF16)
    ckv = _rms_rows(y[:, B_Q_LORA:B_Q_LORA + B_KV_LORA], row(V_GCKV, B_KV_LORA)).astype(BF16)
    k_nope = _seg_rms(_dot(ckv, wukvk_ref[...]), bd64, inv64, row(V_KB, D_MODEL))
    kr = y[:, B_Q_LORA + B_KV_LORA:]
    kr = kr * lax.rsqrt(jnp.sum(kr * kr, axis=-1, keepdims=True) * (1.0 / B_ROPE) + EPS)
    kr = _rope32(kr * row(V_KR, LANES), cosb, sinb)
    kb_ref[...] = (k_nope + _tile_lanes(kr, B_HEADS)).astype(BF16)
    vb = _dot(ckv, wukvv_ref[...])
    vb_ref[...] = vb.astype(BF16)
    vbt_ref[...] = _with_ones_rows(vb.T, B_DV).astype(BF16)

    cosd = cosd_ref[...]
    sind = sind_ref[...]
    y = group(5)
    yn = _seg_rms(y, bd64, inv64, row(V_G5, GROUP_W))
    kc = yn[:, 0:LANES]
    vc = y[:, LANES:2 * LANES]
    kd = _rope32(yn[:, 2 * LANES:3 * LANES], cosd, sind)
    vd = y[:, 3 * LANES:]
    kc_ref[...] = jnp.concatenate(_pair_split(kc, 0.0), axis=1).astype(BF16)
    kd_ref[...] = jnp.concatenate(_pair_split(kd, 0.0), axis=1).astype(BF16)
    vc_ref[...] = jnp.concatenate(_pair_split(vc, 1.0), axis=1).astype(BF16)
    low = lax.broadcasted_iota(jnp.int32, vd.shape, 1) < HALF
    vd_sw = pltpu.roll(vd, HALF, 1)
    vd_ref[...] = jnp.concatenate([jnp.where(low, vd, vd_sw), jnp.where(low, vd_sw, vd)],
                                  axis=1).astype(BF16)
    vdt_ref[...] = _with_ones_rows(vd.T, D_DH).astype(BF16)

    qc_ref[...] = _seg_rms(group(4), bd64, inv64, row(V_CQ, GROUP_W)).astype(BF16)
    qd = _seg_rms(group(6), bd64, inv64, row(V_DQ, GROUP_W))
    qd_ref[...] = _rope32(qd, _tile_lanes(cosd, 4), _tile_lanes(sind, 4)).astype(BF16)

    for c in range(N_GATE_GROUPS):
        z = group(N_QKV_GROUPS + c)
        gate_ref[:, c * GROUP_W:(c + 1) * GROUP_W] = (1.0 / (1.0 + jnp.exp(-z))).astype(BF16)


def _front(x2d, vec, win, wuq, wukvk, wukvv, bd64, bdb, cosb, sinb, cosd, sind, seq, tm):
    t_tokens = x2d.shape[0]
    n_pos = seq // tm
    widths = (512, 512, 512, 512, 1024, 1024, 512, 512, 512, 512, 512, 512, 256, 4096)
    t_rows = (A_HEADS * (A_DV + ONES_ROWS), B_HEADS * (B_DV + ONES_ROWS),
              D_KV_HEADS * (D_DH + ONES_ROWS))

    def tile(w):
        return pl.BlockSpec((tm, w), lambda i: (i, 0))

    def pos(w):
        return pl.BlockSpec((tm, w), lambda i: (i % n_pos, 0))

    def tile_t(r):
        return pl.BlockSpec((None, r, tm), lambda i: (i // n_pos, 0, i % n_pos))

    consts = (vec, win, wuq, wukvk, wukvv, bd64, bdb)
    in_specs = [tile(D_MODEL)] + [_const_spec(c.shape) for c in consts] + [pos(LANES)] * 4
    limit = _vmem_limit(sum(_nbytes(c.shape, c.dtype) for c in consts),
                        2 * tm * (D_MODEL * 4 + 4 * LANES * 4 + (sum(widths) + sum(t_rows)) * 2),
                        12 * tm * D_MODEL * 4)
    return pl.pallas_call(
        _front_kernel,
        grid=(t_tokens // tm,),
        in_specs=in_specs,
        out_specs=[tile(w) for w in widths] + [tile_t(r) for r in t_rows],
        out_shape=([jax.ShapeDtypeStruct((t_tokens, w), BF16) for w in widths]
                   + [jax.ShapeDtypeStruct((t_tokens // seq, r, seq), BF16) for r in t_rows]),
        compiler_params=pltpu.CompilerParams(dimension_semantics=("arbitrary",),
                                             vmem_limit_bytes=limit),
        name="front",
    )(x2d, *consts, cosb, sinb, cosd, sind)


def _t5_bucket(rel):
    nb = REL_BUCKETS // 2
    max_exact = nb // 2
    n = jnp.abs(rel)
    nf = jnp.maximum(n, 1).astype(F32)
    large = max_exact + (jnp.log(nf / max_exact) / math.log(REL_MAX_DIST / max_exact)
                         * (nb - max_exact)).astype(jnp.int32)
    large = jnp.minimum(large, nb - 1)
    return jnp.where(rel > 0, nb, 0) + jnp.where(n < max_exact, n, large)


def _table_lookup(bucket, tbl_ref, col):
    acc = jnp.zeros(bucket.shape, F32)
    for b in range(REL_BUCKETS):
        acc = jnp.where(bucket == b, tbl_ref[b, col], acc)
    return acc


def _bias_dense_kernel(tbl_ref, o_ref, *, t, key_axis):
    h = pl.program_id(0)
    d = pl.program_id(1) - 2
    rel = (d * t + lax.broadcasted_iota(jnp.int32, (t, t), key_axis)
           - lax.broadcasted_iota(jnp.int32, (t, t), 1 - key_axis))
    o_ref[...] = _table_lookup(_t5_bucket(rel), tbl_ref, h) * LOG2E


def _bias_window_kernel(tbl_ref, o_ref):
    h = pl.program_id(0)
    shape = (LANES, 3 * LANES)
    rel = (lax.broadcasted_iota(jnp.int32, shape, 1) - LANES
           - lax.broadcasted_iota(jnp.int32, shape, 0))
    bias = _table_lookup(_t5_bucket(rel), tbl_ref, A_HEADS + h) * LOG2E
    o_ref[...] = jnp.where(jnp.abs(rel) <= WINDOW, bias, NEG_INF)


def _bias_dense(rel_bias, t, key_axis):
    assert t >= LANES
    return pl.pallas_call(
        functools.partial(_bias_dense_kernel, t=t, key_axis=key_axis),
        grid=(A_HEADS, 5),
        in_specs=[pl.BlockSpec(memory_space=pltpu.SMEM)],
        out_specs=pl.BlockSpec((None, None, t, t), lambda h, d: (h, d, 0, 0)),
        out_shape=jax.ShapeDtypeStruct((A_HEADS, 5, t, t), F32),
        name="bias_dense",
    )(rel_bias)


def _bias_window(rel_bias):
    return pl.pallas_call(
        _bias_window_kernel,
        grid=(C_HEADS,),
        in_specs=[pl.BlockSpec(memory_space=pltpu.SMEM)],
        out_specs=pl.BlockSpec((None, LANES, 3 * LANES), lambda h: (h, 0, 0)),
        out_shape=jax.ShapeDtypeStruct((C_HEADS, LANES, 3 * LANES), F32),
        name="bias_window",
    )(rel_bias)


def _pair_attn_kernel(*refs, t, n_chunks, mode, lam_init):
    if mode == "diff":
        (qa_ref, qb_ref, ka_ref, kb_ref, va_ref, vb_ref, bias_ref, lam_ref, gsub_ref, o_ref,
         ma, la, acca, mb, lb, accb) = refs
    else:
        (qa_ref, qb_ref, ka_ref, kb_ref, va_ref, vb_ref, o_ref,
         ma, la, acca, mb, lb, accb) = refs
        bias_ref = None
    i = pl.program_id(2)
    streams = ((qa_ref[...], ka_ref, va_ref, ma, la, acca),
               (qb_ref[...], kb_ref, vb_ref, mb, lb, accb))
    for _, _, _, m, l, acc in streams:
        m[...] = jnp.full(m.shape, NEG_INF, F32)
        l[...] = jnp.zeros(l.shape, F32)
        acc[...] = jnp.zeros(acc.shape, F32)

    def body(c, carry):
        off = pl.multiple_of(c * t, t)
        bias = None if bias_ref is None else bias_ref[jnp.clip(c - i, -2, 2) + 2]
        for q, k_ref, v_ref, m, l, acc in streams:
            s = _dot_nt(q, k_ref[pl.ds(off, t), :])
            if bias is not None:
                s = s + bias
            m_prev = m[...]
            m_new = jnp.maximum(m_prev, jnp.max(s, axis=1, keepdims=True))
            alpha = jnp.exp2(m_prev - m_new)
            p = jnp.exp2(s - m_new)
            l[...] = alpha * l[...] + jnp.sum(p, axis=1, keepdims=True)
            acc[...] = alpha * acc[...] + _dot(p.astype(BF16), v_ref[pl.ds(off, t), :])
            m[...] = m_new
        return carry

    lax.fori_loop(0, n_chunks, body, 0)
    _finish_pair(acca[...] / la[...], accb[...] / lb[...], mode, refs, lam_init)


def _write_diff(oa, ob, lam_ref, gsub_ref, o_ref, lam_init):
    lam = lam_ref[...]
    lam_full = (jnp.exp(jnp.sum(lam[0:1] * lam[1:2], keepdims=True))
                - jnp.exp(jnp.sum(lam[2:3] * lam[3:4], keepdims=True)) + lam_init)
    o = oa - lam_full * ob
    o_ref[...] = (_rms_rows(o, gsub_ref[...]) * (1.0 - lam_init)).astype(BF16)


def _finish_pair(oa, ob, mode, refs, lam_init):
    if mode == "diff":
        _write_diff(oa, ob, *refs[7:10], lam_init)
    else:
        o_ref = refs[6]
        lane = lax.broadcasted_iota(jnp.int32, oa.shape, 1)
        o_ref[...] = jnp.where(lane < HALF, oa, ob).astype(BF16)


def _pair_attn_fast_kernel(*refs, t, n_chunks, mode, lam_init):
    qa_ref, qb_ref, ka_ref, kb_ref, vta_ref, vtb_ref = refs[:6]
    bias_ref = refs[6] if mode == "diff" else None
    acca, accb = refs[-2:]
    i = pl.program_id(2)
    streams = ((qa_ref[...], ka_ref, vta_ref, acca), (qb_ref[...], kb_ref, vtb_ref, accb))
    for _, _, _, acc in streams:
        acc[...] = jnp.zeros(acc.shape, F32)

    sums = [None, None]
    for c in range(n_chunks):
        bias = None if bias_ref is None else bias_ref[jnp.clip(c - i, -2, 2) + 2]
        for n_s, (q, k_ref, vt_ref, acc) in enumerate(streams):
            s = _dot_nt(k_ref[c * t:(c + 1) * t, :], q)
            if bias is not None:
                s = s + bias
            part = _dot(vt_ref[:, c * t:(c + 1) * t], jnp.exp2(s).astype(BF16))
            sums[n_s] = part if sums[n_s] is None else sums[n_s] + part
    n = acca.shape[0] - ONES_ROWS
    a, b = sums
    oa = a[:n] / a[n:n + 1]
    ob = b[:n] / b[n:n + 1]
    if mode == "diff":
        _write_diff(oa.T, ob.T, *refs[7:10], lam_init)
    else:
        refs[6][...] = jnp.concatenate([oa, ob], axis=0).T.astype(BF16)


def _pair_attention(q, k_a, k_b, v, cols, n_pairs, t, mode, fast, bias=None, lam=None,
                    gsub=None, lam_init=0.0):
    batch, seq, _ = q.shape

    def q_spec(which):
        return pl.BlockSpec((None, t, LANES), lambda b, p, i: (b, i, cols(p)[which]))

    def kv_spec(which):
        return pl.BlockSpec((None, seq, LANES), lambda b, p, i: (b, 0, cols(p)[which]))

    if fast:
        v_rows = (A_DV if mode == "diff" else HALF) + ONES_ROWS

        def v_spec(which):
            return pl.BlockSpec((None, v_rows, seq), lambda b, p, i: (b, cols(p)[which], 0))

        body = _pair_attn_fast_kernel
        scratch = [pltpu.VMEM((v_rows, t), F32)] * 2
        v_bytes = v_rows * seq * 2
    else:
        v_spec = kv_spec
        body = _pair_attn_kernel
        scratch = [pltpu.VMEM((t, 1), F32), pltpu.VMEM((t, 1), F32),
                   pltpu.VMEM((t, LANES), F32)] * 2
        v_bytes = seq * LANES * 2
    in_specs = [q_spec(0), q_spec(1), kv_spec(2), kv_spec(3), v_spec(4), v_spec(5)]
    args = [q, q, k_a, k_b, v, v]
    resident = 2 * 2 * (seq * LANES * 2 + v_bytes)
    if mode == "diff":
        in_specs += [pl.BlockSpec((None, 5, t, t), lambda b, p, i: (p, 0, 0, 0)),
                     _const_spec(lam.shape), _const_spec(gsub.shape)]
        args += [bias, lam, gsub]
        resident += 2 * 5 * t * t * 4
    limit = _vmem_limit(resident, 6 * t * LANES * 2, 8 * t * LANES * 4, 8 * t * t * 4)
    return pl.pallas_call(
        functools.partial(body, t=t, n_chunks=seq // t, mode=mode, lam_init=lam_init),
        grid=(batch, n_pairs, seq // t),
        in_specs=in_specs,
        out_specs=pl.BlockSpec((None, t, LANES), lambda b, p, i: (b, i, p)),
        out_shape=jax.ShapeDtypeStruct((batch, seq, n_pairs * LANES), BF16),
        scratch_shapes=scratch,
        compiler_params=pltpu.CompilerParams(
            dimension_semantics=("arbitrary", "arbitrary", "arbitrary"),
            vmem_limit_bytes=limit),
        name=("attn_fast_" if fast else "attn_") + mode,
    )(*args)


def _bounded(score_bound, fast, robust):
    return lax.cond(score_bound <= SAFE_SCORE, fast, robust)


def _window_kernel(sink_ref, q_ref, kp_ref, km_ref, kn_ref, vp_ref, vm_ref, vn_ref, bias_ref,
                   o_ref, *, rows, fast):
    i = pl.program_id(1)
    n_steps = pl.num_programs(1)
    nblk = rows // LANES
    lane = lax.broadcasted_iota(jnp.int32, (LANES, LANES), 1)
    zero_row = jnp.zeros((1, LANES), F32)

    def kv_block(j):
        if j < 0:
            return kp_ref[...], vp_ref[...]
        if j >= nblk:
            return kn_ref[...], vn_ref[...]
        return km_ref[j * LANES:(j + 1) * LANES, :], vm_ref[j * LANES:(j + 1) * LANES, :]

    for r in range(nblk):
        blocks = [kv_block(r + d) for d in (-1, 0, 1)]
        kcat = jnp.concatenate([blk[0] for blk in blocks], axis=0)
        vcat = jnp.concatenate([blk[1] for blk in blocks], axis=0)
        edge = None
        if r == 0 or r == nblk - 1:
            before = zero_row + jnp.where(i > 0, 0.0, NEG_INF) if r == 0 else zero_row
            after = zero_row + jnp.where(i < n_steps - 1, 0.0, NEG_INF) if r == nblk - 1 else zero_row
            edge = jnp.concatenate([before, zero_row, after], axis=1)
        for p in range(C_HEADS // 2):
            g = p // (C_HEADS // C_KV_HEADS // 2)
            q = q_ref[r * LANES:(r + 1) * LANES, p * LANES:(p + 1) * LANES]
            outs = []
            for e in range(2):
                head = 2 * p + e
                col = slice((2 * g + e) * LANES, (2 * g + e + 1) * LANES)
                s = _dot_nt(q, kcat[:, col]) + bias_ref[head]
                if edge is not None:
                    s = s + edge
                sink = sink_ref[head] * LOG2E
                if fast:
                    o = _dot(jnp.exp2(s).astype(BF16), vcat[:, col])
                    outs.append(o / (pltpu.roll(o, HALF, 1) + jnp.exp2(zero_row + sink)))
                else:
                    m = jnp.maximum(jnp.max(s, axis=1, keepdims=True), sink)
                    pr = jnp.exp2(s - m)
                    denom = jnp.sum(pr, axis=1, keepdims=True) + jnp.exp2(sink - m)
                    outs.append(_dot(pr.astype(BF16), vcat[:, col]) / denom)
            o_ref[r * LANES:(r + 1) * LANES, p * LANES:(p + 1) * LANES] = (
                jnp.where(lane < HALF, outs[0], outs[1]).astype(BF16))


def _window_attention(q, k, v, bias, sink, rows, fast):
    batch, seq, _ = q.shape
    nblk = rows // LANES
    last = seq // LANES - 1
    kw, vw = k.shape[2], v.shape[2]

    def main(w):
        return pl.BlockSpec((None, rows, w), lambda b, i: (b, i, 0))

    def prev(w):
        return pl.BlockSpec((None, LANES, w), lambda b, i: (b, jnp.maximum(i * nblk - 1, 0), 0))

    def nxt(w):
        return pl.BlockSpec((None, LANES, w), lambda b, i: (b, jnp.minimum((i + 1) * nblk, last), 0))

    limit = _vmem_limit(_nbytes(bias.shape, F32) * 2,
                        2 * (rows + 2 * LANES) * (kw + vw) * 2 + 4 * rows * 512 * 2,
                        16 * LANES * 3 * LANES * 4)
    return pl.pallas_call(
        functools.partial(_window_kernel, rows=rows, fast=fast),
        grid=(batch, seq // rows),
        in_specs=[pl.BlockSpec(memory_space=pltpu.SMEM), main(q.shape[2]),
                  prev(kw), main(kw), nxt(kw), prev(vw), main(vw), nxt(vw),
                  _const_spec(bias.shape)],
        out_specs=main(q.shape[2]),
        out_shape=jax.ShapeDtypeStruct(q.shape, BF16),
        compiler_params=pltpu.CompilerParams(dimension_semantics=("arbitrary", "arbitrary"),
                                             vmem_limit_bytes=limit),
        name="attn_window_fast" if fast else "attn_window",
    )(sink, q, k, k, k, v, v, v, bias)


def _memkv_kernel(mem_ref, gmem_ref, gk_ref, w_ref, k_ref, v_ref):
    hn = _rms_rows(mem_ref[...], gmem_ref[...]).astype(BF16)
    kv = _dot(hn, w_ref[...])
    width = X_HEADS * X_DH
    gk = gk_ref[...]
    for h in range(X_HEADS):
        sl = slice(h * X_DH, (h + 1) * X_DH)
        k_ref[:, sl] = _rms_rows(kv[:, sl], gk).astype(BF16)
    v_ref[...] = kv[:, width:].astype(BF16)


def _memkv(mem, gmem, gk, w):
    batch, n_mem, _ = mem.shape
    width = X_HEADS * X_DH
    out = pl.BlockSpec((None, n_mem, width), lambda b: (b, 0, 0))
    return pl.pallas_call(
        _memkv_kernel,
        grid=(batch,),
        in_specs=[pl.BlockSpec((None, n_mem, D_MODEL), lambda b: (b, 0, 0)),
                  _const_spec(gmem.shape), _const_spec(gk.shape), _const_spec(w.shape)],
        out_specs=[out, out],
        out_shape=[jax.ShapeDtypeStruct((batch, n_mem, width), BF16)] * 2,
        compiler_params=pltpu.CompilerParams(
            dimension_semantics=("arbitrary",),
            vmem_limit_bytes=_vmem_limit(_nbytes(w.shape, BF16), 4 * n_mem * D_MODEL * 4,
                                         4 * n_mem * 2 * width * 4)),
        name="memkv",
    )(mem, gmem, gk, w)


def _mix_kernel(oa_ref, ob_ref, oc_ref, od_ref, gate_ref, x_ref, wb_ref, wout_ref, gx_ref,
                gq_ref, wxq_ref, kx_ref, vx_ref, wxo_ref, o_ref):
    mix = None
    for m, o_m in enumerate((oa_ref, ob_ref, oc_ref, od_ref)):
        br = _dot(o_m[...], wb_ref[m])
        term = gate_ref[:, m * D_MODEL:(m + 1) * D_MODEL].astype(F32) * br
        mix = term if mix is None else mix + term
    x1 = x_ref[...] + _dot(mix.astype(BF16), wout_ref[...])
    hn = _rms_rows(x1, gx_ref[...]).astype(BF16)
    qx = _dot(hn, wxq_ref[...])
    gq = gq_ref[...]
    heads = []
    for h in range(X_HEADS):
        sl = slice(h * X_DH, (h + 1) * X_DH)
        qh = _rms_rows(qx[:, sl], gq).astype(BF16)
        s = _dot_nt(qh, kx_ref[:, sl])
        p = jnp.exp2(s - jnp.max(s, axis=1, keepdims=True))
        denom = jnp.sum(p, axis=1, keepdims=True)
        heads.append((_dot(p.astype(BF16), vx_ref[:, sl]) / denom).astype(BF16))
    ox = jnp.concatenate(heads, axis=1)
    o_ref[...] = x1 + _dot(ox, wxo_ref[...])


def _mix(o_a, o_b, o_c, o_d, gates, x2d, wb, wout, gx, gq, wxq, kx, vx, wxo, seq, tm):
    t_tokens = x2d.shape[0]
    per_batch = seq // tm
    n_mem, width = kx.shape[1], kx.shape[2]

    def tile(w):
        return pl.BlockSpec((tm, w), lambda i: (i, 0))

    mem_spec = pl.BlockSpec((None, n_mem, width), lambda i: (i // per_batch, 0, 0))
    consts = (wb, wout, gx, gq, wxq, wxo)
    limit = _vmem_limit(sum(_nbytes(c.shape, c.dtype) for c in consts),
                        2 * tm * (4 * BRANCH_W * 2 + 4 * D_MODEL * 2 + 2 * D_MODEL * 4),
                        4 * n_mem * width * 2, 10 * tm * D_MODEL * 4)
    return pl.pallas_call(
        _mix_kernel,
        grid=(t_tokens // tm,),
        in_specs=[tile(BRANCH_W)] * 4 + [tile(N_BRANCH * D_MODEL), tile(D_MODEL),
                  _const_spec(wb.shape), _const_spec(wout.shape), _const_spec(gx.shape),
                  _const_spec(gq.shape), _const_spec(wxq.shape), mem_spec, mem_spec,
                  _const_spec(wxo.shape)],
        out_specs=tile(D_MODEL),
        out_shape=jax.ShapeDtypeStruct((t_tokens, D_MODEL), F32),
        compiler_params=pltpu.CompilerParams(dimension_semantics=("arbitrary",),
                                             vmem_limit_bytes=limit),
        name="mix",
    )(o_a, o_b, o_c, o_d, gates, x2d, wb, wout, gx, gq, wxq, kx, vx, wxo)


def _mlp_kernel(x_ref, g_ref, wup_ref, wdown_ref, o_ref, *, chunk):
    x = x_ref[...]
    hn = _rms_rows(x, g_ref[...]).astype(BF16)
    acc = x
    for c in range(D_FF // chunk):
        u = jnp.maximum(_dot(hn, wup_ref[:, c * chunk:(c + 1) * chunk]), 0.0)
        acc = acc + _dot((u * u).astype(BF16), wdown_ref[c * chunk:(c + 1) * chunk, :])
    o_ref[...] = acc


def _mlp(x2d, g, wup, wdown, tm, chunk=512):
    t_tokens = x2d.shape[0]
    tile = pl.BlockSpec((tm, D_MODEL), lambda i: (i, 0))
    limit = _vmem_limit(_nbytes(wup.shape, BF16) + _nbytes(wdown.shape, BF16),
                        4 * tm * D_MODEL * 4, 8 * tm * max(chunk, D_MODEL) * 4)
    return pl.pallas_call(
        functools.partial(_mlp_kernel, chunk=chunk),
        grid=(t_tokens // tm,),
        in_specs=[tile, _const_spec(g.shape), _const_spec(wup.shape), _const_spec(wdown.shape)],
        out_specs=tile,
        out_shape=jax.ShapeDtypeStruct((t_tokens, D_MODEL), F32),
        compiler_params=pltpu.CompilerParams(dimension_semantics=("arbitrary",),
                                             vmem_limit_bytes=limit),
        name="mlp",
    )(x2d, g, wup, wdown)


def _block_diag_ones(sizes):
    m = np.zeros((SEG, SEG), np.float32)
    start = 0
    for size, on in sizes:
        if on:
            m[start:start + size, start:start + size] = 1.0
        start += size
    assert start == SEG
    return jnp.asarray(m, BF16)


def _rope_tables(seq):
    pos = jnp.arange(seq)
    inv = ROPE_THETA ** (-jnp.arange(0, B_ROPE, 2, dtype=F32) / B_ROPE)

    def cs(p):
        ang = p.astype(F32)[:, None] * inv[None, :]
        return (jnp.concatenate([jnp.cos(ang)] * 2, -1), jnp.concatenate([jnp.sin(ang)] * 2, -1))

    cos_p, sin_p = cs(pos)
    one = jnp.ones((seq, 1), F32)
    zero = jnp.zeros((seq, 1), F32)
    cosb = jnp.concatenate([one * jnp.ones((1, HALF)), cos_p, one * jnp.ones((1, 32))], 1)
    sinb = jnp.concatenate([zero * jnp.ones((1, HALF)), sin_p, zero * jnp.ones((1, 32))], 1)
    cos_r, sin_r = cs(pos // GRID_W)
    cos_c, sin_c = cs(pos % GRID_W)
    cosd = jnp.concatenate([cos_r, cos_c] * 2, 1)
    sind = jnp.concatenate([sin_r, sin_c] * 2, 1)
    return cosb, sinb, cosd, sind


def _layout_w_in(w):
    off = np.concatenate([[0], np.cumsum(IN_SIZES)])

    def seg(i):
        return w[:, off[i]:off[i + 1]]

    def zeros(n):
        return jnp.zeros((w.shape[0], n), w.dtype)

    cols = [seg(0), seg(1), seg(2), seg(3), seg(4), zeros(HALF), seg(5), zeros(32),
            seg(6), seg(7), seg(8), seg(10), seg(11), seg(9), seg(12)]
    out = jnp.concatenate(cols, axis=1).astype(BF16)
    assert out.shape[1] == IN_WIDTH_PADDED
    return out


def _pad_row(v):
    return jnp.pad(v, (0, D_MODEL - v.shape[0]))


def _vector_table(g_mix, g_qk_a, g_cq, g_ckv, g_qk_b, g_qk_c, g_qk_d):
    z32 = jnp.zeros((32,), F32)
    z64 = jnp.zeros((HALF,), F32)
    rows = [None] * N_VROWS
    rows[V_GMIX] = g_mix
    rows[V_AQ] = jnp.tile(g_qk_a[0], 8) * (A_DK ** -0.5 * LOG2E)
    rows[V_AK] = jnp.tile(g_qk_a[1], 8)
    rows[V_GCQ] = g_cq
    rows[V_GCKV] = g_ckv
    rows[V_QB] = jnp.tile(jnp.concatenate([g_qk_b[0], z32]), B_HEADS) * (
        (B_NOPE + B_ROPE) ** -0.5 * LOG2E)
    rows[V_QB_CNT] = jnp.tile(jnp.concatenate(
        [jnp.full((B_NOPE,), 1.0 / B_NOPE), jnp.full((B_ROPE,), 1.0 / B_ROPE), jnp.ones((32,))]),
        B_HEADS)
    rows[V_KB] = jnp.tile(jnp.concatenate([g_qk_b[1, :B_NOPE], z64]), B_HEADS)
    rows[V_KR] = jnp.concatenate([z64, g_qk_b[1, B_NOPE:], z32])
    rows[V_CQ] = jnp.tile(g_qk_c[0], 8) * (C_DH ** -0.5 * LOG2E)
    ones = jnp.ones((LANES,), F32)
    rows[V_G5] = jnp.concatenate([jnp.tile(g_qk_c[1], 2), ones, jnp.tile(g_qk_d[1], 2), ones])
    rows[V_DQ] = jnp.tile(g_qk_d[0], 8) * (D_DH ** -0.5 * LOG2E)
    return jnp.stack([_pad_row(r.astype(F32)) for r in rows])


def _forward(x, mem, rel_bias, g_mix, w_in, lam, g_qk_a, g_sub_a, g_cq, g_ckv, w_uq, w_ukv,
             g_qk_b, g_qk_c, sink_c, g_qk_d, w_branch, w_out, g_x, g_mem, w_xq, w_xkv,
             g_qk_x, w_xo, g_mlp, w_up, w_down, *, t_attn, tm, rows_c):
    batch, seq, _ = x.shape
    depth = w_in.shape[0]
    assert seq % t_attn == 0 and seq % tm == 0 and seq % rows_c == 0 and seq % GRID_W == 0

    bd64 = _block_diag_ones([(HALF, True)] * 4)
    bdb = _block_diag_ones([(B_NOPE, True), (B_ROPE, True), (32, False)] * 2)
    cosb, sinb, cosd, sind = _rope_tables(seq)
    bias_a_t = _bias_dense(rel_bias, t_attn, key_axis=0)
    bias_c = _bias_window(rel_bias)

    def cols_a(p):
        return (p, p, p, p, p, p)

    def cols_b(p):
        return (2 * p, 2 * p + 1, 2 * p, 2 * p + 1, p, p)

    def cols_b_fast(p):
        return (2 * p, 2 * p + 1, 2 * p, 2 * p + 1, 2 * p, 2 * p + 1)

    def cols_d(p):
        g = p // 2
        return (p, p, 2 * g, 2 * g + 1, g, g)

    def amax(v):
        return jnp.max(jnp.abs(v))

    x2d = x.reshape(batch * seq, D_MODEL)
    for layer in range(depth):
        lam_init = 0.8 - 0.6 * math.exp(-0.3 * layer)
        vec = _vector_table(g_mix[layer], g_qk_a[layer], g_cq[layer], g_ckv[layer],
                            g_qk_b[layer], g_qk_c[layer], g_qk_d[layer])
        wuq = jnp.pad(w_uq[layer].reshape(B_Q_LORA, B_HEADS, B_NOPE + B_ROPE),
                      ((0, 0), (0, 0), (0, 32))).reshape(B_Q_LORA, B_HEADS * LANES).astype(BF16)
        wukv = w_ukv[layer].reshape(B_KV_LORA, B_HEADS, B_NOPE + B_DV)
        wukvk = jnp.pad(wukv[:, :, :B_NOPE], ((0, 0), (0, 0), (0, HALF))).reshape(
            B_KV_LORA, B_HEADS * LANES).astype(BF16)
        wukvv = wukv[:, :, B_NOPE:].reshape(B_KV_LORA, B_HEADS * B_DV).astype(BF16)
        (qa, ka1, ka2, va, qb, kb, vb, qc, kc, vc, qd, kd, vd, gates, vat, vbt, vdt) = _front(
            x2d, vec, _layout_w_in(w_in[layer]), wuq, wukvk, wukvv, bd64, bdb,
            cosb, sinb, cosd, sind, seq, tm)
        qa, ka1, ka2, va, qb, kb, vb, qc, kc, vc, qd, kd, vd = (
            a.reshape(batch, seq, a.shape[1])
            for a in (qa, ka1, ka2, va, qb, kb, vb, qc, kc, vc, qd, kd, vd))

        ga, gb, gc, gd = g_qk_a[layer], g_qk_b[layer], g_qk_c[layer], g_qk_d[layer]
        bound_a = LOG2E * (A_DK ** 0.5 * amax(ga[0]) * amax(ga[1]) + amax(rel_bias[:, :A_HEADS]))
        norm_b = [jnp.sqrt(B_NOPE * amax(gb[s, :B_NOPE]) ** 2 + B_ROPE * amax(gb[s, B_NOPE:]) ** 2)
                  for s in range(2)]
        bound_b = LOG2E * (B_NOPE + B_ROPE) ** -0.5 * norm_b[0] * norm_b[1]
        bound_c = LOG2E * jnp.maximum(
            C_DH ** 0.5 * amax(gc[0]) * amax(gc[1]) + amax(rel_bias[:, A_HEADS:]),
            amax(sink_c[layer]))
        bound_d = LOG2E * D_DH ** 0.5 * amax(gd[0]) * amax(gd[1])

        diff_kw = dict(lam=lam[layer], gsub=g_sub_a[layer][None, :], lam_init=lam_init)
        o_a = _bounded(
            bound_a,
            lambda: _pair_attention(qa, ka1, ka2, vat, cols_a, A_HEADS, t_attn, "diff", True,
                                    bias=bias_a_t, **diff_kw),
            lambda: _pair_attention(qa, ka1, ka2, va, cols_a, A_HEADS, t_attn, "diff", False,
                                    bias=_bias_dense(rel_bias, t_attn, key_axis=1), **diff_kw))
        o_b = _bounded(
            bound_b,
            lambda: _pair_attention(qb, kb, kb, vbt, cols_b_fast, B_HEADS // 2, t_attn, "merge", True),
            lambda: _pair_attention(qb, kb, kb, vb, cols_b, B_HEADS // 2, t_attn, "merge", False))
        o_c = _bounded(
            bound_c,
            lambda: _window_attention(qc, kc, vc, bias_c, sink_c[layer], rows_c, True),
            lambda: _window_attention(qc, kc, vc, bias_c, sink_c[layer], rows_c, False))
        o_d = _bounded(
            bound_d,
            lambda: _pair_attention(qd, kd, kd, vdt, cols_d, D_HEADS // 2, t_attn, "merge", True),
            lambda: _pair_attention(qd, kd, kd, vd, cols_d, D_HEADS // 2, t_attn, "merge", False))

        kx, vx = _memkv(mem, g_mem[layer][None, :], g_qk_x[layer][1][None, :],
                        w_xkv[layer].astype(BF16))
        t2 = batch * seq
        x2d = _mix(o_a.reshape(t2, -1), o_b.reshape(t2, -1), o_c.reshape(t2, -1),
                   o_d.reshape(t2, -1), gates, x2d, w_branch[layer].astype(BF16),
                   w_out[layer].astype(BF16), g_x[layer][None, :],
                   (g_qk_x[layer][0] * (X_DH ** -0.5 * LOG2E))[None, :],
                   w_xq[layer].astype(BF16), kx, vx, w_xo[layer].astype(BF16), seq, tm)
        x2d = _mlp(x2d, g_mlp[layer][None, :], w_up[layer].astype(BF16),
                   w_down[layer].astype(BF16), tm)
    return x2d.reshape(batch, seq, D_MODEL)


def kernel(x, mem, rel_bias, g_mix, w_in, lam, g_qk_a, g_sub_a, g_cq, g_ckv, w_uq, w_ukv, g_qk_b, g_qk_c, sink_c, g_qk_d, w_branch, w_out, g_x, g_mem, w_xq, w_xkv, g_qk_x, w_xo, g_mlp, w_up, w_down):
    return _forward(x, mem, rel_bias, g_mix, w_in, lam, g_qk_a, g_sub_a, g_cq, g_ckv, w_uq, w_ukv,
                    g_qk_b, g_qk_c, sink_c, g_qk_d, w_branch, w_out, g_x, g_mem, w_xq, w_xkv,
                    g_qk_x, w_xo, g_mlp, w_up, w_down, t_attn=512, tm=256, rows_c=512)
```

```python
import functools
import math

import numpy as np
import jax
import jax.numpy as jnp
from jax import lax
from jax.experimental import pallas as pl
from jax.experimental.pallas import tpu as pltpu

F32 = jnp.float32
BF16 = jnp.bfloat16

D_MODEL = 1024
GRID_W = 64
N_MEM = 256
EPS = 1e-6
ROPE_THETA = 10000.0
NEG_INF = -1e30
LOG2E = 1.4426950408889634
SAFE_SCORE = 60.0

A_HEADS, A_DK, A_DV = 4, 64, 128
B_HEADS, B_Q_LORA, B_KV_LORA, B_NOPE, B_ROPE, B_DV = 8, 256, 128, 64, 32, 64
C_HEADS, C_KV_HEADS, C_DH, WINDOW = 8, 2, 64, 128
D_HEADS, D_KV_HEADS, D_DH = 8, 2, 64
X_HEADS, X_DH = 4, 128
D_FF = 4 * D_MODEL
REL_BUCKETS, REL_MAX_DIST = 32, 128
N_BRANCH, BRANCH_W = 4, 512

IN_SIZES = (512, 512, 512, B_Q_LORA, B_KV_LORA, B_ROPE, 512, 128, 128, 512, 128, 128,
            N_BRANCH * D_MODEL)
GROUP_W = 512
N_QKV_GROUPS = 7
N_GATE_GROUPS = N_BRANCH * D_MODEL // GROUP_W
IN_WIDTH_PADDED = (N_QKV_GROUPS + N_GATE_GROUPS) * GROUP_W

LANES = 128
HALF = 64
ONES_ROWS = 16
SEG = 256
VMEM_CAP = 60 * 1024 * 1024

(V_GMIX, V_AQ, V_AK, V_GCQ, V_GCKV, V_QB, V_QB_CNT, V_KB, V_KR, V_CQ, V_G5, V_DQ,
 N_VROWS) = range(13)


def _vmem_limit(*nbytes):
    return int(min(VMEM_CAP, sum(nbytes) + (6 << 20)))


def _const_spec(shape):
    zeros = (0,) * len(shape)
    return pl.BlockSpec(shape, lambda *_: zeros, pipeline_mode=pl.Buffered(1))


def _nbytes(shape, dtype):
    return int(np.prod(shape)) * jnp.dtype(dtype).itemsize


def _dot(a, b):
    return jnp.dot(a, b, preferred_element_type=F32)


def _dot_nt(a, b):
    return lax.dot_general(a, b, (((1,), (1,)), ((), ())), preferred_element_type=F32)


def _rms_rows(x, g):
    return x * lax.rsqrt(jnp.mean(x * x, axis=-1, keepdims=True) + EPS) * g


def _seg_rms(y, bd, inv_cnt, gain):
    outs = []
    for c in range(y.shape[1] // SEG):
        sl = slice(c * SEG, (c + 1) * SEG)
        yc = y[:, sl]
        ssq = _dot((yc * yc).astype(BF16), bd)
        outs.append(yc * lax.rsqrt(ssq * inv_cnt[:, sl] + EPS) * gain[:, sl])
    return outs[0] if len(outs) == 1 else jnp.concatenate(outs, axis=1)


def _rope32(x, cos, sin):
    w = x.shape[1]
    lane = lax.broadcasted_iota(jnp.int32, x.shape, 1)
    first = (lane & 16) == 0
    rot = jnp.where(first, -pltpu.roll(x, w - 16, 1), pltpu.roll(x, 16, 1))
    return x * cos + rot * sin


def _tile_lanes(v, n):
    return v if n == 1 else jnp.concatenate([v] * n, axis=1)


def _pair_split(x, fill):
    lane = lax.broadcasted_iota(jnp.int32, x.shape, 1)
    low = lane < HALF
    sw = pltpu.roll(x, HALF, 1)
    return (jnp.where(low, x, fill), jnp.where(low, fill, sw), jnp.where(low, sw, fill),
            jnp.where(low, fill, x))


def _with_ones_rows(vt, head_rows):
    ones = jnp.ones((ONES_ROWS, vt.shape[1]), vt.dtype)
    parts = []
    for hh in range(vt.shape[0] // head_rows):
        parts += [vt[hh * head_rows:(hh + 1) * head_rows], ones]
    return jnp.concatenate(parts, axis=0)


def _front_kernel(x_ref, vec_ref, win_ref, wuq_ref, wukvk_ref, wukvv_ref, bd64_ref, bdb_ref,
                  cosb_ref, sinb_ref, cosd_ref, sind_ref,
                  qa_ref, ka1_ref, ka2_ref, va_ref, qb_ref, kb_ref, vb_ref,
                  qc_ref, kc_ref, vc_ref, qd_ref, kd_ref, vd_ref, gate_ref,
                  vat_ref, vbt_ref, vdt_ref, qat_ref, qbt_ref, qdt_ref):
    def row(r, w):
        return vec_ref[r:r + 1, :w]

    ones = jnp.ones((1, D_MODEL), F32)
    inv64 = ones * (1.0 / HALF)
    bd64 = bd64_ref[...]
    h = _rms_rows(x_ref[...], row(V_GMIX, D_MODEL)).astype(BF16)

    def group(g):
        return _dot(h, win_ref[:, g * GROUP_W:(g + 1) * GROUP_W])

    lane512 = lax.broadcasted_iota(jnp.int32, (x_ref.shape[0], GROUP_W), 1)

    qa = _seg_rms(group(0), bd64, inv64, row(V_AQ, GROUP_W))
    qa_ref[...] = qa.astype(BF16)
    qat_ref[...] = qa.T.astype(BF16)
    ka = _seg_rms(group(1), bd64, inv64, row(V_AK, GROUP_W))
    first_map = (lane512 & HALF) == 0
    ka1_ref[...] = jnp.where(first_map, ka, 0.0).astype(BF16)
    ka2_ref[...] = jnp.where(first_map, 0.0, ka).astype(BF16)
    va = group(2)
    va_ref[...] = va.astype(BF16)
    vat_ref[...] = _with_ones_rows(va.T, A_DV).astype(BF16)

    y = group(3)
    cosb = cosb_ref[...]
    sinb = sinb_ref[...]
    cq = _rms_rows(y[:, :B_Q_LORA], row(V_GCQ, B_Q_LORA)).astype(BF16)
    qb = _seg_rms(_dot(cq, wuq_ref[...]), bdb_ref[...], row(V_QB_CNT, D_MODEL), row(V_QB, D_MODEL))
    qb = _rope32(qb, _tile_lanes(cosb, B_HEADS), _tile_lanes(sinb, B_HEADS))
    qb_ref[...] = qb.astype(BF16)
    qbt_ref[...] = qb.T.astype(BF16)
    ckv = _rms_rows(y[:, B_Q_LORA:B_Q_LORA + B_KV_LORA], row(V_GCKV, B_KV_LORA)).astype(BF16)
    k_nope = _seg_rms(_dot(ckv, wukvk_ref[...]), bd64, inv64, row(V_KB, D_MODEL))
    kr = y[:, B_Q_LORA + B_KV_LORA:]
    kr = kr * lax.rsqrt(jnp.sum(kr * kr, axis=-1, keepdims=True) * (1.0 / B_ROPE) + EPS)
    kr = _rope32(kr * row(V_KR, LANES), cosb, sinb)
    kb_ref[...] = (k_nope + _tile_lanes(kr, B_HEADS)).astype(BF16)
    vb = _dot(ckv, wukvv_ref[...])
    vb_ref[...] = vb.astype(BF16)
    vbt_ref[...] = _with_ones_rows(vb.T, B_DV).astype(BF16)

    cosd = cosd_ref[...]
    sind = sind_ref[...]
    y = group(5)
    yn = _seg_rms(y, bd64, inv64, row(V_G5, GROUP_W))
    kc = yn[:, 0:LANES]
    vc = y[:, LANES:2 * LANES]
    kd = _rope32(yn[:, 2 * LANES:3 * LANES], cosd, sind)
    vd = y[:, 3 * LANES:]
    kc_ref[...] = jnp.concatenate(_pair_split(kc, 0.0), axis=1).astype(BF16)
    kd_ref[...] = jnp.concatenate(_pair_split(kd, 0.0), axis=1).astype(BF16)
    vc_ref[...] = jnp.concatenate(_pair_split(vc, 1.0), axis=1).astype(BF16)
    low = lax.broadcasted_iota(jnp.int32, vd.shape, 1) < HALF
    vd_sw = pltpu.roll(vd, HALF, 1)
    vd_ref[...] = jnp.concatenate([jnp.where(low, vd, vd_sw), jnp.where(low, vd_sw, vd)],
                                  axis=1).astype(BF16)
    vdt_ref[...] = _with_ones_rows(vd.T, D_DH).astype(BF16)

    qc_ref[...] = _seg_rms(group(4), bd64, inv64, row(V_CQ, GROUP_W)).astype(BF16)
    qd = _seg_rms(group(6), bd64, inv64, row(V_DQ, GROUP_W))
    qd = _rope32(qd, _tile_lanes(cosd, 4), _tile_lanes(sind, 4))
    qd_ref[...] = qd.astype(BF16)
    qdt_ref[...] = qd.T.astype(BF16)

    for c in range(N_GATE_GROUPS):
        z = group(N_QKV_GROUPS + c)
        gate_ref[:, c * GROUP_W:(c + 1) * GROUP_W] = (1.0 / (1.0 + jnp.exp(-z))).astype(BF16)


def _front(x2d, vec, win, wuq, wukvk, wukvv, bd64, bdb, cosb, sinb, cosd, sind, seq, tm):
    t_tokens = x2d.shape[0]
    n_pos = seq // tm
    widths = (512, 512, 512, 512, 1024, 1024, 512, 512, 512, 512, 512, 512, 256, 4096)
    t_rows = (A_HEADS * (A_DV + ONES_ROWS), B_HEADS * (B_DV + ONES_ROWS),
              D_KV_HEADS * (D_DH + ONES_ROWS), A_HEADS * LANES, B_HEADS * LANES,
              D_HEADS * D_DH)

    def tile(w):
        return pl.BlockSpec((tm, w), lambda i: (i, 0))

    def pos(w):
        return pl.BlockSpec((tm, w), lambda i: (i % n_pos, 0))

    def tile_t(r):
        return pl.BlockSpec((None, r, tm), lambda i: (i // n_pos, 0, i % n_pos))

    consts = (vec, win, wuq, wukvk, wukvv, bd64, bdb)
    in_specs = [tile(D_MODEL)] + [_const_spec(c.shape) for c in consts] + [pos(LANES)] * 4
    limit = _vmem_limit(sum(_nbytes(c.shape, c.dtype) for c in consts),
                        2 * tm * (D_MODEL * 4 + 4 * LANES * 4 + (sum(widths) + sum(t_rows)) * 2),
                        12 * tm * D_MODEL * 4)
    return pl.pallas_call(
        _front_kernel,
        grid=(t_tokens // tm,),
        in_specs=in_specs,
        out_specs=[tile(w) for w in widths] + [tile_t(r) for r in t_rows],
        out_shape=([jax.ShapeDtypeStruct((t_tokens, w), BF16) for w in widths]
                   + [jax.ShapeDtypeStruct((t_tokens // seq, r, seq), BF16) for r in t_rows]),
        compiler_params=pltpu.CompilerParams(dimension_semantics=("arbitrary",),
                                             vmem_limit_bytes=limit),
        name="front",
    )(x2d, *consts, cosb, sinb, cosd, sind)


def _t5_bucket(rel):
    nb = REL_BUCKETS // 2
    max_exact = nb // 2
    n = jnp.abs(rel)
    nf = jnp.maximum(n, 1).astype(F32)
    large = max_exact + (jnp.log(nf / max_exact) / math.log(REL_MAX_DIST / max_exact)
                         * (nb - max_exact)).astype(jnp.int32)
    large = jnp.minimum(large, nb - 1)
    return jnp.where(rel > 0, nb, 0) + jnp.where(n < max_exact, n, large)


def _table_lookup(bucket, tbl_ref, col):
    acc = jnp.zeros(bucket.shape, F32)
    for b in range(REL_BUCKETS):
        acc = jnp.where(bucket == b, tbl_ref[b, col], acc)
    return acc


def _bias_dense_kernel(tbl_ref, o_ref, *, tk, key_axis):
    h = pl.program_id(0)
    d = pl.program_id(1) - 2
    rel = (d * tk + lax.broadcasted_iota(jnp.int32, o_ref.shape, key_axis)
           - lax.broadcasted_iota(jnp.int32, o_ref.shape, 1 - key_axis))
    o_ref[...] = _table_lookup(_t5_bucket(rel), tbl_ref, h) * LOG2E


def _bias_window_kernel(tbl_ref, o_ref):
    h = pl.program_id(0)
    shape = (LANES, 3 * LANES)
    rel = (lax.broadcasted_iota(jnp.int32, shape, 1) - LANES
           - lax.broadcasted_iota(jnp.int32, shape, 0))
    bias = _table_lookup(_t5_bucket(rel), tbl_ref, A_HEADS + h) * LOG2E
    o_ref[...] = jnp.where(jnp.abs(rel) <= WINDOW, bias, NEG_INF)


def _bias_dense(rel_bias, tq, tk, key_axis):
    assert tk >= LANES and tq % tk == 0
    n_tiles = tq // tk + 4
    shape = (tk, tq) if key_axis == 0 else (tq, tk)
    return pl.pallas_call(
        functools.partial(_bias_dense_kernel, tk=tk, key_axis=key_axis),
        grid=(A_HEADS, n_tiles),
        in_specs=[pl.BlockSpec(memory_space=pltpu.SMEM)],
        out_specs=pl.BlockSpec((None, None) + shape, lambda h, d: (h, d, 0, 0)),
        out_shape=jax.ShapeDtypeStruct((A_HEADS, n_tiles) + shape, F32),
        name="bias_dense",
    )(rel_bias)


def _bias_window(rel_bias):
    return pl.pallas_call(
        _bias_window_kernel,
        grid=(C_HEADS,),
        in_specs=[pl.BlockSpec(memory_space=pltpu.SMEM)],
        out_specs=pl.BlockSpec((None, LANES, 3 * LANES), lambda h: (h, 0, 0)),
        out_shape=jax.ShapeDtypeStruct((C_HEADS, LANES, 3 * LANES), F32),
        name="bias_window",
    )(rel_bias)


def _pair_attn_kernel(*refs, t, n_chunks, mode, lam_init):
    if mode == "diff":
        (qa_ref, qb_ref, ka_ref, kb_ref, va_ref, vb_ref, bias_ref, lam_ref, gsub_ref, o_ref,
         ma, la, acca, mb, lb, accb) = refs
    else:
        (qa_ref, qb_ref, ka_ref, kb_ref, va_ref, vb_ref, o_ref,
         ma, la, acca, mb, lb, accb) = refs
        bias_ref = None
    i = pl.program_id(2)
    streams = ((qa_ref[...], ka_ref, va_ref, ma, la, acca),
               (qb_ref[...], kb_ref, vb_ref, mb, lb, accb))
    for _, _, _, m, l, acc in streams:
        m[...] = jnp.full(m.shape, NEG_INF, F32)
        l[...] = jnp.zeros(l.shape, F32)
        acc[...] = jnp.zeros(acc.shape, F32)

    def body(c, carry):
        off = pl.multiple_of(c * t, t)
        bias = None if bias_ref is None else bias_ref[jnp.clip(c - i, -2, 2) + 2]
        for q, k_ref, v_ref, m, l, acc in streams:
            s = _dot_nt(q, k_ref[pl.ds(off, t), :])
            if bias is not None:
                s = s + bias
            m_prev = m[...]
            m_new = jnp.maximum(m_prev, jnp.max(s, axis=1, keepdims=True))
            alpha = jnp.exp2(m_prev - m_new)
            p = jnp.exp2(s - m_new)
            l[...] = alpha * l[...] + jnp.sum(p, axis=1, keepdims=True)
            acc[...] = alpha * acc[...] + _dot(p.astype(BF16), v_ref[pl.ds(off, t), :])
            m[...] = m_new
        return carry

    lax.fori_loop(0, n_chunks, body, 0)
    _finish_pair(acca[...] / la[...], accb[...] / lb[...], mode, refs, lam_init)


def _write_diff(oa, ob, lam_ref, gsub_ref, o_ref, lam_init):
    lam = lam_ref[...]
    lam_full = (jnp.exp(jnp.sum(lam[0:1] * lam[1:2], keepdims=True))
                - jnp.exp(jnp.sum(lam[2:3] * lam[3:4], keepdims=True)) + lam_init)
    o = oa - lam_full * ob
    o_ref[...] = (_rms_rows(o, gsub_ref[...]) * (1.0 - lam_init)).astype(BF16)


def _finish_pair(oa, ob, mode, refs, lam_init):
    if mode == "diff":
        _write_diff(oa, ob, *refs[7:10], lam_init)
    else:
        o_ref = refs[6]
        lane = lax.broadcasted_iota(jnp.int32, oa.shape, 1)
        o_ref[...] = jnp.where(lane < HALF, oa, ob).astype(BF16)


def _pair_attn_fast_kernel(*refs, tq, tk, n_chunks, mode, lam_init):
    qta_ref, qtb_ref, ka_ref, kb_ref, vta_ref, vtb_ref = refs[:6]
    bias_ref = refs[6] if mode == "diff" else None
    i = pl.program_id(2)
    ratio = tq // tk
    streams = ((qta_ref[...], ka_ref, vta_ref), (qtb_ref[...], kb_ref, vtb_ref))

    tasks = [(c, n_s) for c in range(n_chunks) for n_s in range(2)]

    def scores(c, n_s):
        qt, k_ref, _ = streams[n_s]
        s = _dot(k_ref[c * tk:(c + 1) * tk, :], qt)
        if bias_ref is not None:
            s = s + bias_ref[jnp.clip(c - i * ratio, -2, ratio + 1) + 2]
        return s

    sums = [None, None]
    s_live, p_live = {}, {}
    for j in range(len(tasks) + 2):
        if j < len(tasks):
            s_live[j] = scores(*tasks[j])
        if 0 <= j - 1 < len(tasks):
            p_live[j - 1] = jnp.exp2(s_live.pop(j - 1)).astype(BF16)
        if 0 <= j - 2 < len(tasks):
            c, n_s = tasks[j - 2]
            part = _dot(streams[n_s][2][:, c * tk:(c + 1) * tk], p_live.pop(j - 2))
            sums[n_s] = part if sums[n_s] is None else sums[n_s] + part
    n = vta_ref.shape[0] - ONES_ROWS
    a, b = sums
    oa = a[:n] / a[n:n + 1]
    ob = b[:n] / b[n:n + 1]
    if mode == "diff":
        _write_diff(oa.T, ob.T, *refs[7:10], lam_init)
    else:
        refs[6][...] = jnp.concatenate([oa, ob], axis=0).T.astype(BF16)


def _pair_attention(q, k_a, k_b, v, cols, n_pairs, t, mode, fast, tk=None, bias=None, lam=None,
                    gsub=None, lam_init=0.0):
    batch, seq = k_a.shape[:2]

    def kv_spec(which):
        return pl.BlockSpec((None, seq, LANES), lambda b, p, i: (b, 0, cols(p)[which]))

    if fast:
        v_rows = (A_DV if mode == "diff" else HALF) + ONES_ROWS

        def q_spec(which):
            return pl.BlockSpec((None, LANES, t), lambda b, p, i: (b, cols(p)[which], i))

        def v_spec(which):
            return pl.BlockSpec((None, v_rows, seq), lambda b, p, i: (b, cols(p)[which], 0))

        body = functools.partial(_pair_attn_fast_kernel, tq=t, tk=tk, n_chunks=seq // tk)
        scratch = []
        v_bytes = v_rows * seq * 2
    else:
        tk = t

        def q_spec(which):
            return pl.BlockSpec((None, t, LANES), lambda b, p, i: (b, i, cols(p)[which]))

        v_spec = kv_spec
        body = functools.partial(_pair_attn_kernel, t=t, n_chunks=seq // t)
        scratch = [pltpu.VMEM((t, 1), F32), pltpu.VMEM((t, 1), F32),
                   pltpu.VMEM((t, LANES), F32)] * 2
        v_bytes = seq * LANES * 2
    in_specs = [q_spec(0), q_spec(1), kv_spec(2), kv_spec(3), v_spec(4), v_spec(5)]
    args = [q, q, k_a, k_b, v, v]
    resident = 2 * 2 * (seq * LANES * 2 + v_bytes)
    if mode == "diff":
        in_specs += [pl.BlockSpec((None,) + bias.shape[1:], lambda b, p, i: (p, 0, 0, 0)),
                     _const_spec(lam.shape), _const_spec(gsub.shape)]
        args += [bias, lam, gsub]
        resident += 2 * _nbytes(bias.shape[1:], F32)
    limit = _vmem_limit(resident, 6 * t * LANES * 2, 8 * t * LANES * 4, 8 * t * tk * 4)
    return pl.pallas_call(
        functools.partial(body, mode=mode, lam_init=lam_init),
        grid=(batch, n_pairs, seq // t),
        in_specs=in_specs,
        out_specs=pl.BlockSpec((None, t, LANES), lambda b, p, i: (b, i, p)),
        out_shape=jax.ShapeDtypeStruct((batch, seq, n_pairs * LANES), BF16),
        scratch_shapes=scratch,
        compiler_params=pltpu.CompilerParams(
            dimension_semantics=("arbitrary", "arbitrary", "arbitrary"),
            vmem_limit_bytes=limit),
        name=("attn_fast_" if fast else "attn_") + mode,
    )(*args)


def _bounded(score_bound, fast, robust):
    return lax.cond(score_bound <= SAFE_SCORE, fast, robust)


def _window_kernel(sink_ref, q_ref, kp_ref, km_ref, kn_ref, vp_ref, vm_ref, vn_ref, bias_ref,
                   o_ref, *, rows, fast):
    i = pl.program_id(1)
    n_steps = pl.num_programs(1)
    nblk = rows // LANES
    lane = lax.broadcasted_iota(jnp.int32, (LANES, LANES), 1)
    zero_row = jnp.zeros((1, LANES), F32)

    def kv_block(j):
        if j < 0:
            return kp_ref[...], vp_ref[...]
        if j >= nblk:
            return kn_ref[...], vn_ref[...]
        return km_ref[j * LANES:(j + 1) * LANES, :], vm_ref[j * LANES:(j + 1) * LANES, :]

    for r in range(nblk):
        blocks = [kv_block(r + d) for d in (-1, 0, 1)]
        kcat = jnp.concatenate([blk[0] for blk in blocks], axis=0)
        vcat = jnp.concatenate([blk[1] for blk in blocks], axis=0)
        edge = None
        if r == 0 or r == nblk - 1:
            before = zero_row + jnp.where(i > 0, 0.0, NEG_INF) if r == 0 else zero_row
            after = zero_row + jnp.where(i < n_steps - 1, 0.0, NEG_INF) if r == nblk - 1 else zero_row
            edge = jnp.concatenate([before, zero_row, after], axis=1)
        for p in range(C_HEADS // 2):
            g = p // (C_HEADS // C_KV_HEADS // 2)
            q = q_ref[r * LANES:(r + 1) * LANES, p * LANES:(p + 1) * LANES]
            outs = []
            for e in range(2):
                head = 2 * p + e
                col = slice((2 * g + e) * LANES, (2 * g + e + 1) * LANES)
                s = _dot_nt(q, kcat[:, col]) + bias_ref[head]
                if edge is not None:
                    s = s + edge
                sink = sink_ref[head] * LOG2E
                if fast:
                    o = _dot(jnp.exp2(s).astype(BF16), vcat[:, col])
                    outs.append(o / (pltpu.roll(o, HALF, 1) + jnp.exp2(zero_row + sink)))
                else:
                    m = jnp.maximum(jnp.max(s, axis=1, keepdims=True), sink)
                    pr = jnp.exp2(s - m)
                    denom = jnp.sum(pr, axis=1, keepdims=True) + jnp.exp2(sink - m)
                    outs.append(_dot(pr.astype(BF16), vcat[:, col]) / denom)
            o_ref[r * LANES:(r + 1) * LANES, p * LANES:(p + 1) * LANES] = (
                jnp.where(lane < HALF, outs[0], outs[1]).astype(BF16))


def _window_attention(q, k, v, bias, sink, rows, fast):
    batch, seq, _ = q.shape
    nblk = rows // LANES
    last = seq // LANES - 1
    kw, vw = k.shape[2], v.shape[2]

    def main(w):
        return pl.BlockSpec((None, rows, w), lambda b, i: (b, i, 0))

    def prev(w):
        return pl.BlockSpec((None, LANES, w), lambda b, i: (b, jnp.maximum(i * nblk - 1, 0), 0))

    def nxt(w):
        return pl.BlockSpec((None, LANES, w), lambda b, i: (b, jnp.minimum((i + 1) * nblk, last), 0))

    limit = _vmem_limit(_nbytes(bias.shape, F32) * 2,
                        2 * (rows + 2 * LANES) * (kw + vw) * 2 + 4 * rows * 512 * 2,
                        16 * LANES * 3 * LANES * 4)
    return pl.pallas_call(
        functools.partial(_window_kernel, rows=rows, fast=fast),
        grid=(batch, seq // rows),
        in_specs=[pl.BlockSpec(memory_space=pltpu.SMEM), main(q.shape[2]),
                  prev(kw), main(kw), nxt(kw), prev(vw), main(vw), nxt(vw),
                  _const_spec(bias.shape)],
        out_specs=main(q.shape[2]),
        out_shape=jax.ShapeDtypeStruct(q.shape, BF16),
        compiler_params=pltpu.CompilerParams(dimension_semantics=("arbitrary", "arbitrary"),
                                             vmem_limit_bytes=limit),
        name="attn_window_fast" if fast else "attn_window",
    )(sink, q, k, k, k, v, v, v, bias)


def _memkv_kernel(mem_ref, gmem_ref, gk_ref, w_ref, k_ref, v_ref):
    hn = _rms_rows(mem_ref[...], gmem_ref[...]).astype(BF16)
    kv = _dot(hn, w_ref[...])
    width = X_HEADS * X_DH
    gk = gk_ref[...]
    for h in range(X_HEADS):
        sl = slice(h * X_DH, (h + 1) * X_DH)
        k_ref[:, sl] = _rms_rows(kv[:, sl], gk).astype(BF16)
    v_ref[...] = kv[:, width:].astype(BF16)


def _memkv(mem, gmem, gk, w):
    batch, n_mem, _ = mem.shape
    width = X_HEADS * X_DH
    out = pl.BlockSpec((None, n_mem, width), lambda b: (b, 0, 0))
    return pl.pallas_call(
        _memkv_kernel,
        grid=(batch,),
        in_specs=[pl.BlockSpec((None, n_mem, D_MODEL), lambda b: (b, 0, 0)),
                  _const_spec(gmem.shape), _const_spec(gk.shape), _const_spec(w.shape)],
        out_specs=[out, out],
        out_shape=[jax.ShapeDtypeStruct((batch, n_mem, width), BF16)] * 2,
        compiler_params=pltpu.CompilerParams(
            dimension_semantics=("arbitrary",),
            vmem_limit_bytes=_vmem_limit(_nbytes(w.shape, BF16), 4 * n_mem * D_MODEL * 4,
                                         4 * n_mem * 2 * width * 4)),
        name="memkv",
    )(mem, gmem, gk, w)


def _mix_kernel(oa_ref, ob_ref, oc_ref, od_ref, gate_ref, x_ref, wb_ref, wout_ref, gx_ref,
                gq_ref, wxq_ref, kx_ref, vx_ref, wxo_ref, o_ref):
    mix = None
    for m, o_m in enumerate((oa_ref, ob_ref, oc_ref, od_ref)):
        br = _dot(o_m[...], wb_ref[m])
        term = gate_ref[:, m * D_MODEL:(m + 1) * D_MODEL].astype(F32) * br
        mix = term if mix is None else mix + term
    x1 = x_ref[...] + _dot(mix.astype(BF16), wout_ref[...])
    hn = _rms_rows(x1, gx_ref[...]).astype(BF16)
    qx = _dot(hn, wxq_ref[...])
    gq = gq_ref[...]
    heads = []
    for h in range(X_HEADS):
        sl = slice(h * X_DH, (h + 1) * X_DH)
        qh = _rms_rows(qx[:, sl], gq).astype(BF16)
        s = _dot_nt(qh, kx_ref[:, sl])
        p = jnp.exp2(s - jnp.max(s, axis=1, keepdims=True))
        denom = jnp.sum(p, axis=1, keepdims=True)
        heads.append((_dot(p.astype(BF16), vx_ref[:, sl]) / denom).astype(BF16))
    ox = jnp.concatenate(heads, axis=1)
    o_ref[...] = x1 + _dot(ox, wxo_ref[...])


def _mix(o_a, o_b, o_c, o_d, gates, x2d, wb, wout, gx, gq, wxq, kx, vx, wxo, seq, tm):
    t_tokens = x2d.shape[0]
    per_batch = seq // tm
    n_mem, width = kx.shape[1], kx.shape[2]

    def tile(w):
        return pl.BlockSpec((tm, w), lambda i: (i, 0))

    mem_spec = pl.BlockSpec((None, n_mem, width), lambda i: (i // per_batch, 0, 0))
    consts = (wb, wout, gx, gq, wxq, wxo)
    limit = _vmem_limit(sum(_nbytes(c.shape, c.dtype) for c in consts),
                        2 * tm * (4 * BRANCH_W * 2 + 4 * D_MODEL * 2 + 2 * D_MODEL * 4),
                        4 * n_mem * width * 2, 10 * tm * D_MODEL * 4)
    return pl.pallas_call(
        _mix_kernel,
        grid=(t_tokens // tm,),
        in_specs=[tile(BRANCH_W)] * 4 + [tile(N_BRANCH * D_MODEL), tile(D_MODEL),
                  _const_spec(wb.shape), _const_spec(wout.shape), _const_spec(gx.shape),
                  _const_spec(gq.shape), _const_spec(wxq.shape), mem_spec, mem_spec,
                  _const_spec(wxo.shape)],
        out_specs=tile(D_MODEL),
        out_shape=jax.ShapeDtypeStruct((t_tokens, D_MODEL), F32),
        compiler_params=pltpu.CompilerParams(dimension_semantics=("arbitrary",),
                                             vmem_limit_bytes=limit),
        name="mix",
    )(o_a, o_b, o_c, o_d, gates, x2d, wb, wout, gx, gq, wxq, kx, vx, wxo)


def _mlp_kernel(x_ref, g_ref, wup_ref, wdown_ref, o_ref, *, chunk):
    x = x_ref[...]
    hn = _rms_rows(x, g_ref[...]).astype(BF16)
    acc = x
    for c in range(D_FF // chunk):
        u = jnp.maximum(_dot(hn, wup_ref[:, c * chunk:(c + 1) * chunk]), 0.0)
        acc = acc + _dot((u * u).astype(BF16), wdown_ref[c * chunk:(c + 1) * chunk, :])
    o_ref[...] = acc


def _mlp(x2d, g, wup, wdown, tm, chunk=512):
    t_tokens = x2d.shape[0]
    tile = pl.BlockSpec((tm, D_MODEL), lambda i: (i, 0))
    limit = _vmem_limit(_nbytes(wup.shape, BF16) + _nbytes(wdown.shape, BF16),
                        4 * tm * D_MODEL * 4, 8 * tm * max(chunk, D_MODEL) * 4)
    return pl.pallas_call(
        functools.partial(_mlp_kernel, chunk=chunk),
        grid=(t_tokens // tm,),
        in_specs=[tile, _const_spec(g.shape), _const_spec(wup.shape), _const_spec(wdown.shape)],
        out_specs=tile,
        out_shape=jax.ShapeDtypeStruct((t_tokens, D_MODEL), F32),
        compiler_params=pltpu.CompilerParams(dimension_semantics=("arbitrary",),
                                             vmem_limit_bytes=limit),
        name="mlp",
    )(x2d, g, wup, wdown)


def _block_diag_ones(sizes):
    m = np.zeros((SEG, SEG), np.float32)
    start = 0
    for size, on in sizes:
        if on:
            m[start:start + size, start:start + size] = 1.0
        start += size
    assert start == SEG
    return jnp.asarray(m, BF16)


def _rope_tables(seq):
    pos = jnp.arange(seq)
    inv = ROPE_THETA ** (-jnp.arange(0, B_ROPE, 2, dtype=F32) / B_ROPE)

    def cs(p):
        ang = p.astype(F32)[:, None] * inv[None, :]
        return (jnp.concatenate([jnp.cos(ang)] * 2, -1), jnp.concatenate([jnp.sin(ang)] * 2, -1))

    cos_p, sin_p = cs(pos)
    one = jnp.ones((seq, 1), F32)
    zero = jnp.zeros((seq, 1), F32)
    cosb = jnp.concatenate([one * jnp.ones((1, HALF)), cos_p, one * jnp.ones((1, 32))], 1)
    sinb = jnp.concatenate([zero * jnp.ones((1, HALF)), sin_p, zero * jnp.ones((1, 32))], 1)
    cos_r, sin_r = cs(pos // GRID_W)
    cos_c, sin_c = cs(pos % GRID_W)
    cosd = jnp.concatenate([cos_r, cos_c] * 2, 1)
    sind = jnp.concatenate([sin_r, sin_c] * 2, 1)
    return cosb, sinb, cosd, sind


def _layout_w_in(w):
    off = np.concatenate([[0], np.cumsum(IN_SIZES)])

    def seg(i):
        return w[:, off[i]:off[i + 1]]

    def zeros(n):
        return jnp.zeros((w.shape[0], n), w.dtype)

    cols = [seg(0), seg(1), seg(2), seg(3), seg(4), zeros(HALF), seg(5), zeros(32),
            seg(6), seg(7), seg(8), seg(10), seg(11), seg(9), seg(12)]
    out = jnp.concatenate(cols, axis=1).astype(BF16)
    assert out.shape[1] == IN_WIDTH_PADDED
    return out


def _pad_row(v):
    return jnp.pad(v, (0, D_MODEL - v.shape[0]))


def _vector_table(g_mix, g_qk_a, g_cq, g_ckv, g_qk_b, g_qk_c, g_qk_d):
    z32 = jnp.zeros((32,), F32)
    z64 = jnp.zeros((HALF,), F32)
    rows = [None] * N_VROWS
    rows[V_GMIX] = g_mix
    rows[V_AQ] = jnp.tile(g_qk_a[0], 8) * (A_DK ** -0.5 * LOG2E)
    rows[V_AK] = jnp.tile(g_qk_a[1], 8)
    rows[V_GCQ] = g_cq
    rows[V_GCKV] = g_ckv
    rows[V_QB] = jnp.tile(jnp.concatenate([g_qk_b[0], z32]), B_HEADS) * (
        (B_NOPE + B_ROPE) ** -0.5 * LOG2E)
    rows[V_QB_CNT] = jnp.tile(jnp.concatenate(
        [jnp.full((B_NOPE,), 1.0 / B_NOPE), jnp.full((B_ROPE,), 1.0 / B_ROPE), jnp.ones((32,))]),
        B_HEADS)
    rows[V_KB] = jnp.tile(jnp.concatenate([g_qk_b[1, :B_NOPE], z64]), B_HEADS)
    rows[V_KR] = jnp.concatenate([z64, g_qk_b[1, B_NOPE:], z32])
    rows[V_CQ] = jnp.tile(g_qk_c[0], 8) * (C_DH ** -0.5 * LOG2E)
    ones = jnp.ones((LANES,), F32)
    rows[V_G5] = jnp.concatenate([jnp.tile(g_qk_c[1], 2), ones, jnp.tile(g_qk_d[1], 2), ones])
    rows[V_DQ] = jnp.tile(g_qk_d[0], 8) * (D_DH ** -0.5 * LOG2E)
    return jnp.stack([_pad_row(r.astype(F32)) for r in rows])


def _forward(x, mem, rel_bias, g_mix, w_in, lam, g_qk_a, g_sub_a, g_cq, g_ckv, w_uq, w_ukv,
             g_qk_b, g_qk_c, sink_c, g_qk_d, w_branch, w_out, g_x, g_mem, w_xq, w_xkv,
             g_qk_x, w_xo, g_mlp, w_up, w_down, *, t_attn, tk_attn, tm, rows_c):
    batch, seq, _ = x.shape
    depth = w_in.shape[0]
    assert seq % t_attn == 0 and seq % tm == 0 and seq % rows_c == 0 and seq % GRID_W == 0

    bd64 = _block_diag_ones([(HALF, True)] * 4)
    bdb = _block_diag_ones([(B_NOPE, True), (B_ROPE, True), (32, False)] * 2)
    cosb, sinb, cosd, sind = _rope_tables(seq)
    bias_a_t = _bias_dense(rel_bias, t_attn, tk_attn, key_axis=0)
    bias_c = _bias_window(rel_bias)

    def cols_a(p):
        return (p, p, p, p, p, p)

    def cols_b(p):
        return (2 * p, 2 * p + 1, 2 * p, 2 * p + 1, p, p)

    def cols_b_fast(p):
        return (2 * p, 2 * p + 1, 2 * p, 2 * p + 1, 2 * p, 2 * p + 1)

    def cols_d(p):
        g = p // 2
        return (p, p, 2 * g, 2 * g + 1, g, g)

    def amax(v):
        return jnp.max(jnp.abs(v))

    x2d = x.reshape(batch * seq, D_MODEL)
    for layer in range(depth):
        lam_init = 0.8 - 0.6 * math.exp(-0.3 * layer)
        vec = _vector_table(g_mix[layer], g_qk_a[layer], g_cq[layer], g_ckv[layer],
                            g_qk_b[layer], g_qk_c[layer], g_qk_d[layer])
        wuq = jnp.pad(w_uq[layer].reshape(B_Q_LORA, B_HEADS, B_NOPE + B_ROPE),
                      ((0, 0), (0, 0), (0, 32))).reshape(B_Q_LORA, B_HEADS * LANES).astype(BF16)
        wukv = w_ukv[layer].reshape(B_KV_LORA, B_HEADS, B_NOPE + B_DV)
        wukvk = jnp.pad(wukv[:, :, :B_NOPE], ((0, 0), (0, 0), (0, HALF))).reshape(
            B_KV_LORA, B_HEADS * LANES).astype(BF16)
        wukvv = wukv[:, :, B_NOPE:].reshape(B_KV_LORA, B_HEADS * B_DV).astype(BF16)
        (qa, ka1, ka2, va, qb, kb, vb, qc, kc, vc, qd, kd, vd, gates, vat, vbt, vdt,
         qat, qbt, qdt) = _front(
            x2d, vec, _layout_w_in(w_in[layer]), wuq, wukvk, wukvv, bd64, bdb,
            cosb, sinb, cosd, sind, seq, tm)
        qa, ka1, ka2, va, qb, kb, vb, qc, kc, vc, qd, kd, vd = (
            a.reshape(batch, seq, a.shape[1])
            for a in (qa, ka1, ka2, va, qb, kb, vb, qc, kc, vc, qd, kd, vd))

        ga, gb, gc, gd = g_qk_a[layer], g_qk_b[layer], g_qk_c[layer], g_qk_d[layer]
        bound_a = LOG2E * (A_DK ** 0.5 * amax(ga[0]) * amax(ga[1]) + amax(rel_bias[:, :A_HEADS]))
        norm_b = [jnp.sqrt(B_NOPE * amax(gb[s, :B_NOPE]) ** 2 + B_ROPE * amax(gb[s, B_NOPE:]) ** 2)
                  for s in range(2)]
        bound_b = LOG2E * (B_NOPE + B_ROPE) ** -0.5 * norm_b[0] * norm_b[1]
        bound_c = LOG2E * jnp.maximum(
            C_DH ** 0.5 * amax(gc[0]) * amax(gc[1]) + amax(rel_bias[:, A_HEADS:]),
            amax(sink_c[layer]))
        bound_d = LOG2E * D_DH ** 0.5 * amax(gd[0]) * amax(gd[1])

        diff_kw = dict(lam=lam[layer], gsub=g_sub_a[layer][None, :], lam_init=lam_init)
        o_a = _bounded(
            bound_a,
            lambda: _pair_attention(qat, ka1, ka2, vat, cols_a, A_HEADS, t_attn, "diff", True,
                                    tk=tk_attn, bias=bias_a_t, **diff_kw),
            lambda: _pair_attention(qa, ka1, ka2, va, cols_a, A_HEADS, t_attn, "diff", False,
                                    bias=_bias_dense(rel_bias, t_attn, t_attn, key_axis=1),
                                    **diff_kw))
        o_b = _bounded(
            bound_b,
            lambda: _pair_attention(qbt, kb, kb, vbt, cols_b_fast, B_HEADS // 2, t_attn, "merge",
                                    True, tk=tk_attn),
            lambda: _pair_attention(qb, kb, kb, vb, cols_b, B_HEADS // 2, t_attn, "merge", False))
        o_c = _bounded(
            bound_c,
            lambda: _window_attention(qc, kc, vc, bias_c, sink_c[layer], rows_c, True),
            lambda: _window_attention(qc, kc, vc, bias_c, sink_c[layer], rows_c, False))
        o_d = _bounded(
            bound_d,
            lambda: _pair_attention(qdt, kd, kd, vdt, cols_d, D_HEADS // 2, t_attn, "merge", True,
                                    tk=tk_attn),
            lambda: _pair_attention(qd, kd, kd, vd, cols_d, D_HEADS // 2, t_attn, "merge", False))

        kx, vx = _memkv(mem, g_mem[layer][None, :], g_qk_x[layer][1][None, :],
                        w_xkv[layer].astype(BF16))
        t2 = batch * seq
        x2d = _mix(o_a.reshape(t2, -1), o_b.reshape(t2, -1), o_c.reshape(t2, -1),
                   o_d.reshape(t2, -1), gates, x2d, w_branch[layer].astype(BF16),
                   w_out[layer].astype(BF16), g_x[layer][None, :],
                   (g_qk_x[layer][0] * (X_DH ** -0.5 * LOG2E))[None, :],
                   w_xq[layer].astype(BF16), kx, vx, w_xo[layer].astype(BF16), seq, tm)
        x2d = _mlp(x2d, g_mlp[layer][None, :], w_up[layer].astype(BF16),
                   w_down[layer].astype(BF16), tm)
    return x2d.reshape(batch, seq, D_MODEL)


def kernel(x, mem, rel_bias, g_mix, w_in, lam, g_qk_a, g_sub_a, g_cq, g_ckv, w_uq, w_ukv, g_qk_b, g_qk_c, sink_c, g_qk_d, w_branch, w_out, g_x, g_mem, w_xq, w_xkv, g_qk_x, w_xo, g_mlp, w_up, w_down):
    return _forward(x, mem, rel_bias, g_mix, w_in, lam, g_qk_a, g_sub_a, g_cq, g_ckv, w_uq, w_ukv,
                    g_qk_b, g_qk_c, sink_c, g_qk_d, w_branch, w_out, g_x, g_mem, w_xq, w_xkv,
                    g_qk_x, w_xo, g_mlp, w_up, w_down, t_attn=512, tk_attn=256, tm=256, rows_c=512)
```

```python
import functools
import math

import numpy as np
import jax
import jax.numpy as jnp
from jax import lax
from jax.experimental import pallas as pl
from jax.experimental.pallas import tpu as pltpu

F32 = jnp.float32
BF16 = jnp.bfloat16

D_MODEL = 1024
GRID_W = 64
N_MEM = 256
EPS = 1e-6
ROPE_THETA = 10000.0
NEG_INF = -1e30
LOG2E = 1.4426950408889634
SAFE_SCORE = 60.0

A_HEADS, A_DK, A_DV = 4, 64, 128
B_HEADS, B_Q_LORA, B_KV_LORA, B_NOPE, B_ROPE, B_DV = 8, 256, 128, 64, 32, 64
C_HEADS, C_KV_HEADS, C_DH, WINDOW = 8, 2, 64, 128
D_HEADS, D_KV_HEADS, D_DH = 8, 2, 64
X_HEADS, X_DH = 4, 128
D_FF = 4 * D_MODEL
REL_BUCKETS, REL_MAX_DIST = 32, 128
N_BRANCH, BRANCH_W = 4, 512

IN_SIZES = (512, 512, 512, B_Q_LORA, B_KV_LORA, B_ROPE, 512, 128, 128, 512, 128, 128,
            N_BRANCH * D_MODEL)
GROUP_W = 512
N_QKV_GROUPS = 7
N_GATE_GROUPS = N_BRANCH * D_MODEL // GROUP_W
IN_WIDTH_PADDED = (N_QKV_GROUPS + N_GATE_GROUPS) * GROUP_W

LANES = 128
HALF = 64
ONES_ROWS = 16
SEG = 256
VMEM_CAP = 60 * 1024 * 1024

(V_GMIX, V_AQ, V_AK, V_GCQ, V_GCKV, V_QB, V_QB_CNT, V_KB, V_KR, V_CQ, V_G5, V_DQ,
 V_GSUB, V_GX, V_GQX, V_GMLP, V_GMEM, V_GKX, N_VROWS) = range(19)


def _vmem_limit(*nbytes):
    return int(min(VMEM_CAP, sum(nbytes) + (6 << 20)))


def _const_spec(shape):
    zeros = (0,) * len(shape)
    return pl.BlockSpec(shape, lambda *_: zeros, pipeline_mode=pl.Buffered(1))


def _layer_spec(stacked, layer):
    index = (layer,) + (0,) * (stacked.ndim - 1)
    return pl.BlockSpec((None,) + stacked.shape[1:], lambda *_: index,
                        pipeline_mode=pl.Buffered(1))


def _nbytes(shape, dtype):
    return int(np.prod(shape)) * jnp.dtype(dtype).itemsize


def _dot(a, b):
    return jnp.dot(a, b, preferred_element_type=F32)


def _dot_nt(a, b):
    return lax.dot_general(a, b, (((1,), (1,)), ((), ())), preferred_element_type=F32)


def _rms_rows(x, g):
    return x * lax.rsqrt(jnp.mean(x * x, axis=-1, keepdims=True) + EPS) * g


def _seg_ssq(y, bd):
    y2 = (y * y).astype(BF16)
    parts = [_dot(y2[:, c * SEG:(c + 1) * SEG], bd) for c in range(y.shape[1] // SEG)]
    return parts[0] if len(parts) == 1 else jnp.concatenate(parts, axis=1)


def _seg_scale(y, ssq, inv_cnt, gain):
    return y * lax.rsqrt(ssq * inv_cnt + EPS) * gain


def _rope32(x, cos, sin):
    w = x.shape[1]
    lane = lax.broadcasted_iota(jnp.int32, x.shape, 1)
    first = (lane & 16) == 0
    rot = jnp.where(first, -pltpu.roll(x, w - 16, 1), pltpu.roll(x, 16, 1))
    return x * cos + rot * sin


def _tile_lanes(v, n):
    return v if n == 1 else jnp.concatenate([v] * n, axis=1)


def _pair_split(x, fill):
    lane = lax.broadcasted_iota(jnp.int32, x.shape, 1)
    low = lane < HALF
    sw = pltpu.roll(x, HALF, 1)
    return (jnp.where(low, x, fill), jnp.where(low, fill, sw), jnp.where(low, sw, fill),
            jnp.where(low, fill, x))


def _with_ones_rows(vt, head_rows):
    ones = jnp.ones((ONES_ROWS, vt.shape[1]), vt.dtype)
    parts = []
    for hh in range(vt.shape[0] // head_rows):
        parts += [vt[hh * head_rows:(hh + 1) * head_rows], ones]
    return jnp.concatenate(parts, axis=0)


def _front_kernel(x_ref, vec_ref, win_ref, wuq_ref, wukvk_ref, wukvv_ref, bd64_ref, bdb_ref,
                  cosb_ref, sinb_ref, cosd_ref, sind_ref,
                  qa_ref, ka1_ref, ka2_ref, va_ref, qb_ref, kb_ref, vb_ref,
                  qc_ref, kc_ref, vc_ref, qd_ref, kd_ref, vd_ref, gate_ref,
                  vat_ref, vbt_ref, vdt_ref, qat_ref, qbt_ref, qdt_ref):
    def row(r, w):
        return vec_ref[r:r + 1, :w]

    inv64 = 1.0 / HALF
    bd64 = bd64_ref[...]
    h = _rms_rows(x_ref[...], row(V_GMIX, D_MODEL)).astype(BF16)

    def group(g):
        return _dot(h, win_ref[:, g * GROUP_W:(g + 1) * GROUP_W])

    lane512 = lax.broadcasted_iota(jnp.int32, (x_ref.shape[0], GROUP_W), 1)

    def gate(c, z):
        gate_ref[:, c * GROUP_W:(c + 1) * GROUP_W] = (1.0 / (1.0 + jnp.exp(-z))).astype(BF16)

    y0 = group(0)
    y1 = group(1)
    ssq0 = _seg_ssq(y0, bd64)
    y2 = group(2)
    ssq1 = _seg_ssq(y1, bd64)
    y3 = group(3)
    cq = _rms_rows(y3[:, :B_Q_LORA], row(V_GCQ, B_Q_LORA)).astype(BF16)
    ckv = _rms_rows(y3[:, B_Q_LORA:B_Q_LORA + B_KV_LORA], row(V_GCKV, B_KV_LORA)).astype(BF16)
    y4 = group(4)
    tq = _dot(cq, wuq_ref[...])
    tk = _dot(ckv, wukvk_ref[...])
    vb = _dot(ckv, wukvv_ref[...])
    y5 = group(5)
    ssq4 = _seg_ssq(y4, bd64)
    y6 = group(6)
    ssq_tq = _seg_ssq(tq, bdb_ref[...])
    ssq_tk = _seg_ssq(tk, bd64)
    z0 = group(N_QKV_GROUPS)
    ssq5 = _seg_ssq(y5, bd64)
    z1 = group(N_QKV_GROUPS + 1)
    ssq6 = _seg_ssq(y6, bd64)
    gate(0, z0)
    gate(1, z1)
    for c in range(2, N_GATE_GROUPS):
        gate(c, group(N_QKV_GROUPS + c))

    qa = _seg_scale(y0, ssq0, inv64, row(V_AQ, GROUP_W))
    qa_ref[...] = qa.astype(BF16)
    qat_ref[...] = qa.T.astype(BF16)
    ka = _seg_scale(y1, ssq1, inv64, row(V_AK, GROUP_W))
    first_map = (lane512 & HALF) == 0
    ka1_ref[...] = jnp.where(first_map, ka, 0.0).astype(BF16)
    ka2_ref[...] = jnp.where(first_map, 0.0, ka).astype(BF16)
    va_ref[...] = y2.astype(BF16)
    vat_ref[...] = _with_ones_rows(y2.T, A_DV).astype(BF16)

    cosb = cosb_ref[...]
    sinb = sinb_ref[...]
    qb = _seg_scale(tq, ssq_tq, row(V_QB_CNT, D_MODEL), row(V_QB, D_MODEL))
    qb = _rope32(qb, _tile_lanes(cosb, B_HEADS), _tile_lanes(sinb, B_HEADS))
    qb_ref[...] = qb.astype(BF16)
    qbt_ref[...] = qb.T.astype(BF16)
    k_nope = _seg_scale(tk, ssq_tk, inv64, row(V_KB, D_MODEL))
    kr = y3[:, B_Q_LORA + B_KV_LORA:]
    kr = kr * lax.rsqrt(jnp.sum(kr * kr, axis=-1, keepdims=True) * (1.0 / B_ROPE) + EPS)
    kr = _rope32(kr * row(V_KR, LANES), cosb, sinb)
    kb_ref[...] = (k_nope + _tile_lanes(kr, B_HEADS)).astype(BF16)
    vb_ref[...] = vb.astype(BF16)
    vbt_ref[...] = _with_ones_rows(vb.T, B_DV).astype(BF16)

    cosd = cosd_ref[...]
    sind = sind_ref[...]
    y = y5
    yn = _seg_scale(y5, ssq5, inv64, row(V_G5, GROUP_W))
    kc = yn[:, 0:LANES]
    vc = y[:, LANES:2 * LANES]
    kd = _rope32(yn[:, 2 * LANES:3 * LANES], cosd, sind)
    vd = y[:, 3 * LANES:]
    kc_ref[...] = jnp.concatenate(_pair_split(kc, 0.0), axis=1).astype(BF16)
    kd_ref[...] = jnp.concatenate(_pair_split(kd, 0.0), axis=1).astype(BF16)
    vc_ref[...] = jnp.concatenate(_pair_split(vc, 1.0), axis=1).astype(BF16)
    low = lax.broadcasted_iota(jnp.int32, vd.shape, 1) < HALF
    vd_sw = pltpu.roll(vd, HALF, 1)
    vd_ref[...] = jnp.concatenate([jnp.where(low, vd, vd_sw), jnp.where(low, vd_sw, vd)],
                                  axis=1).astype(BF16)
    vdt_ref[...] = _with_ones_rows(vd.T, D_DH).astype(BF16)

    qc_ref[...] = _seg_scale(y4, ssq4, inv64, row(V_CQ, GROUP_W)).astype(BF16)
    qd = _seg_scale(y6, ssq6, inv64, row(V_DQ, GROUP_W))
    qd = _rope32(qd, _tile_lanes(cosd, 4), _tile_lanes(sind, 4))
    qd_ref[...] = qd.astype(BF16)
    qdt_ref[...] = qd.T.astype(BF16)


def _front(x2d, layer, vec, win, wuq, wukvk, wukvv, bd64, bdb, cosb, sinb, cosd, sind, seq, tm):
    t_tokens = x2d.shape[0]
    n_pos = seq // tm
    widths = (512, 512, 512, 512, 1024, 1024, 512, 512, 512, 512, 512, 512, 256, 4096)
    t_rows = (A_HEADS * (A_DV + ONES_ROWS), B_HEADS * (B_DV + ONES_ROWS),
              D_KV_HEADS * (D_DH + ONES_ROWS), A_HEADS * LANES, B_HEADS * LANES,
              D_HEADS * D_DH)

    def tile(w):
        return pl.BlockSpec((tm, w), lambda i: (i, 0))

    def pos(w):
        return pl.BlockSpec((tm, w), lambda i: (i % n_pos, 0))

    def tile_t(r):
        return pl.BlockSpec((None, r, tm), lambda i: (i // n_pos, 0, i % n_pos))

    per_layer = (vec, win, wuq, wukvk, wukvv)
    consts = per_layer + (bd64, bdb)
    in_specs = ([tile(D_MODEL)] + [_layer_spec(c, layer) for c in per_layer]
                + [_const_spec(bd64.shape), _const_spec(bdb.shape)] + [pos(LANES)] * 4)
    limit = _vmem_limit(sum(_nbytes(c.shape[-2:], c.dtype) for c in consts),
                        2 * tm * (D_MODEL * 4 + 4 * LANES * 4 + (sum(widths) + sum(t_rows)) * 2),
                        12 * tm * D_MODEL * 4)
    return pl.pallas_call(
        _front_kernel,
        grid=(t_tokens // tm,),
        in_specs=in_specs,
        out_specs=[tile(w) for w in widths] + [tile_t(r) for r in t_rows],
        out_shape=([jax.ShapeDtypeStruct((t_tokens, w), BF16) for w in widths]
                   + [jax.ShapeDtypeStruct((t_tokens // seq, r, seq), BF16) for r in t_rows]),
        compiler_params=pltpu.CompilerParams(dimension_semantics=("arbitrary",),
                                             vmem_limit_bytes=limit),
        name="front",
    )(x2d, *consts, cosb, sinb, cosd, sind)


def _t5_bucket(rel):
    nb = REL_BUCKETS // 2
    max_exact = nb // 2
    n = jnp.abs(rel)
    nf = jnp.maximum(n, 1).astype(F32)
    large = max_exact + (jnp.log(nf / max_exact) / math.log(REL_MAX_DIST / max_exact)
                         * (nb - max_exact)).astype(jnp.int32)
    large = jnp.minimum(large, nb - 1)
    return jnp.where(rel > 0, nb, 0) + jnp.where(n < max_exact, n, large)


def _table_lookup(bucket, tbl_ref, col):
    acc = jnp.zeros(bucket.shape, F32)
    for b in range(REL_BUCKETS):
        acc = jnp.where(bucket == b, tbl_ref[b, col], acc)
    return acc


def _bias_dense_kernel(tbl_ref, o_ref, *, tk, key_axis):
    h = pl.program_id(0)
    d = pl.program_id(1) - 2
    rel = (d * tk + lax.broadcasted_iota(jnp.int32, o_ref.shape, key_axis)
           - lax.broadcasted_iota(jnp.int32, o_ref.shape, 1 - key_axis))
    o_ref[...] = _table_lookup(_t5_bucket(rel), tbl_ref, h) * LOG2E


def _bias_window_kernel(tbl_ref, o_ref):
    h = pl.program_id(0)
    shape = (LANES, 3 * LANES)
    rel = (lax.broadcasted_iota(jnp.int32, shape, 1) - LANES
           - lax.broadcasted_iota(jnp.int32, shape, 0))
    bias = _table_lookup(_t5_bucket(rel), tbl_ref, A_HEADS + h) * LOG2E
    o_ref[...] = jnp.where(jnp.abs(rel) <= WINDOW, bias, NEG_INF)


def _bias_dense(rel_bias, tq, tk, key_axis):
    assert tk >= LANES and tq % tk == 0
    n_tiles = tq // tk + 4
    shape = (tk, tq) if key_axis == 0 else (tq, tk)
    return pl.pallas_call(
        functools.partial(_bias_dense_kernel, tk=tk, key_axis=key_axis),
        grid=(A_HEADS, n_tiles),
        in_specs=[pl.BlockSpec(memory_space=pltpu.SMEM)],
        out_specs=pl.BlockSpec((None, None) + shape, lambda h, d: (h, d, 0, 0)),
        out_shape=jax.ShapeDtypeStruct((A_HEADS, n_tiles) + shape, F32),
        name="bias_dense",
    )(rel_bias)


def _bias_window(rel_bias):
    return pl.pallas_call(
        _bias_window_kernel,
        grid=(C_HEADS,),
        in_specs=[pl.BlockSpec(memory_space=pltpu.SMEM)],
        out_specs=pl.BlockSpec((None, LANES, 3 * LANES), lambda h: (h, 0, 0)),
        out_shape=jax.ShapeDtypeStruct((C_HEADS, LANES, 3 * LANES), F32),
        name="bias_window",
    )(rel_bias)


def _pair_attn_kernel(*refs, t, n_chunks, mode, lam_init):
    if mode == "diff":
        (qa_ref, qb_ref, ka_ref, kb_ref, va_ref, vb_ref, bias_ref, lam_ref, gsub_ref, o_ref,
         ma, la, acca, mb, lb, accb) = refs
    else:
        (qa_ref, qb_ref, ka_ref, kb_ref, va_ref, vb_ref, o_ref,
         ma, la, acca, mb, lb, accb) = refs
        bias_ref = None
    i = pl.program_id(2)
    streams = ((qa_ref[...], ka_ref, va_ref, ma, la, acca),
               (qb_ref[...], kb_ref, vb_ref, mb, lb, accb))
    for _, _, _, m, l, acc in streams:
        m[...] = jnp.full(m.shape, NEG_INF, F32)
        l[...] = jnp.zeros(l.shape, F32)
        acc[...] = jnp.zeros(acc.shape, F32)

    def body(c, carry):
        off = pl.multiple_of(c * t, t)
        bias = None if bias_ref is None else bias_ref[jnp.clip(c - i, -2, 2) + 2]
        for q, k_ref, v_ref, m, l, acc in streams:
            s = _dot_nt(q, k_ref[pl.ds(off, t), :])
            if bias is not None:
                s = s + bias
            m_prev = m[...]
            m_new = jnp.maximum(m_prev, jnp.max(s, axis=1, keepdims=True))
            alpha = jnp.exp2(m_prev - m_new)
            p = jnp.exp2(s - m_new)
            l[...] = alpha * l[...] + jnp.sum(p, axis=1, keepdims=True)
            acc[...] = alpha * acc[...] + _dot(p.astype(BF16), v_ref[pl.ds(off, t), :])
            m[...] = m_new
        return carry

    lax.fori_loop(0, n_chunks, body, 0)
    _finish_pair(acca[...] / la[...], accb[...] / lb[...], mode, refs, lam_init)


def _write_diff(oa, ob, lam_ref, vec_ref, o_ref, lam_init):
    lam = lam_ref[...]
    lam_full = (jnp.exp(jnp.sum(lam[0:1] * lam[1:2], keepdims=True))
                - jnp.exp(jnp.sum(lam[2:3] * lam[3:4], keepdims=True)) + lam_init)
    o = oa - lam_full * ob
    gsub = vec_ref[V_GSUB:V_GSUB + 1, :A_DV]
    o_ref[...] = (_rms_rows(o, gsub) * (1.0 - lam_init)).astype(BF16)


def _finish_pair(oa, ob, mode, refs, lam_init):
    if mode == "diff":
        _write_diff(oa, ob, *refs[7:10], lam_init)
    else:
        o_ref = refs[6]
        lane = lax.broadcasted_iota(jnp.int32, oa.shape, 1)
        o_ref[...] = jnp.where(lane < HALF, oa, ob).astype(BF16)


def _pair_attn_fast_kernel(*refs, tq, tk, n_chunks, mode, lam_init):
    qta_ref, qtb_ref, ka_ref, kb_ref, vta_ref, vtb_ref = refs[:6]
    bias_ref = refs[6] if mode == "diff" else None
    i = pl.program_id(2)
    ratio = tq // tk
    streams = ((qta_ref[...], ka_ref, vta_ref), (qtb_ref[...], kb_ref, vtb_ref))

    tasks = [(c, n_s) for c in range(n_chunks) for n_s in range(2)]

    def scores(c, n_s):
        qt, k_ref, _ = streams[n_s]
        s = _dot(k_ref[c * tk:(c + 1) * tk, :], qt)
        if bias_ref is not None:
            s = s + bias_ref[jnp.clip(c - i * ratio, -2, ratio + 1) + 2]
        return s

    sums = [None, None]
    s_live, p_live = {}, {}
    for j in range(len(tasks) + 2):
        if j < len(tasks):
            s_live[j] = scores(*tasks[j])
        if 0 <= j - 1 < len(tasks):
            p_live[j - 1] = jnp.exp2(s_live.pop(j - 1)).astype(BF16)
        if 0 <= j - 2 < len(tasks):
            c, n_s = tasks[j - 2]
            part = _dot(streams[n_s][2][:, c * tk:(c + 1) * tk], p_live.pop(j - 2))
            sums[n_s] = part if sums[n_s] is None else sums[n_s] + part
    n = vta_ref.shape[0] - ONES_ROWS
    a, b = sums
    oa = a[:n] / a[n:n + 1]
    ob = b[:n] / b[n:n + 1]
    if mode == "diff":
        _write_diff(oa.T, ob.T, *refs[7:10], lam_init)
    else:
        refs[6][...] = jnp.concatenate([oa, ob], axis=0).T.astype(BF16)


def _pair_attention(q, k_a, k_b, v, cols, n_pairs, t, mode, fast, tk=None, bias=None, lam=None,
                    vec=None, layer=None, lam_init=0.0):
    batch, seq = k_a.shape[:2]

    def kv_spec(which):
        return pl.BlockSpec((None, seq, LANES), lambda b, p, i: (b, 0, cols(p)[which]))

    if fast:
        v_rows = (A_DV if mode == "diff" else HALF) + ONES_ROWS

        def q_spec(which):
            return pl.BlockSpec((None, LANES, t), lambda b, p, i: (b, cols(p)[which], i))

        def v_spec(which):
            return pl.BlockSpec((None, v_rows, seq), lambda b, p, i: (b, cols(p)[which], 0))

        body = functools.partial(_pair_attn_fast_kernel, tq=t, tk=tk, n_chunks=seq // tk)
        scratch = []
        v_bytes = v_rows * seq * 2
    else:
        tk = t

        def q_spec(which):
            return pl.BlockSpec((None, t, LANES), lambda b, p, i: (b, i, cols(p)[which]))

        v_spec = kv_spec
        body = functools.partial(_pair_attn_kernel, t=t, n_chunks=seq // t)
        scratch = [pltpu.VMEM((t, 1), F32), pltpu.VMEM((t, 1), F32),
                   pltpu.VMEM((t, LANES), F32)] * 2
        v_bytes = seq * LANES * 2
    in_specs = [q_spec(0), q_spec(1), kv_spec(2), kv_spec(3), v_spec(4), v_spec(5)]
    args = [q, q, k_a, k_b, v, v]
    resident = 2 * 2 * (seq * LANES * 2 + v_bytes)
    if mode == "diff":
        in_specs += [pl.BlockSpec((None,) + bias.shape[1:], lambda b, p, i: (p, 0, 0, 0)),
                     _layer_spec(lam, layer), _layer_spec(vec, layer)]
        args += [bias, lam, vec]
        resident += 2 * _nbytes(bias.shape[1:], F32)
    limit = _vmem_limit(resident, 6 * t * LANES * 2, 8 * t * LANES * 4, 8 * t * tk * 4)
    return pl.pallas_call(
        functools.partial(body, mode=mode, lam_init=lam_init),
        grid=(batch, n_pairs, seq // t),
        in_specs=in_specs,
        out_specs=pl.BlockSpec((None, t, LANES), lambda b, p, i: (b, i, p)),
        out_shape=jax.ShapeDtypeStruct((batch, seq, n_pairs * LANES), BF16),
        scratch_shapes=scratch,
        compiler_params=pltpu.CompilerParams(
            dimension_semantics=("arbitrary", "arbitrary", "arbitrary"),
            vmem_limit_bytes=limit),
        name=("attn_fast_" if fast else "attn_") + mode,
    )(*args)


def _bounded(score_bound, fast, robust):
    return lax.cond(score_bound <= SAFE_SCORE, fast, robust)


def _window_kernel(sink_ref, q_ref, kp_ref, km_ref, kn_ref, vp_ref, vm_ref, vn_ref, bias_ref,
                   o_ref, *, rows, fast, layer):
    i = pl.program_id(1)
    n_steps = pl.num_programs(1)
    nblk = rows // LANES
    lane = lax.broadcasted_iota(jnp.int32, (LANES, LANES), 1)
    zero_row = jnp.zeros((1, LANES), F32)

    def kv_block(j):
        if j < 0:
            return kp_ref[...], vp_ref[...]
        if j >= nblk:
            return kn_ref[...], vn_ref[...]
        return km_ref[j * LANES:(j + 1) * LANES, :], vm_ref[j * LANES:(j + 1) * LANES, :]

    kcat, vcat, edge = [], [], []
    for r in range(nblk):
        blocks = [kv_block(r + d) for d in (-1, 0, 1)]
        kcat.append(jnp.concatenate([blk[0] for blk in blocks], axis=0))
        vcat.append(jnp.concatenate([blk[1] for blk in blocks], axis=0))
        if r == 0 or r == nblk - 1:
            before = zero_row + jnp.where(i > 0, 0.0, NEG_INF) if r == 0 else zero_row
            after = zero_row + jnp.where(i < n_steps - 1, 0.0, NEG_INF) if r == nblk - 1 else zero_row
            edge.append(jnp.concatenate([before, zero_row, after], axis=1))
        else:
            edge.append(None)

    def col(p, e):
        g = p // (C_HEADS // C_KV_HEADS // 2)
        return slice((2 * g + e) * LANES, (2 * g + e + 1) * LANES)

    def scores(r, p, e):
        q = q_ref[r * LANES:(r + 1) * LANES, p * LANES:(p + 1) * LANES]
        s = _dot_nt(q, kcat[r][:, col(p, e)]) + bias_ref[2 * p + e]
        return s if edge[r] is None else s + edge[r]

    def store(r, p, outs):
        o_ref[r * LANES:(r + 1) * LANES, p * LANES:(p + 1) * LANES] = (
            jnp.where(lane < HALF, outs[0], outs[1]).astype(BF16))

    tasks = [(r, p, e) for r in range(nblk) for p in range(C_HEADS // 2) for e in range(2)]
    if fast:
        s_live, p_live, outs = {}, {}, []
        for j in range(len(tasks) + 2):
            if j < len(tasks):
                s_live[j] = scores(*tasks[j])
            if 0 <= j - 1 < len(tasks):
                p_live[j - 1] = jnp.exp2(s_live.pop(j - 1)).astype(BF16)
            if 0 <= j - 2 < len(tasks):
                r, p, e = tasks[j - 2]
                o = _dot(p_live.pop(j - 2), vcat[r][:, col(p, e)])
                sink = sink_ref[layer, 2 * p + e] * LOG2E
                outs.append(o / (pltpu.roll(o, HALF, 1) + jnp.exp2(zero_row + sink)))
                if e == 1:
                    store(r, p, outs)
                    outs = []
    else:
        outs = []
        for r, p, e in tasks:
            s = scores(r, p, e)
            sink = sink_ref[layer, 2 * p + e] * LOG2E
            m = jnp.maximum(jnp.max(s, axis=1, keepdims=True), sink)
            pr = jnp.exp2(s - m)
            denom = jnp.sum(pr, axis=1, keepdims=True) + jnp.exp2(sink - m)
            outs.append(_dot(pr.astype(BF16), vcat[r][:, col(p, e)]) / denom)
            if e == 1:
                store(r, p, outs)
                outs = []


def _window_attention(q, k, v, bias, sink, layer, rows, fast):
    batch, seq, _ = q.shape
    nblk = rows // LANES
    last = seq // LANES - 1
    kw, vw = k.shape[2], v.shape[2]

    def main(w):
        return pl.BlockSpec((None, rows, w), lambda b, i: (b, i, 0))

    def prev(w):
        return pl.BlockSpec((None, LANES, w), lambda b, i: (b, jnp.maximum(i * nblk - 1, 0), 0))

    def nxt(w):
        return pl.BlockSpec((None, LANES, w), lambda b, i: (b, jnp.minimum((i + 1) * nblk, last), 0))

    limit = _vmem_limit(_nbytes(bias.shape, F32) * 2,
                        2 * (rows + 2 * LANES) * (kw + vw) * 2 + 4 * rows * 512 * 2,
                        16 * LANES * 3 * LANES * 4)
    return pl.pallas_call(
        functools.partial(_window_kernel, rows=rows, fast=fast, layer=layer),
        grid=(batch, seq // rows),
        in_specs=[pl.BlockSpec(memory_space=pltpu.SMEM), main(q.shape[2]),
                  prev(kw), main(kw), nxt(kw), prev(vw), main(vw), nxt(vw),
                  _const_spec(bias.shape)],
        out_specs=main(q.shape[2]),
        out_shape=jax.ShapeDtypeStruct(q.shape, BF16),
        compiler_params=pltpu.CompilerParams(dimension_semantics=("arbitrary", "arbitrary"),
                                             vmem_limit_bytes=limit),
        name="attn_window_fast" if fast else "attn_window",
    )(sink, q, k, k, k, v, v, v, bias)


def _memkv_kernel(mem_ref, vec_ref, w_ref, k_ref, v_ref):
    hn = _rms_rows(mem_ref[...], vec_ref[V_GMEM:V_GMEM + 1, :]).astype(BF16)
    kv = _dot(hn, w_ref[...])
    width = X_HEADS * X_DH
    gk = vec_ref[V_GKX:V_GKX + 1, :X_DH]
    for h in range(X_HEADS):
        sl = slice(h * X_DH, (h + 1) * X_DH)
        k_ref[:, sl] = _rms_rows(kv[:, sl], gk).astype(BF16)
    v_ref[...] = kv[:, width:].astype(BF16)


def _memkv(mem, layer, vec, w):
    batch, n_mem, _ = mem.shape
    width = X_HEADS * X_DH
    out = pl.BlockSpec((None, n_mem, width), lambda b: (b, 0, 0))
    return pl.pallas_call(
        _memkv_kernel,
        grid=(batch,),
        in_specs=[pl.BlockSpec((None, n_mem, D_MODEL), lambda b: (b, 0, 0)),
                  _layer_spec(vec, layer), _layer_spec(w, layer)],
        out_specs=[out, out],
        out_shape=[jax.ShapeDtypeStruct((batch, n_mem, width), BF16)] * 2,
        compiler_params=pltpu.CompilerParams(
            dimension_semantics=("arbitrary",),
            vmem_limit_bytes=_vmem_limit(_nbytes(w.shape[1:], BF16), 4 * n_mem * D_MODEL * 4,
                                         4 * n_mem * 2 * width * 4)),
        name="memkv",
    )(mem, vec, w)


def _mix_kernel(oa_ref, ob_ref, oc_ref, od_ref, gate_ref, x_ref, wb_ref, wout_ref, vec_ref,
                wxq_ref, kx_ref, vx_ref, wxo_ref, o_ref):
    mix = None
    for m, o_m in enumerate((oa_ref, ob_ref, oc_ref, od_ref)):
        br = _dot(o_m[...], wb_ref[m])
        term = gate_ref[:, m * D_MODEL:(m + 1) * D_MODEL].astype(F32) * br
        mix = term if mix is None else mix + term
    x1 = x_ref[...] + _dot(mix.astype(BF16), wout_ref[...])
    hn = _rms_rows(x1, vec_ref[V_GX:V_GX + 1, :]).astype(BF16)
    qx = _dot(hn, wxq_ref[...])
    gq = vec_ref[V_GQX:V_GQX + 1, :X_DH]

    def head_cols(h):
        return slice(h * X_DH, (h + 1) * X_DH)

    s_live, p_live, heads = {}, {}, []
    for h in range(X_HEADS + 2):
        if h < X_HEADS:
            qh = _rms_rows(qx[:, head_cols(h)], gq).astype(BF16)
            s_live[h] = _dot_nt(qh, kx_ref[:, head_cols(h)])
        if 0 <= h - 1 < X_HEADS:
            s = s_live.pop(h - 1)
            p = jnp.exp2(s - jnp.max(s, axis=1, keepdims=True))
            p_live[h - 1] = (p.astype(BF16), jnp.sum(p, axis=1, keepdims=True))
        if 0 <= h - 2 < X_HEADS:
            p, denom = p_live.pop(h - 2)
            heads.append((_dot(p, vx_ref[:, head_cols(h - 2)]) / denom).astype(BF16))
    ox = jnp.concatenate(heads, axis=1)
    o_ref[...] = x1 + _dot(ox, wxo_ref[...])


def _mix(o_a, o_b, o_c, o_d, gates, x2d, layer, wb, wout, vec, wxq, kx, vx, wxo, seq, tm):
    t_tokens = x2d.shape[0]
    per_batch = seq // tm
    n_mem, width = kx.shape[1], kx.shape[2]

    def tile(w):
        return pl.BlockSpec((tm, w), lambda i: (i, 0))

    def per_layer(a):
        return _layer_spec(a, layer)

    mem_spec = pl.BlockSpec((None, n_mem, width), lambda i: (i // per_batch, 0, 0))
    limit = _vmem_limit(sum(_nbytes(c.shape[1:], c.dtype) for c in (wb, wout, vec, wxq, wxo)),
                        2 * tm * (4 * BRANCH_W * 2 + 4 * D_MODEL * 2 + 2 * D_MODEL * 4),
                        4 * n_mem * width * 2, 10 * tm * D_MODEL * 4)
    return pl.pallas_call(
        _mix_kernel,
        grid=(t_tokens // tm,),
        in_specs=[tile(BRANCH_W)] * 4 + [tile(N_BRANCH * D_MODEL), tile(D_MODEL),
                  per_layer(wb), per_layer(wout), per_layer(vec), per_layer(wxq),
                  mem_spec, mem_spec, per_layer(wxo)],
        out_specs=tile(D_MODEL),
        out_shape=jax.ShapeDtypeStruct((t_tokens, D_MODEL), F32),
        compiler_params=pltpu.CompilerParams(dimension_semantics=("arbitrary",),
                                             vmem_limit_bytes=limit),
        name="mix",
    )(o_a, o_b, o_c, o_d, gates, x2d, wb, wout, vec, wxq, kx, vx, wxo)


def _mlp_kernel(x_ref, vec_ref, wup_ref, wdown_ref, o_ref, *, chunk):
    x = x_ref[...]
    hn = _rms_rows(x, vec_ref[V_GMLP:V_GMLP + 1, :]).astype(BF16)
    acc = x
    for c in range(D_FF // chunk):
        u = jnp.maximum(_dot(hn, wup_ref[:, c * chunk:(c + 1) * chunk]), 0.0)
        acc = acc + _dot((u * u).astype(BF16), wdown_ref[c * chunk:(c + 1) * chunk, :])
    o_ref[...] = acc


def _mlp(x2d, layer, vec, wup, wdown, tm, chunk=512):
    t_tokens = x2d.shape[0]
    tile = pl.BlockSpec((tm, D_MODEL), lambda i: (i, 0))
    limit = _vmem_limit(_nbytes(wup.shape[1:], BF16) + _nbytes(wdown.shape[1:], BF16),
                        4 * tm * D_MODEL * 4, 8 * tm * max(chunk, D_MODEL) * 4)
    return pl.pallas_call(
        functools.partial(_mlp_kernel, chunk=chunk),
        grid=(t_tokens // tm,),
        in_specs=[tile, _layer_spec(vec, layer), _layer_spec(wup, layer),
                  _layer_spec(wdown, layer)],
        out_specs=tile,
        out_shape=jax.ShapeDtypeStruct((t_tokens, D_MODEL), F32),
        compiler_params=pltpu.CompilerParams(dimension_semantics=("arbitrary",),
                                             vmem_limit_bytes=limit),
        name="mlp",
    )(x2d, vec, wup, wdown)


def _block_diag_ones(sizes):
    m = np.zeros((SEG, SEG), np.float32)
    start = 0
    for size, on in sizes:
        if on:
            m[start:start + size, start:start + size] = 1.0
        start += size
    assert start == SEG
    return jnp.asarray(m, BF16)


def _rope_tables(seq):
    pos = jnp.arange(seq)
    inv = ROPE_THETA ** (-jnp.arange(0, B_ROPE, 2, dtype=F32) / B_ROPE)

    def cs(p):
        ang = inv[:, None] * p.astype(F32)[None, :]
        return (jnp.concatenate([jnp.cos(ang)] * 2, 0).T, jnp.concatenate([jnp.sin(ang)] * 2, 0).T)

    cos_p, sin_p = cs(pos)
    one = jnp.ones((seq, 1), F32)
    zero = jnp.zeros((seq, 1), F32)
    cosb = jnp.concatenate([one * jnp.ones((1, HALF)), cos_p, one * jnp.ones((1, 32))], 1)
    sinb = jnp.concatenate([zero * jnp.ones((1, HALF)), sin_p, zero * jnp.ones((1, 32))], 1)
    cos_r, sin_r = cs(pos // GRID_W)
    cos_c, sin_c = cs(pos % GRID_W)
    cosd = jnp.concatenate([cos_r, cos_c] * 2, 1)
    sind = jnp.concatenate([sin_r, sin_c] * 2, 1)
    return cosb, sinb, cosd, sind


def _layout_w_in(w):
    off = np.concatenate([[0], np.cumsum(IN_SIZES)])

    def seg(i):
        return w[:, off[i]:off[i + 1]]

    def zeros(n):
        return jnp.zeros((w.shape[0], n), w.dtype)

    cols = [seg(0), seg(1), seg(2), seg(3), seg(4), zeros(HALF), seg(5), zeros(32),
            seg(6), seg(7), seg(8), seg(10), seg(11), seg(9), seg(12)]
    out = jnp.concatenate(cols, axis=1).astype(BF16)
    assert out.shape[1] == IN_WIDTH_PADDED
    return out


def _pad_row(v):
    return jnp.pad(v, (0, D_MODEL - v.shape[0]))


def _vector_table(g_mix, g_qk_a, g_cq, g_ckv, g_qk_b, g_qk_c, g_qk_d, g_sub_a, g_x, g_qk_x,
                  g_mlp, g_mem):
    z32 = jnp.zeros((32,), F32)
    z64 = jnp.zeros((HALF,), F32)
    rows = [None] * N_VROWS
    rows[V_GMIX] = g_mix
    rows[V_AQ] = jnp.tile(g_qk_a[0], 8) * (A_DK ** -0.5 * LOG2E)
    rows[V_AK] = jnp.tile(g_qk_a[1], 8)
    rows[V_GCQ] = g_cq
    rows[V_GCKV] = g_ckv
    rows[V_QB] = jnp.tile(jnp.concatenate([g_qk_b[0], z32]), B_HEADS) * (
        (B_NOPE + B_ROPE) ** -0.5 * LOG2E)
    rows[V_QB_CNT] = jnp.tile(jnp.concatenate(
        [jnp.full((B_NOPE,), 1.0 / B_NOPE), jnp.full((B_ROPE,), 1.0 / B_ROPE), jnp.ones((32,))]),
        B_HEADS)
    rows[V_KB] = jnp.tile(jnp.concatenate([g_qk_b[1, :B_NOPE], z64]), B_HEADS)
    rows[V_KR] = jnp.concatenate([z64, g_qk_b[1, B_NOPE:], z32])
    rows[V_CQ] = jnp.tile(g_qk_c[0], 8) * (C_DH ** -0.5 * LOG2E)
    ones = jnp.ones((LANES,), F32)
    rows[V_G5] = jnp.concatenate([jnp.tile(g_qk_c[1], 2), ones, jnp.tile(g_qk_d[1], 2), ones])
    rows[V_DQ] = jnp.tile(g_qk_d[0], 8) * (D_DH ** -0.5 * LOG2E)
    rows[V_GSUB] = g_sub_a
    rows[V_GX] = g_x
    rows[V_GQX] = g_qk_x[0] * (X_DH ** -0.5 * LOG2E)
    rows[V_GMLP] = g_mlp
    rows[V_GMEM] = g_mem
    rows[V_GKX] = g_qk_x[1]
    return jnp.stack([_pad_row(r.astype(F32)) for r in rows])


def _layout_latent_weights(w_uq, w_ukv):
    wuq = jnp.pad(w_uq.reshape(B_Q_LORA, B_HEADS, B_NOPE + B_ROPE),
                  ((0, 0), (0, 0), (0, 32))).reshape(B_Q_LORA, B_HEADS * LANES)
    wukv = w_ukv.reshape(B_KV_LORA, B_HEADS, B_NOPE + B_DV)
    wukvk = jnp.pad(wukv[:, :, :B_NOPE], ((0, 0), (0, 0), (0, HALF))).reshape(
        B_KV_LORA, B_HEADS * LANES)
    wukvv = wukv[:, :, B_NOPE:].reshape(B_KV_LORA, B_HEADS * B_DV)
    return wuq.astype(BF16), wukvk.astype(BF16), wukvv.astype(BF16)


def _score_bounds(rel_bias, ga, gb, gc, gd, sink):
    def amax(v):
        return jnp.max(jnp.abs(v))

    bound_a = A_DK ** 0.5 * amax(ga[0]) * amax(ga[1]) + amax(rel_bias[:, :A_HEADS])
    norm_b = [jnp.sqrt(B_NOPE * amax(gb[s, :B_NOPE]) ** 2 + B_ROPE * amax(gb[s, B_NOPE:]) ** 2)
              for s in range(2)]
    bound_b = (B_NOPE + B_ROPE) ** -0.5 * norm_b[0] * norm_b[1]
    bound_c = jnp.maximum(
        C_DH ** 0.5 * amax(gc[0]) * amax(gc[1]) + amax(rel_bias[:, A_HEADS:]), amax(sink))
    bound_d = D_DH ** 0.5 * amax(gd[0]) * amax(gd[1])
    return LOG2E * jnp.stack([bound_a, bound_b, bound_c, bound_d])


def _forward(x, mem, rel_bias, g_mix, w_in, lam, g_qk_a, g_sub_a, g_cq, g_ckv, w_uq, w_ukv,
             g_qk_b, g_qk_c, sink_c, g_qk_d, w_branch, w_out, g_x, g_mem, w_xq, w_xkv,
             g_qk_x, w_xo, g_mlp, w_up, w_down, *, t_attn, tk_attn, tm, tm_wide, rows_c):
    batch, seq, _ = x.shape
    depth = w_in.shape[0]
    assert seq % t_attn == 0 and seq % tm == 0 and seq % rows_c == 0 and seq % GRID_W == 0
    assert seq % tm_wide == 0

    bd64 = _block_diag_ones([(HALF, True)] * 4)
    bdb = _block_diag_ones([(B_NOPE, True), (B_ROPE, True), (32, False)] * 2)
    cosb, sinb, cosd, sind = _rope_tables(seq)
    bias_a_t = _bias_dense(rel_bias, t_attn, tk_attn, key_axis=0)
    bias_c = _bias_window(rel_bias)

    def cols_a(p):
        return (p, p, p, p, p, p)

    def cols_b(p):
        return (2 * p, 2 * p + 1, 2 * p, 2 * p + 1, p, p)

    def cols_b_fast(p):
        return (2 * p, 2 * p + 1, 2 * p, 2 * p + 1, 2 * p, 2 * p + 1)

    def cols_d(p):
        g = p // 2
        return (p, p, 2 * g, 2 * g + 1, g, g)

    vec = jax.vmap(_vector_table)(g_mix, g_qk_a, g_cq, g_ckv, g_qk_b, g_qk_c, g_qk_d, g_sub_a,
                                  g_x, g_qk_x, g_mlp, g_mem)
    win = jax.vmap(_layout_w_in)(w_in)
    wuq, wukvk, wukvv = jax.vmap(_layout_latent_weights)(w_uq, w_ukv)
    wb, wout, wxq, wxkv, wxo, wup, wdown = (
        w.astype(BF16) for w in (w_branch, w_out, w_xq, w_xkv, w_xo, w_up, w_down))
    bounds = jax.vmap(functools.partial(_score_bounds, rel_bias))(
        g_qk_a, g_qk_b, g_qk_c, g_qk_d, sink_c)

    x2d = x.reshape(batch * seq, D_MODEL)
    for layer in range(depth):
        lam_init = 0.8 - 0.6 * math.exp(-0.3 * layer)
        (qa, ka1, ka2, va, qb, kb, vb, qc, kc, vc, qd, kd, vd, gates, vat, vbt, vdt,
         qat, qbt, qdt) = _front(
            x2d, layer, vec, win, wuq, wukvk, wukvv, bd64, bdb, cosb, sinb, cosd, sind, seq, tm)
        qa, ka1, ka2, va, qb, kb, vb, qc, kc, vc, qd, kd, vd = (
            a.reshape(batch, seq, a.shape[1])
            for a in (qa, ka1, ka2, va, qb, kb, vb, qc, kc, vc, qd, kd, vd))
        bound_a, bound_b, bound_c, bound_d = (bounds[layer, n] for n in range(4))

        diff_kw = dict(lam=lam, vec=vec, layer=layer, lam_init=lam_init)
        o_a = _bounded(
            bound_a,
            lambda: _pair_attention(qat, ka1, ka2, vat, cols_a, A_HEADS, t_attn, "diff", True,
                                    tk=tk_attn, bias=bias_a_t, **diff_kw),
            lambda: _pair_attention(qa, ka1, ka2, va, cols_a, A_HEADS, t_attn, "diff", False,
                                    bias=_bias_dense(rel_bias, t_attn, t_attn, key_axis=1),
                                    **diff_kw))
        o_b = _bounded(
            bound_b,
            lambda: _pair_attention(qbt, kb, kb, vbt, cols_b_fast, B_HEADS // 2, t_attn, "merge",
                                    True, tk=tk_attn),
            lambda: _pair_attention(qb, kb, kb, vb, cols_b, B_HEADS // 2, t_attn, "merge", False))
        o_c = _bounded(
            bound_c,
            lambda: _window_attention(qc, kc, vc, bias_c, sink_c, layer, rows_c, True),
            lambda: _window_attention(qc, kc, vc, bias_c, sink_c, layer, rows_c, False))
        o_d = _bounded(
            bound_d,
            lambda: _pair_attention(qdt, kd, kd, vdt, cols_d, D_HEADS // 2, t_attn, "merge", True,
                                    tk=tk_attn),
            lambda: _pair_attention(qd, kd, kd, vd, cols_d, D_HEADS // 2, t_attn, "merge", False))

        kx, vx = _memkv(mem, layer, vec, wxkv)
        t2 = batch * seq
        x2d = _mix(o_a.reshape(t2, -1), o_b.reshape(t2, -1), o_c.reshape(t2, -1),
                   o_d.reshape(t2, -1), gates, x2d, layer, wb, wout, vec, wxq, kx, vx, wxo,
                   seq, tm_wide)
        x2d = _mlp(x2d, layer, vec, wup, wdown, tm_wide)
    return x2d.reshape(batch, seq, D_MODEL)


def kernel(x, mem, rel_bias, g_mix, w_in, lam, g_qk_a, g_sub_a, g_cq, g_ckv, w_uq, w_ukv, g_qk_b, g_qk_c, sink_c, g_qk_d, w_branch, w_out, g_x, g_mem, w_xq, w_xkv, g_qk_x, w_xo, g_mlp, w_up, w_down):
    return _forward(x, mem, rel_bias, g_mix, w_in, lam, g_qk_a, g_sub_a, g_cq, g_ckv, w_uq, w_ukv,
                    g_qk_b, g_qk_c, sink_c, g_qk_d, w_branch, w_out, g_x, g_mem, w_xq, w_xkv,
                    g_qk_x, w_xo, g_mlp, w_up, w_down, t_attn=512, tk_attn=256, tm=256, tm_wide=512,
                    rows_c=512)
```

```python
import functools
import math

import numpy as np
import jax
import jax.numpy as jnp
from jax import lax
from jax.experimental import pallas as pl
from jax.experimental.pallas import tpu as pltpu

F32 = jnp.float32
BF16 = jnp.bfloat16

D_MODEL = 1024
GRID_W = 64
N_MEM = 256
EPS = 1e-6
ROPE_THETA = 10000.0
NEG_INF = -1e30
LOG2E = 1.4426950408889634
SAFE_SCORE = 60.0

A_HEADS, A_DK, A_DV = 4, 64, 128
B_HEADS, B_Q_LORA, B_KV_LORA, B_NOPE, B_ROPE, B_DV = 8, 256, 128, 64, 32, 64
C_HEADS, C_KV_HEADS, C_DH, WINDOW = 8, 2, 64, 128
D_HEADS, D_KV_HEADS, D_DH = 8, 2, 64
X_HEADS, X_DH = 4, 128
D_FF = 4 * D_MODEL
REL_BUCKETS, REL_MAX_DIST = 32, 128
N_BRANCH, BRANCH_W = 4, 512

IN_SIZES = (512, 512, 512, B_Q_LORA, B_KV_LORA, B_ROPE, 512, 128, 128, 512, 128, 128,
            N_BRANCH * D_MODEL)
GROUP_W = 512
N_QKV_GROUPS = 7
N_GATE_GROUPS = N_BRANCH * D_MODEL // GROUP_W

LANES = 128
HALF = 64
SEG = 256
VMEM_CAP = 60 * 1024 * 1024

(V_GMIX, V_AQ, V_AK, V_GCQ, V_GCKV, V_QB, V_QB_CNT, V_KB, V_KR, V_CQ, V_G5, V_DQ,
 V_GSUB, V_GX, V_GQX, V_GMLP, V_GMEM, V_GKX, N_VROWS) = range(19)


def _vmem_limit(*nbytes):
    return int(min(VMEM_CAP, sum(nbytes) + (6 << 20)))


def _const_spec(shape):
    zeros = (0,) * len(shape)
    return pl.BlockSpec(shape, lambda *_: zeros, pipeline_mode=pl.Buffered(1))


def _layer_spec(stacked, layer):
    index = (layer,) + (0,) * (stacked.ndim - 1)
    return pl.BlockSpec((None,) + stacked.shape[1:], lambda *_: index,
                        pipeline_mode=pl.Buffered(1))


def _nbytes(shape, dtype):
    return int(np.prod(shape)) * jnp.dtype(dtype).itemsize


def _dot(a, b):
    return jnp.dot(a, b, preferred_element_type=F32)


def _dot_nt(a, b):
    return lax.dot_general(a, b, (((1,), (1,)), ((), ())), preferred_element_type=F32)


def _rms_rows(x, g):
    return x * lax.rsqrt(jnp.mean(x * x, axis=-1, keepdims=True) + EPS) * g


def _seg_ssq(y, bd):
    y2 = (y * y).astype(BF16)
    parts = [_dot(y2[:, c * SEG:(c + 1) * SEG], bd) for c in range(y.shape[1] // SEG)]
    return parts[0] if len(parts) == 1 else jnp.concatenate(parts, axis=1)


def _seg_scale(y, ssq, inv_cnt, gain):
    return y * lax.rsqrt(ssq * inv_cnt + EPS) * gain


def _rope32(x, cos, sin):
    w = x.shape[1]
    lane = lax.broadcasted_iota(jnp.int32, x.shape, 1)
    first = (lane & 16) == 0
    rot = jnp.where(first, -pltpu.roll(x, w - 16, 1), pltpu.roll(x, 16, 1))
    return x * cos + rot * sin


def _tile_lanes(v, n):
    return v if n == 1 else jnp.concatenate([v] * n, axis=1)


def _pair_split(x, fill):
    lane = lax.broadcasted_iota(jnp.int32, x.shape, 1)
    low = lane < HALF
    sw = pltpu.roll(x, HALF, 1)
    return (jnp.where(low, x, fill), jnp.where(low, fill, sw), jnp.where(low, sw, fill),
            jnp.where(low, fill, x))


def _front_kernel(x_ref, vec_ref, win_ref, wgate_ref, wuq_ref, wukvk_ref, wukvv_ref, bd64_ref,
                  bdb_ref,
                  cosb_ref, sinb_ref, cosd_ref, sind_ref,
                  qa_ref, ka1_ref, ka2_ref, va_ref, qb_ref, kb_ref, vb_ref,
                  qc_ref, kc_ref, vc_ref, qd_ref, kd_ref, vd_ref, gate_ref,
                  vat_ref, vbt_ref, vdt_ref, qat_ref, qbt_ref, qdt_ref):
    def row(r, w):
        return vec_ref[r:r + 1, :w]

    inv64 = 1.0 / HALF
    bd64 = bd64_ref[...]
    h = _rms_rows(x_ref[...], row(V_GMIX, D_MODEL)).astype(BF16)

    def group(g):
        w_ref, g = (win_ref, g) if g < N_QKV_GROUPS else (wgate_ref, g - N_QKV_GROUPS)
        return _dot(h, w_ref[:, g * GROUP_W:(g + 1) * GROUP_W])

    lane512 = lax.broadcasted_iota(jnp.int32, (x_ref.shape[0], GROUP_W), 1)

    def gate(c, z):
        gate_ref[:, c * GROUP_W:(c + 1) * GROUP_W] = (1.0 / (1.0 + jnp.exp(-z))).astype(BF16)

    y0 = group(0)
    y1 = group(1)
    ssq0 = _seg_ssq(y0, bd64)
    y2 = group(2)
    ssq1 = _seg_ssq(y1, bd64)
    y3 = group(3)
    cq = _rms_rows(y3[:, :B_Q_LORA], row(V_GCQ, B_Q_LORA)).astype(BF16)
    ckv = _rms_rows(y3[:, B_Q_LORA:B_Q_LORA + B_KV_LORA], row(V_GCKV, B_KV_LORA)).astype(BF16)
    y4 = group(4)
    tq = _dot(cq, wuq_ref[...])
    tk = _dot(ckv, wukvk_ref[...])
    vb = _dot(ckv, wukvv_ref[...])
    y5 = group(5)
    ssq4 = _seg_ssq(y4, bd64)
    y6 = group(6)
    ssq_tq = _seg_ssq(tq, bdb_ref[...])
    ssq_tk = _seg_ssq(tk, bd64)
    z0 = group(N_QKV_GROUPS)
    ssq5 = _seg_ssq(y5, bd64)
    z1 = group(N_QKV_GROUPS + 1)
    ssq6 = _seg_ssq(y6, bd64)
    gate(0, z0)
    gate(1, z1)
    for c in range(2, N_GATE_GROUPS):
        gate(c, group(N_QKV_GROUPS + c))

    qa = _seg_scale(y0, ssq0, inv64, row(V_AQ, GROUP_W))
    qa_ref[...] = qa.astype(BF16)
    qat_ref[...] = qa.T.astype(BF16)
    ka = _seg_scale(y1, ssq1, inv64, row(V_AK, GROUP_W))
    first_map = (lane512 & HALF) == 0
    ka1_ref[...] = jnp.where(first_map, ka, 0.0).astype(BF16)
    ka2_ref[...] = jnp.where(first_map, 0.0, ka).astype(BF16)
    va_ref[...] = y2.astype(BF16)
    vat_ref[...] = y2.T.astype(BF16)

    cosb = cosb_ref[...]
    sinb = sinb_ref[...]
    qb = _seg_scale(tq, ssq_tq, row(V_QB_CNT, D_MODEL), row(V_QB, D_MODEL))
    qb = _rope32(qb, _tile_lanes(cosb, B_HEADS), _tile_lanes(sinb, B_HEADS))
    qb_ref[...] = qb.astype(BF16)
    qbt_ref[...] = qb.T.astype(BF16)
    k_nope = _seg_scale(tk, ssq_tk, inv64, row(V_KB, D_MODEL))
    kr = y3[:, B_Q_LORA + B_KV_LORA:]
    kr = kr * lax.rsqrt(jnp.sum(kr * kr, axis=-1, keepdims=True) * (1.0 / B_ROPE) + EPS)
    kr = _rope32(kr * row(V_KR, LANES), cosb, sinb)
    kb_ref[...] = (k_nope + _tile_lanes(kr, B_HEADS)).astype(BF16)
    vb_ref[...] = vb.astype(BF16)
    vbt_ref[...] = vb.T.astype(BF16)

    cosd = cosd_ref[...]
    sind = sind_ref[...]
    y = y5
    yn = _seg_scale(y5, ssq5, inv64, row(V_G5, GROUP_W))
    kc = yn[:, 0:LANES]
    vc = y[:, LANES:2 * LANES]
    kd = _rope32(yn[:, 2 * LANES:3 * LANES], cosd, sind)
    vd = y[:, 3 * LANES:]
    kc_ref[...] = jnp.concatenate(_pair_split(kc, 0.0), axis=1).astype(BF16)
    kd_ref[...] = jnp.concatenate(_pair_split(kd, 0.0), axis=1).astype(BF16)
    vc_ref[...] = jnp.concatenate(_pair_split(vc, 1.0), axis=1).astype(BF16)
    low = lax.broadcasted_iota(jnp.int32, vd.shape, 1) < HALF
    vd_sw = pltpu.roll(vd, HALF, 1)
    vd_ref[...] = jnp.concatenate([jnp.where(low, vd, vd_sw), jnp.where(low, vd_sw, vd)],
                                  axis=1).astype(BF16)
    vdt_ref[...] = vd.T.astype(BF16)

    qc_ref[...] = _seg_scale(y4, ssq4, inv64, row(V_CQ, GROUP_W)).astype(BF16)
    qd = _seg_scale(y6, ssq6, inv64, row(V_DQ, GROUP_W))
    qd = _rope32(qd, _tile_lanes(cosd, 4), _tile_lanes(sind, 4))
    qd_ref[...] = qd.astype(BF16)
    qdt_ref[...] = qd.T.astype(BF16)


def _front(x2d, layer, vec, win, wgate, wuq, wukvk, wukvv, bd64, bdb, cosb, sinb, cosd, sind,
           seq, tm):
    t_tokens = x2d.shape[0]
    n_pos = seq // tm
    widths = (512, 512, 512, 512, 1024, 1024, 512, 512, 512, 512, 512, 512, 256, 4096)
    t_rows = (A_HEADS * A_DV, B_HEADS * B_DV, D_KV_HEADS * D_DH, A_HEADS * LANES,
              B_HEADS * LANES, D_HEADS * D_DH)

    def tile(w):
        return pl.BlockSpec((tm, w), lambda i: (i, 0))

    def pos(w):
        return pl.BlockSpec((tm, w), lambda i: (i % n_pos, 0))

    def tile_t(r):
        return pl.BlockSpec((None, r, tm), lambda i: (i // n_pos, 0, i % n_pos))

    per_layer = (vec, win, wgate, wuq, wukvk, wukvv)
    consts = per_layer + (bd64, bdb)
    in_specs = ([tile(D_MODEL)] + [_layer_spec(c, layer) for c in per_layer]
                + [_const_spec(bd64.shape), _const_spec(bdb.shape)] + [pos(LANES)] * 4)
    limit = _vmem_limit(sum(_nbytes(c.shape[-2:], c.dtype) for c in consts),
                        2 * tm * (D_MODEL * 4 + 4 * LANES * 4 + (sum(widths) + sum(t_rows)) * 2),
                        12 * tm * D_MODEL * 4)
    return pl.pallas_call(
        _front_kernel,
        grid=(t_tokens // tm,),
        in_specs=in_specs,
        out_specs=[tile(w) for w in widths] + [tile_t(r) for r in t_rows],
        out_shape=([jax.ShapeDtypeStruct((t_tokens, w), BF16) for w in widths]
                   + [jax.ShapeDtypeStruct((t_tokens // seq, r, seq), BF16) for r in t_rows]),
        compiler_params=pltpu.CompilerParams(dimension_semantics=("arbitrary",),
                                             vmem_limit_bytes=limit),
        name="front",
    )(x2d, *consts, cosb, sinb, cosd, sind)


def _t5_bucket(rel):
    nb = REL_BUCKETS // 2
    max_exact = nb // 2
    n = jnp.abs(rel)
    nf = jnp.maximum(n, 1).astype(F32)
    large = max_exact + (jnp.log(nf / max_exact) / math.log(REL_MAX_DIST / max_exact)
                         * (nb - max_exact)).astype(jnp.int32)
    large = jnp.minimum(large, nb - 1)
    return jnp.where(rel > 0, nb, 0) + jnp.where(n < max_exact, n, large)


def _table_lookup(bucket, tbl_ref, col):
    acc = jnp.zeros(bucket.shape, F32)
    for b in range(REL_BUCKETS):
        acc = jnp.where(bucket == b, tbl_ref[b, col], acc)
    return acc


def _bias_dense_kernel(tbl_ref, o_ref, *, tk, key_axis):
    h = pl.program_id(0)
    d = pl.program_id(1) - 2
    last = pl.num_programs(1) - 3
    half = REL_BUCKETS // 2

    @pl.when(d == -2)
    def _():
        o_ref[...] = jnp.full(o_ref.shape, tbl_ref[half - 1, h] * LOG2E, F32)

    @pl.when(d == last)
    def _():
        o_ref[...] = jnp.full(o_ref.shape, tbl_ref[REL_BUCKETS - 1, h] * LOG2E, F32)

    @pl.when((d > -2) & (d < last))
    def _():
        rel = (d * tk + lax.broadcasted_iota(jnp.int32, o_ref.shape, key_axis)
               - lax.broadcasted_iota(jnp.int32, o_ref.shape, 1 - key_axis))
        o_ref[...] = _table_lookup(_t5_bucket(rel), tbl_ref, h) * LOG2E


def _bias_window_kernel(tbl_ref, o_ref):
    h = pl.program_id(0)
    shape = (LANES, 3 * LANES)
    rel = (lax.broadcasted_iota(jnp.int32, shape, 1) - LANES
           - lax.broadcasted_iota(jnp.int32, shape, 0))
    bias = _table_lookup(_t5_bucket(rel), tbl_ref, A_HEADS + h) * LOG2E
    o_ref[...] = jnp.where(jnp.abs(rel) <= WINDOW, bias, NEG_INF)


def _bias_dense(rel_bias, tq, tk, key_axis):
    assert tk >= LANES and tq % tk == 0
    n_tiles = tq // tk + 4
    shape = (tk, tq) if key_axis == 0 else (tq, tk)
    return pl.pallas_call(
        functools.partial(_bias_dense_kernel, tk=tk, key_axis=key_axis),
        grid=(A_HEADS, n_tiles),
        in_specs=[pl.BlockSpec(memory_space=pltpu.SMEM)],
        out_specs=pl.BlockSpec((None, None) + shape, lambda h, d: (h, d, 0, 0)),
        out_shape=jax.ShapeDtypeStruct((A_HEADS, n_tiles) + shape, F32),
        name="bias_dense",
    )(rel_bias)


def _bias_window(rel_bias):
    return pl.pallas_call(
        _bias_window_kernel,
        grid=(C_HEADS,),
        in_specs=[pl.BlockSpec(memory_space=pltpu.SMEM)],
        out_specs=pl.BlockSpec((None, LANES, 3 * LANES), lambda h: (h, 0, 0)),
        out_shape=jax.ShapeDtypeStruct((C_HEADS, LANES, 3 * LANES), F32),
        name="bias_window",
    )(rel_bias)


def _pair_attn_kernel(*refs, t, n_chunks, mode, lam_init):
    if mode == "diff":
        (qa_ref, qb_ref, ka_ref, kb_ref, va_ref, vb_ref, bias_ref, lam_ref, gsub_ref, o_ref,
         ma, la, acca, mb, lb, accb) = refs
    else:
        (qa_ref, qb_ref, ka_ref, kb_ref, va_ref, vb_ref, o_ref,
         ma, la, acca, mb, lb, accb) = refs
        bias_ref = None
    i = pl.program_id(2)
    streams = ((qa_ref[...], ka_ref, va_ref, ma, la, acca),
               (qb_ref[...], kb_ref, vb_ref, mb, lb, accb))
    for _, _, _, m, l, acc in streams:
        m[...] = jnp.full(m.shape, NEG_INF, F32)
        l[...] = jnp.zeros(l.shape, F32)
        acc[...] = jnp.zeros(acc.shape, F32)

    def body(c, carry):
        off = pl.multiple_of(c * t, t)
        bias = None if bias_ref is None else bias_ref[jnp.clip(c - i, -2, 2) + 2]
        for q, k_ref, v_ref, m, l, acc in streams:
            s = _dot_nt(q, k_ref[pl.ds(off, t), :])
            if bias is not None:
                s = s + bias
            m_prev = m[...]
            m_new = jnp.maximum(m_prev, jnp.max(s, axis=1, keepdims=True))
            alpha = jnp.exp2(m_prev - m_new)
            p = jnp.exp2(s - m_new)
            l[...] = alpha * l[...] + jnp.sum(p, axis=1, keepdims=True)
            acc[...] = alpha * acc[...] + _dot(p.astype(BF16), v_ref[pl.ds(off, t), :])
            m[...] = m_new
        return carry

    lax.fori_loop(0, n_chunks, body, 0)
    _finish_pair(acca[...] / la[...], accb[...] / lb[...], mode, refs, lam_init)


def _write_diff(oa, ob, lam_ref, vec_ref, o_ref, lam_init):
    lam = lam_ref[...]
    lam_full = (jnp.exp(jnp.sum(lam[0:1] * lam[1:2], keepdims=True))
                - jnp.exp(jnp.sum(lam[2:3] * lam[3:4], keepdims=True)) + lam_init)
    o = oa - lam_full * ob
    gsub = vec_ref[V_GSUB:V_GSUB + 1, :A_DV]
    o_ref[...] = (_rms_rows(o, gsub) * (1.0 - lam_init)).astype(BF16)


def _finish_pair(oa, ob, mode, refs, lam_init):
    if mode == "diff":
        _write_diff(oa, ob, *refs[7:10], lam_init)
    else:
        o_ref = refs[6]
        lane = lax.broadcasted_iota(jnp.int32, oa.shape, 1)
        o_ref[...] = jnp.where(lane < HALF, oa, ob).astype(BF16)


def _pair_attn_fast_kernel(*refs, tq, tk, n_chunks, mode, lam_init):
    qta_ref, qtb_ref, ka_ref, kb_ref, vta_ref, vtb_ref = refs[:6]
    bias_ref = refs[6] if mode == "diff" else None
    i = pl.program_id(2)
    ratio = tq // tk
    streams = ((qta_ref[...], ka_ref, vta_ref), (qtb_ref[...], kb_ref, vtb_ref))

    tasks = [(c, n_s) for c in range(n_chunks) for n_s in range(2)]

    def scores(c, n_s):
        qt, k_ref, _ = streams[n_s]
        s = _dot(k_ref[c * tk:(c + 1) * tk, :], qt)
        if bias_ref is not None:
            s = s + bias_ref[jnp.clip(c - i * ratio, -2, ratio + 1) + 2]
        return s

    sums = [None, None]
    denoms = [None, None]
    s_live, p_live = {}, {}
    for j in range(len(tasks) + 2):
        if j < len(tasks):
            s_live[j] = scores(*tasks[j])
        if 0 <= j - 1 < len(tasks):
            n_s = tasks[j - 1][1]
            p = jnp.exp2(s_live.pop(j - 1))
            part = jnp.sum(p.reshape(tk // 8, 8, tq), axis=0)
            denoms[n_s] = part if denoms[n_s] is None else denoms[n_s] + part
            p_live[j - 1] = p.astype(BF16)
        if 0 <= j - 2 < len(tasks):
            c, n_s = tasks[j - 2]
            part = _dot(streams[n_s][2][:, c * tk:(c + 1) * tk], p_live.pop(j - 2))
            sums[n_s] = part if sums[n_s] is None else sums[n_s] + part
    oa = sums[0] / jnp.sum(denoms[0], axis=0, keepdims=True)
    ob = sums[1] / jnp.sum(denoms[1], axis=0, keepdims=True)
    if mode == "diff":
        _write_diff(oa.T, ob.T, *refs[7:10], lam_init)
    else:
        refs[6][...] = jnp.concatenate([oa, ob], axis=0).T.astype(BF16)


def _pair_attention(q, k_a, k_b, v, cols, n_pairs, t, mode, fast, tk=None, bias=None, lam=None,
                    vec=None, layer=None, lam_init=0.0):
    batch, seq = k_a.shape[:2]

    def kv_spec(which):
        return pl.BlockSpec((None, seq, LANES), lambda b, p, i: (b, 0, cols(p)[which]))

    if fast:
        v_rows = A_DV if mode == "diff" else HALF

        def q_spec(which):
            return pl.BlockSpec((None, LANES, t), lambda b, p, i: (b, cols(p)[which], i))

        def v_spec(which):
            return pl.BlockSpec((None, v_rows, seq), lambda b, p, i: (b, cols(p)[which], 0))

        body = functools.partial(_pair_attn_fast_kernel, tq=t, tk=tk, n_chunks=seq // tk)
        scratch = []
        v_bytes = v_rows * seq * 2
    else:
        tk = t

        def q_spec(which):
            return pl.BlockSpec((None, t, LANES), lambda b, p, i: (b, i, cols(p)[which]))

        v_spec = kv_spec
        body = functools.partial(_pair_attn_kernel, t=t, n_chunks=seq // t)
        scratch = [pltpu.VMEM((t, 1), F32), pltpu.VMEM((t, 1), F32),
                   pltpu.VMEM((t, LANES), F32)] * 2
        v_bytes = seq * LANES * 2
    in_specs = [q_spec(0), q_spec(1), kv_spec(2), kv_spec(3), v_spec(4), v_spec(5)]
    args = [q, q, k_a, k_b, v, v]
    resident = 2 * 2 * (seq * LANES * 2 + v_bytes)
    if mode == "diff":
        in_specs += [pl.BlockSpec((None,) + bias.shape[1:], lambda b, p, i: (p, 0, 0, 0)),
                     _layer_spec(lam, layer), _layer_spec(vec, layer)]
        args += [bias, lam, vec]
        resident += 2 * _nbytes(bias.shape[1:], F32)
    limit = _vmem_limit(resident, 6 * t * LANES * 2, 8 * t * LANES * 4, 8 * t * tk * 4)
    return pl.pallas_call(
        functools.partial(body, mode=mode, lam_init=lam_init),
        grid=(batch, n_pairs, seq // t),
        in_specs=in_specs,
        out_specs=pl.BlockSpec((None, t, LANES), lambda b, p, i: (b, i, p)),
        out_shape=jax.ShapeDtypeStruct((batch, seq, n_pairs * LANES), BF16),
        scratch_shapes=scratch,
        compiler_params=pltpu.CompilerParams(
            dimension_semantics=("arbitrary", "arbitrary", "arbitrary"),
            vmem_limit_bytes=limit),
        name=("attn_fast_" if fast else "attn_") + mode,
    )(*args)


def _bounded(score_bound, fast, robust):
    return lax.cond(score_bound <= SAFE_SCORE, fast, robust)


def _window_kernel(sink_ref, q_ref, kp_ref, km_ref, kn_ref, vp_ref, vm_ref, vn_ref, bias_ref,
                   o_ref, *, rows, fast, layer):
    i = pl.program_id(1)
    n_steps = pl.num_programs(1)
    nblk = rows // LANES
    lane = lax.broadcasted_iota(jnp.int32, (LANES, LANES), 1)
    zero_row = jnp.zeros((1, LANES), F32)

    def kv_block(j):
        if j < 0:
            return kp_ref[...], vp_ref[...]
        if j >= nblk:
            return kn_ref[...], vn_ref[...]
        return km_ref[j * LANES:(j + 1) * LANES, :], vm_ref[j * LANES:(j + 1) * LANES, :]

    kcat, vcat, edge = [], [], []
    for r in range(nblk):
        blocks = [kv_block(r + d) for d in (-1, 0, 1)]
        kcat.append(jnp.concatenate([blk[0] for blk in blocks], axis=0))
        vcat.append(jnp.concatenate([blk[1] for blk in blocks], axis=0))
        if r == 0 or r == nblk - 1:
            before = zero_row + jnp.where(i > 0, 0.0, NEG_INF) if r == 0 else zero_row
            after = zero_row + jnp.where(i < n_steps - 1, 0.0, NEG_INF) if r == nblk - 1 else zero_row
            edge.append(jnp.concatenate([before, zero_row, after], axis=1))
        else:
            edge.append(None)

    def col(p, e):
        g = p // (C_HEADS // C_KV_HEADS // 2)
        return slice((2 * g + e) * LANES, (2 * g + e + 1) * LANES)

    def scores(r, p, e):
        q = q_ref[r * LANES:(r + 1) * LANES, p * LANES:(p + 1) * LANES]
        s = _dot_nt(q, kcat[r][:, col(p, e)]) + bias_ref[2 * p + e]
        return s if edge[r] is None else s + edge[r]

    def store(r, p, outs):
        o_ref[r * LANES:(r + 1) * LANES, p * LANES:(p + 1) * LANES] = (
            jnp.where(lane < HALF, outs[0], outs[1]).astype(BF16))

    tasks = [(r, p, e) for r in range(nblk) for p in range(C_HEADS // 2) for e in range(2)]
    if fast:
        s_live, p_live, outs = {}, {}, []
        for j in range(len(tasks) + 2):
            if j < len(tasks):
                s_live[j] = scores(*tasks[j])
            if 0 <= j - 1 < len(tasks):
                p_live[j - 1] = jnp.exp2(s_live.pop(j - 1)).astype(BF16)
            if 0 <= j - 2 < len(tasks):
                r, p, e = tasks[j - 2]
                o = _dot(p_live.pop(j - 2), vcat[r][:, col(p, e)])
                sink = sink_ref[layer, 2 * p + e] * LOG2E
                outs.append(o / (pltpu.roll(o, HALF, 1) + jnp.exp2(zero_row + sink)))
                if e == 1:
                    store(r, p, outs)
                    outs = []
    else:
        outs = []
        for r, p, e in tasks:
            s = scores(r, p, e)
            sink = sink_ref[layer, 2 * p + e] * LOG2E
            m = jnp.maximum(jnp.max(s, axis=1, keepdims=True), sink)
            pr = jnp.exp2(s - m)
            denom = jnp.sum(pr, axis=1, keepdims=True) + jnp.exp2(sink - m)
            outs.append(_dot(pr.astype(BF16), vcat[r][:, col(p, e)]) / denom)
            if e == 1:
                store(r, p, outs)
                outs = []


def _window_attention(q, k, v, bias, sink, layer, rows, fast):
    batch, seq, _ = q.shape
    nblk = rows // LANES
    last = seq // LANES - 1
    kw, vw = k.shape[2], v.shape[2]

    def main(w):
        return pl.BlockSpec((None, rows, w), lambda b, i: (b, i, 0))

    def prev(w):
        return pl.BlockSpec((None, LANES, w), lambda b, i: (b, jnp.maximum(i * nblk - 1, 0), 0))

    def nxt(w):
        return pl.BlockSpec((None, LANES, w), lambda b, i: (b, jnp.minimum((i + 1) * nblk, last), 0))

    limit = _vmem_limit(_nbytes(bias.shape, F32) * 2,
                        2 * (rows + 2 * LANES) * (kw + vw) * 2 + 4 * rows * 512 * 2,
                        16 * LANES * 3 * LANES * 4)
    return pl.pallas_call(
        functools.partial(_window_kernel, rows=rows, fast=fast, layer=layer),
        grid=(batch, seq // rows),
        in_specs=[pl.BlockSpec(memory_space=pltpu.SMEM), main(q.shape[2]),
                  prev(kw), main(kw), nxt(kw), prev(vw), main(vw), nxt(vw),
                  _const_spec(bias.shape)],
        out_specs=main(q.shape[2]),
        out_shape=jax.ShapeDtypeStruct(q.shape, BF16),
        compiler_params=pltpu.CompilerParams(dimension_semantics=("arbitrary", "arbitrary"),
                                             vmem_limit_bytes=limit),
        name="attn_window_fast" if fast else "attn_window",
    )(sink, q, k, k, k, v, v, v, bias)


def _memkv_kernel(mem_ref, vec_ref, w_ref, k_ref, v_ref):
    hn = _rms_rows(mem_ref[...], vec_ref[V_GMEM:V_GMEM + 1, :]).astype(BF16)
    kv = _dot(hn, w_ref[...])
    width = X_HEADS * X_DH
    gk = vec_ref[V_GKX:V_GKX + 1, :X_DH]
    for h in range(X_HEADS):
        sl = slice(h * X_DH, (h + 1) * X_DH)
        k_ref[:, sl] = _rms_rows(kv[:, sl], gk).astype(BF16)
    v_ref[...] = kv[:, width:].astype(BF16)


def _memkv(mem, layer, vec, w):
    batch, n_mem, _ = mem.shape
    width = X_HEADS * X_DH
    out = pl.BlockSpec((None, n_mem, width), lambda b: (b, 0, 0))
    return pl.pallas_call(
        _memkv_kernel,
        grid=(batch,),
        in_specs=[pl.BlockSpec((None, n_mem, D_MODEL), lambda b: (b, 0, 0)),
                  _layer_spec(vec, layer), _layer_spec(w, layer)],
        out_specs=[out, out],
        out_shape=[jax.ShapeDtypeStruct((batch, n_mem, width), BF16)] * 2,
        compiler_params=pltpu.CompilerParams(
            dimension_semantics=("arbitrary",),
            vmem_limit_bytes=_vmem_limit(_nbytes(w.shape[1:], BF16), 4 * n_mem * D_MODEL * 4,
                                         4 * n_mem * 2 * width * 4)),
        name="memkv",
    )(mem, vec, w)


def _mix_kernel(oa_ref, ob_ref, oc_ref, od_ref, gate_ref, x_ref, wb_ref, wout_ref, vec_ref,
                wxq_ref, kx_ref, vx_ref, wxo_ref, o_ref):
    mix = None
    for m, o_m in enumerate((oa_ref, ob_ref, oc_ref, od_ref)):
        br = _dot(o_m[...], wb_ref[m])
        term = gate_ref[:, m * D_MODEL:(m + 1) * D_MODEL].astype(F32) * br
        mix = term if mix is None else mix + term
    x1 = x_ref[...] + _dot(mix.astype(BF16), wout_ref[...])
    hn = _rms_rows(x1, vec_ref[V_GX:V_GX + 1, :]).astype(BF16)
    qx = _dot(hn, wxq_ref[...])
    gq = vec_ref[V_GQX:V_GQX + 1, :X_DH]

    def head_cols(h):
        return slice(h * X_DH, (h + 1) * X_DH)

    s_live, p_live, heads = {}, {}, []
    for h in range(X_HEADS + 2):
        if h < X_HEADS:
            qh = _rms_rows(qx[:, head_cols(h)], gq).astype(BF16)
            s_live[h] = _dot_nt(qh, kx_ref[:, head_cols(h)])
        if 0 <= h - 1 < X_HEADS:
            s = s_live.pop(h - 1)
            p = jnp.exp2(s - jnp.max(s, axis=1, keepdims=True))
            p_live[h - 1] = (p.astype(BF16), jnp.sum(p, axis=1, keepdims=True))
        if 0 <= h - 2 < X_HEADS:
            p, denom = p_live.pop(h - 2)
            heads.append((_dot(p, vx_ref[:, head_cols(h - 2)]) / denom).astype(BF16))
    ox = jnp.concatenate(heads, axis=1)
    o_ref[...] = x1 + _dot(ox, wxo_ref[...])


def _mix(o_a, o_b, o_c, o_d, gates, x2d, layer, wb, wout, vec, wxq, kx, vx, wxo, seq, tm):
    t_tokens = x2d.shape[0]
    per_batch = seq // tm
    n_mem, width = kx.shape[1], kx.shape[2]

    def tile(w):
        return pl.BlockSpec((tm, w), lambda i: (i, 0))

    def per_layer(a):
        return _layer_spec(a, layer)

    mem_spec = pl.BlockSpec((None, n_mem, width), lambda i: (i // per_batch, 0, 0))
    limit = _vmem_limit(sum(_nbytes(c.shape[1:], c.dtype) for c in (wb, wout, vec, wxq, wxo)),
                        2 * tm * (4 * BRANCH_W * 2 + 4 * D_MODEL * 2 + 2 * D_MODEL * 4),
                        4 * n_mem * width * 2, 10 * tm * D_MODEL * 4)
    return pl.pallas_call(
        _mix_kernel,
        grid=(t_tokens // tm,),
        in_specs=[tile(BRANCH_W)] * 4 + [tile(N_BRANCH * D_MODEL), tile(D_MODEL),
                  per_layer(wb), per_layer(wout), per_layer(vec), per_layer(wxq),
                  mem_spec, mem_spec, per_layer(wxo)],
        out_specs=tile(D_MODEL),
        out_shape=jax.ShapeDtypeStruct((t_tokens, D_MODEL), F32),
        compiler_params=pltpu.CompilerParams(dimension_semantics=("arbitrary",),
                                             vmem_limit_bytes=limit),
        name="mix",
    )(o_a, o_b, o_c, o_d, gates, x2d, wb, wout, vec, wxq, kx, vx, wxo)


def _mlp_kernel(x_ref, vec_ref, wup_ref, wdown_ref, o_ref, *, chunk):
    x = x_ref[...]
    hn = _rms_rows(x, vec_ref[V_GMLP:V_GMLP + 1, :]).astype(BF16)
    acc = x
    for c in range(D_FF // chunk):
        u = jnp.maximum(_dot(hn, wup_ref[:, c * chunk:(c + 1) * chunk]), 0.0)
        acc = acc + _dot((u * u).astype(BF16), wdown_ref[c * chunk:(c + 1) * chunk, :])
    o_ref[...] = acc


def _mlp(x2d, layer, vec, wup, wdown, tm, chunk=512):
    t_tokens = x2d.shape[0]
    tile = pl.BlockSpec((tm, D_MODEL), lambda i: (i, 0))
    limit = _vmem_limit(_nbytes(wup.shape[1:], BF16) + _nbytes(wdown.shape[1:], BF16),
                        4 * tm * D_MODEL * 4, 8 * tm * max(chunk, D_MODEL) * 4)
    return pl.pallas_call(
        functools.partial(_mlp_kernel, chunk=chunk),
        grid=(t_tokens // tm,),
        in_specs=[tile, _layer_spec(vec, layer), _layer_spec(wup, layer),
                  _layer_spec(wdown, layer)],
        out_specs=tile,
        out_shape=jax.ShapeDtypeStruct((t_tokens, D_MODEL), F32),
        compiler_params=pltpu.CompilerParams(dimension_semantics=("arbitrary",),
                                             vmem_limit_bytes=limit),
        name="mlp",
    )(x2d, vec, wup, wdown)


def _block_diag_ones(sizes):
    m = np.zeros((SEG, SEG), np.float32)
    start = 0
    for size, on in sizes:
        if on:
            m[start:start + size, start:start + size] = 1.0
        start += size
    assert start == SEG
    return jnp.asarray(m, BF16)


def _rope_tables(seq):
    pos = jnp.arange(seq)
    inv = ROPE_THETA ** (-jnp.arange(0, B_ROPE, 2, dtype=F32) / B_ROPE)

    def cs(p):
        ang = inv[:, None] * p.astype(F32)[None, :]
        return (jnp.concatenate([jnp.cos(ang)] * 2, 0).T, jnp.concatenate([jnp.sin(ang)] * 2, 0).T)

    cos_p, sin_p = cs(pos)
    one = jnp.ones((seq, 1), F32)
    zero = jnp.zeros((seq, 1), F32)
    cosb = jnp.concatenate([one * jnp.ones((1, HALF)), cos_p, one * jnp.ones((1, 32))], 1)
    sinb = jnp.concatenate([zero * jnp.ones((1, HALF)), sin_p, zero * jnp.ones((1, 32))], 1)
    cos_r, sin_r = cs(pos // GRID_W)
    cos_c, sin_c = cs(pos % GRID_W)
    cosd = jnp.concatenate([cos_r, cos_c] * 2, 1)
    sind = jnp.concatenate([sin_r, sin_c] * 2, 1)
    return cosb, sinb, cosd, sind


def _layout_w_in(w):
    off = np.concatenate([[0], np.cumsum(IN_SIZES)])

    def seg(i):
        return w[:, off[i]:off[i + 1]]

    def zeros(n):
        return jnp.zeros((w.shape[0], n), w.dtype)

    cols = [seg(0), seg(1), seg(2), seg(3), seg(4), zeros(HALF), seg(5), zeros(32),
            seg(6), seg(7), seg(8), seg(10), seg(11), seg(9)]
    qkv = jnp.concatenate([c.astype(BF16) for c in cols], axis=1)
    assert qkv.shape[1] == N_QKV_GROUPS * GROUP_W
    return qkv, seg(12).astype(BF16)


def _pad_row(v):
    return jnp.pad(v, (0, D_MODEL - v.shape[0]))


def _vector_table(g_mix, g_qk_a, g_cq, g_ckv, g_qk_b, g_qk_c, g_qk_d, g_sub_a, g_x, g_qk_x,
                  g_mlp, g_mem):
    z32 = jnp.zeros((32,), F32)
    z64 = jnp.zeros((HALF,), F32)
    rows = [None] * N_VROWS
    rows[V_GMIX] = g_mix
    rows[V_AQ] = jnp.tile(g_qk_a[0], 8) * (A_DK ** -0.5 * LOG2E)
    rows[V_AK] = jnp.tile(g_qk_a[1], 8)
    rows[V_GCQ] = g_cq
    rows[V_GCKV] = g_ckv
    rows[V_QB] = jnp.tile(jnp.concatenate([g_qk_b[0], z32]), B_HEADS) * (
        (B_NOPE + B_ROPE) ** -0.5 * LOG2E)
    rows[V_QB_CNT] = jnp.tile(jnp.concatenate(
        [jnp.full((B_NOPE,), 1.0 / B_NOPE), jnp.full((B_ROPE,), 1.0 / B_ROPE), jnp.ones((32,))]),
        B_HEADS)
    rows[V_KB] = jnp.tile(jnp.concatenate([g_qk_b[1, :B_NOPE], z64]), B_HEADS)
    rows[V_KR] = jnp.concatenate([z64, g_qk_b[1, B_NOPE:], z32])
    rows[V_CQ] = jnp.tile(g_qk_c[0], 8) * (C_DH ** -0.5 * LOG2E)
    ones = jnp.ones((LANES,), F32)
    rows[V_G5] = jnp.concatenate([jnp.tile(g_qk_c[1], 2), ones, jnp.tile(g_qk_d[1], 2), ones])
    rows[V_DQ] = jnp.tile(g_qk_d[0], 8) * (D_DH ** -0.5 * LOG2E)
    rows[V_GSUB] = g_sub_a
    rows[V_GX] = g_x
    rows[V_GQX] = g_qk_x[0] * (X_DH ** -0.5 * LOG2E)
    rows[V_GMLP] = g_mlp
    rows[V_GMEM] = g_mem
    rows[V_GKX] = g_qk_x[1]
    return jnp.stack([_pad_row(r.astype(F32)) for r in rows])


def _layout_latent_weights(w_uq, w_ukv):
    wuq = jnp.pad(w_uq.reshape(B_Q_LORA, B_HEADS, B_NOPE + B_ROPE),
                  ((0, 0), (0, 0), (0, 32))).reshape(B_Q_LORA, B_HEADS * LANES)
    wukv = w_ukv.reshape(B_KV_LORA, B_HEADS, B_NOPE + B_DV)
    wukvk = jnp.pad(wukv[:, :, :B_NOPE], ((0, 0), (0, 0), (0, HALF))).reshape(
        B_KV_LORA, B_HEADS * LANES)
    wukvv = wukv[:, :, B_NOPE:].reshape(B_KV_LORA, B_HEADS * B_DV)
    return wuq.astype(BF16), wukvk.astype(BF16), wukvv.astype(BF16)


def _score_bounds(rel_bias, ga, gb, gc, gd, sink):
    def amax(v):
        return jnp.max(jnp.abs(v))

    bound_a = A_DK ** 0.5 * amax(ga[0]) * amax(ga[1]) + amax(rel_bias[:, :A_HEADS])
    norm_b = [jnp.sqrt(B_NOPE * amax(gb[s, :B_NOPE]) ** 2 + B_ROPE * amax(gb[s, B_NOPE:]) ** 2)
              for s in range(2)]
    bound_b = (B_NOPE + B_ROPE) ** -0.5 * norm_b[0] * norm_b[1]
    bound_c = jnp.maximum(
        C_DH ** 0.5 * amax(gc[0]) * amax(gc[1]) + amax(rel_bias[:, A_HEADS:]), amax(sink))
    bound_d = D_DH ** 0.5 * amax(gd[0]) * amax(gd[1])
    return LOG2E * jnp.stack([bound_a, bound_b, bound_c, bound_d])


def _forward(x, mem, rel_bias, g_mix, w_in, lam, g_qk_a, g_sub_a, g_cq, g_ckv, w_uq, w_ukv,
             g_qk_b, g_qk_c, sink_c, g_qk_d, w_branch, w_out, g_x, g_mem, w_xq, w_xkv,
             g_qk_x, w_xo, g_mlp, w_up, w_down, *, t_attn, tk_attn, tm, tm_wide, rows_c):
    batch, seq, _ = x.shape
    depth = w_in.shape[0]
    assert seq % t_attn == 0 and seq % tm == 0 and seq % rows_c == 0 and seq % GRID_W == 0
    assert seq % tm_wide == 0

    bd64 = _block_diag_ones([(HALF, True)] * 4)
    bdb = _block_diag_ones([(B_NOPE, True), (B_ROPE, True), (32, False)] * 2)
    cosb, sinb, cosd, sind = _rope_tables(seq)
    t_robust = min(t_attn, 512)
    bias_a_t = _bias_dense(rel_bias, t_attn, tk_attn, key_axis=0)
    bias_c = _bias_window(rel_bias)

    def cols_a(p):
        return (p, p, p, p, p, p)

    def cols_b(p):
        return (2 * p, 2 * p + 1, 2 * p, 2 * p + 1, p, p)

    def cols_b_fast(p):
        return (2 * p, 2 * p + 1, 2 * p, 2 * p + 1, 2 * p, 2 * p + 1)

    def cols_d(p):
        g = p // 2
        return (p, p, 2 * g, 2 * g + 1, g, g)

    vec = jax.vmap(_vector_table)(g_mix, g_qk_a, g_cq, g_ckv, g_qk_b, g_qk_c, g_qk_d, g_sub_a,
                                  g_x, g_qk_x, g_mlp, g_mem)
    win, wgate = jax.vmap(_layout_w_in)(w_in)
    wuq, wukvk, wukvv = jax.vmap(_layout_latent_weights)(w_uq, w_ukv)
    wb, wout, wxq, wxkv, wxo, wup, wdown = (
        w.astype(BF16) for w in (w_branch, w_out, w_xq, w_xkv, w_xo, w_up, w_down))
    bounds = jax.vmap(functools.partial(_score_bounds, rel_bias))(
        g_qk_a, g_qk_b, g_qk_c, g_qk_d, sink_c)

    x2d = x.reshape(batch * seq, D_MODEL)
    for layer in range(depth):
        lam_init = 0.8 - 0.6 * math.exp(-0.3 * layer)
        (qa, ka1, ka2, va, qb, kb, vb, qc, kc, vc, qd, kd, vd, gates, vat, vbt, vdt,
         qat, qbt, qdt) = _front(
            x2d, layer, vec, win, wgate, wuq, wukvk, wukvv, bd64, bdb, cosb, sinb, cosd, sind,
            seq, tm)
        qa, ka1, ka2, va, qb, kb, vb, qc, kc, vc, qd, kd, vd = (
            a.reshape(batch, seq, a.shape[1])
            for a in (qa, ka1, ka2, va, qb, kb, vb, qc, kc, vc, qd, kd, vd))
        bound_a, bound_b, bound_c, bound_d = (bounds[layer, n] for n in range(4))

        diff_kw = dict(lam=lam, vec=vec, layer=layer, lam_init=lam_init)
        o_a = _bounded(
            bound_a,
            lambda: _pair_attention(qat, ka1, ka2, vat, cols_a, A_HEADS, t_attn, "diff", True,
                                    tk=tk_attn, bias=bias_a_t, **diff_kw),
            lambda: _pair_attention(qa, ka1, ka2, va, cols_a, A_HEADS, t_robust, "diff", False,
                                    bias=_bias_dense(rel_bias, t_robust, t_robust, key_axis=1),
                                    **diff_kw))
        o_b = _bounded(
            bound_b,
            lambda: _pair_attention(qbt, kb, kb, vbt, cols_b_fast, B_HEADS // 2, t_attn, "merge",
                                    True, tk=tk_attn),
            lambda: _pair_attention(qb, kb, kb, vb, cols_b, B_HEADS // 2, t_robust, "merge", False))
        o_c = _bounded(
            bound_c,
            lambda: _window_attention(qc, kc, vc, bias_c, sink_c, layer, rows_c, True),
            lambda: _window_attention(qc, kc, vc, bias_c, sink_c, layer, rows_c, False))
        o_d = _bounded(
            bound_d,
            lambda: _pair_attention(qdt, kd, kd, vdt, cols_d, D_HEADS // 2, t_attn, "merge", True,
                                    tk=tk_attn),
            lambda: _pair_attention(qd, kd, kd, vd, cols_d, D_HEADS // 2, t_robust, "merge", False))

        kx, vx = _memkv(mem, layer, vec, wxkv)
        t2 = batch * seq
        x2d = _mix(o_a.reshape(t2, -1), o_b.reshape(t2, -1), o_c.reshape(t2, -1),
                   o_d.reshape(t2, -1), gates, x2d, layer, wb, wout, vec, wxq, kx, vx, wxo,
                   seq, tm_wide)
        x2d = _mlp(x2d, layer, vec, wup, wdown, tm_wide)
    return x2d.reshape(batch, seq, D_MODEL)


def kernel(x, mem, rel_bias, g_mix, w_in, lam, g_qk_a, g_sub_a, g_cq, g_ckv, w_uq, w_ukv, g_qk_b, g_qk_c, sink_c, g_qk_d, w_branch, w_out, g_x, g_mem, w_xq, w_xkv, g_qk_x, w_xo, g_mlp, w_up, w_down):
    return _forward(x, mem, rel_bias, g_mix, w_in, lam, g_qk_a, g_sub_a, g_cq, g_ckv, w_uq, w_ukv,
                    g_qk_b, g_qk_c, sink_c, g_qk_d, w_branch, w_out, g_x, g_mem, w_xq, w_xkv,
                    g_qk_x, w_xo, g_mlp, w_up, w_down, t_attn=512, tk_attn=256, tm=256, tm_wide=512,
                    rows_c=512)
```

```python
import functools
import math

import numpy as np
import jax
import jax.numpy as jnp
from jax import lax
from jax.experimental import pallas as pl
from jax.experimental.pallas import tpu as pltpu

F32 = jnp.float32
BF16 = jnp.bfloat16

D_MODEL = 1024
GRID_W = 64
N_MEM = 256
EPS = 1e-6
ROPE_THETA = 10000.0
NEG_INF = -1e30
LOG2E = 1.4426950408889634
SAFE_SCORE = 60.0

A_HEADS, A_DK, A_DV = 4, 64, 128
B_HEADS, B_Q_LORA, B_KV_LORA, B_NOPE, B_ROPE, B_DV = 8, 256, 128, 64, 32, 64
C_HEADS, C_KV_HEADS, C_DH, WINDOW = 8, 2, 64, 128
D_HEADS, D_KV_HEADS, D_DH = 8, 2, 64
X_HEADS, X_DH = 4, 128
D_FF = 4 * D_MODEL
REL_BUCKETS, REL_MAX_DIST = 32, 128
N_BRANCH, BRANCH_W = 4, 512

IN_SIZES = (512, 512, 512, B_Q_LORA, B_KV_LORA, B_ROPE, 512, 128, 128, 512, 128, 128,
            N_BRANCH * D_MODEL)
GROUP_W = 512
N_QKV_GROUPS = 7
N_GATE_GROUPS = N_BRANCH * D_MODEL // GROUP_W

LANES = 128
HALF = 64
SEG = 256
VMEM_CAP = 60 * 1024 * 1024

(V_GMIX, V_AQ, V_AK, V_GCQ, V_GCKV, V_QB, V_QB_CNT, V_KB, V_KR, V_CQ, V_G5, V_DQ,
 V_GSUB, V_GX, V_GQX, V_GMLP, V_GMEM, V_GKX, N_VROWS) = range(19)


def _vmem_limit(*nbytes):
    return int(min(VMEM_CAP, sum(nbytes) + (6 << 20)))


def _const_spec(shape):
    zeros = (0,) * len(shape)
    return pl.BlockSpec(shape, lambda *_: zeros, pipeline_mode=pl.Buffered(1))


def _layer_spec(stacked, layer):
    index = (layer,) + (0,) * (stacked.ndim - 1)
    return pl.BlockSpec((None,) + stacked.shape[1:], lambda *_: index,
                        pipeline_mode=pl.Buffered(1))


def _nbytes(shape, dtype):
    return int(np.prod(shape)) * jnp.dtype(dtype).itemsize


def _dot(a, b):
    return jnp.dot(a, b, preferred_element_type=F32)


def _dot_nt(a, b):
    return lax.dot_general(a, b, (((1,), (1,)), ((), ())), preferred_element_type=F32)


def _rms_rows(x, g):
    return x * lax.rsqrt(jnp.mean(x * x, axis=-1, keepdims=True) + EPS) * g


def _seg_ssq(y, bd):
    y2 = (y * y).astype(BF16)
    parts = [_dot(y2[:, c * SEG:(c + 1) * SEG], bd) for c in range(y.shape[1] // SEG)]
    return parts[0] if len(parts) == 1 else jnp.concatenate(parts, axis=1)


def _seg_scale(y, ssq, inv_cnt, gain):
    return y * lax.rsqrt(ssq * inv_cnt + EPS) * gain


def _rope32(x, cos, sin):
    w = x.shape[1]
    lane = lax.broadcasted_iota(jnp.int32, x.shape, 1)
    first = (lane & 16) == 0
    rot = jnp.where(first, -pltpu.roll(x, w - 16, 1), pltpu.roll(x, 16, 1))
    return x * cos + rot * sin


def _tile_lanes(v, n):
    return v if n == 1 else jnp.concatenate([v] * n, axis=1)


def _pair_split(x, fill):
    lane = lax.broadcasted_iota(jnp.int32, x.shape, 1)
    low = lane < HALF
    sw = pltpu.roll(x, HALF, 1)
    return (jnp.where(low, x, fill), jnp.where(low, fill, sw), jnp.where(low, sw, fill),
            jnp.where(low, fill, x))


def _front_kernel(x_ref, vec_ref, win_ref, wgate_ref, wuq_ref, wukvk_ref, wukvv_ref, bd64_ref,
                  bdb_ref,
                  cosb_ref, sinb_ref, cosd_ref, sind_ref,
                  qa_ref, ka1_ref, ka2_ref, va_ref, qb_ref, kb_ref, vb_ref,
                  qc_ref, kc_ref, vc_ref, qd_ref, kd_ref, vd_ref, gate_ref,
                  vat_ref, vbt_ref, vdt_ref, qat_ref, qbt_ref, qdt_ref):
    def row(r, w):
        return vec_ref[r:r + 1, :w]

    inv64 = 1.0 / HALF
    bd64 = bd64_ref[...]
    h = _rms_rows(x_ref[...], row(V_GMIX, D_MODEL)).astype(BF16)

    def group(g):
        w_ref, g = (win_ref, g) if g < N_QKV_GROUPS else (wgate_ref, g - N_QKV_GROUPS)
        return _dot(h, w_ref[:, g * GROUP_W:(g + 1) * GROUP_W])

    lane512 = lax.broadcasted_iota(jnp.int32, (x_ref.shape[0], GROUP_W), 1)

    def gate(c, z):
        gate_ref[:, c * GROUP_W:(c + 1) * GROUP_W] = (1.0 / (1.0 + jnp.exp(-z))).astype(BF16)

    def gate_group(c):
        return group(N_QKV_GROUPS + c)

    def finish_a_q(y, ssq):
        qa = _seg_scale(y, ssq, inv64, row(V_AQ, GROUP_W))
        qa_ref[...] = qa.astype(BF16)
        qat_ref[...] = qa.T.astype(BF16)

    def finish_a_kv(yk, ssq, yv):
        ka = _seg_scale(yk, ssq, inv64, row(V_AK, GROUP_W))
        first_map = (lane512 & HALF) == 0
        ka1_ref[...] = jnp.where(first_map, ka, 0.0).astype(BF16)
        ka2_ref[...] = jnp.where(first_map, 0.0, ka).astype(BF16)
        va_ref[...] = yv.astype(BF16)
        vat_ref[...] = yv.T.astype(BF16)

    def finish_b(tq, ssq_tq, tk, ssq_tk, kr, vb):
        cosb = cosb_ref[...]
        sinb = sinb_ref[...]
        qb = _seg_scale(tq, ssq_tq, row(V_QB_CNT, D_MODEL), row(V_QB, D_MODEL))
        qb = _rope32(qb, _tile_lanes(cosb, B_HEADS), _tile_lanes(sinb, B_HEADS))
        qb_ref[...] = qb.astype(BF16)
        qbt_ref[...] = qb.T.astype(BF16)
        k_nope = _seg_scale(tk, ssq_tk, inv64, row(V_KB, D_MODEL))
        kr = kr * lax.rsqrt(jnp.sum(kr * kr, axis=-1, keepdims=True) * (1.0 / B_ROPE) + EPS)
        kr = _rope32(kr * row(V_KR, LANES), cosb, sinb)
        kb_ref[...] = (k_nope + _tile_lanes(kr, B_HEADS)).astype(BF16)
        vb_ref[...] = vb.astype(BF16)
        vbt_ref[...] = vb.T.astype(BF16)

    def finish_cd_kv(y, ssq):
        yn = _seg_scale(y, ssq, inv64, row(V_G5, GROUP_W))
        kc = yn[:, 0:LANES]
        vc = y[:, LANES:2 * LANES]
        kd = _rope32(yn[:, 2 * LANES:3 * LANES], cosd_ref[...], sind_ref[...])
        vd = y[:, 3 * LANES:]
        kc_ref[...] = jnp.concatenate(_pair_split(kc, 0.0), axis=1).astype(BF16)
        kd_ref[...] = jnp.concatenate(_pair_split(kd, 0.0), axis=1).astype(BF16)
        vc_ref[...] = jnp.concatenate(_pair_split(vc, 1.0), axis=1).astype(BF16)
        low = lax.broadcasted_iota(jnp.int32, vd.shape, 1) < HALF
        vd_sw = pltpu.roll(vd, HALF, 1)
        vd_ref[...] = jnp.concatenate([jnp.where(low, vd, vd_sw), jnp.where(low, vd_sw, vd)],
                                      axis=1).astype(BF16)
        vdt_ref[...] = vd.T.astype(BF16)

    def finish_c_q(y, ssq):
        qc_ref[...] = _seg_scale(y, ssq, inv64, row(V_CQ, GROUP_W)).astype(BF16)

    def finish_d_q(y, ssq):
        qd = _seg_scale(y, ssq, inv64, row(V_DQ, GROUP_W))
        qd = _rope32(qd, _tile_lanes(cosd_ref[...], 4), _tile_lanes(sind_ref[...], 4))
        qd_ref[...] = qd.astype(BF16)
        qdt_ref[...] = qd.T.astype(BF16)

    y0 = group(0)
    y1 = group(1)
    ssq0 = _seg_ssq(y0, bd64)
    y2 = group(2)
    finish_a_q(y0, ssq0)
    ssq1 = _seg_ssq(y1, bd64)
    y3 = group(3)
    finish_a_kv(y1, ssq1, y2)
    cq = _rms_rows(y3[:, :B_Q_LORA], row(V_GCQ, B_Q_LORA)).astype(BF16)
    ckv = _rms_rows(y3[:, B_Q_LORA:B_Q_LORA + B_KV_LORA], row(V_GCKV, B_KV_LORA)).astype(BF16)
    z = gate_group(0)
    tq = _dot(cq, wuq_ref[...])
    tk = _dot(ckv, wukvk_ref[...])
    vb = _dot(ckv, wukvv_ref[...])
    gate(0, z)
    y4 = group(4)
    ssq_tq = _seg_ssq(tq, bdb_ref[...])
    ssq_tk = _seg_ssq(tk, bd64)
    gate(1, gate_group(1))
    z = gate_group(2)
    finish_b(tq, ssq_tq, tk, ssq_tk, y3[:, B_Q_LORA + B_KV_LORA:], vb)
    gate(2, z)
    y5 = group(5)
    ssq4 = _seg_ssq(y4, bd64)
    gate(3, gate_group(3))
    z = gate_group(4)
    finish_c_q(y4, ssq4)
    gate(4, z)
    y6 = group(6)
    ssq5 = _seg_ssq(y5, bd64)
    gate(5, gate_group(5))
    z = gate_group(6)
    finish_cd_kv(y5, ssq5)
    gate(6, z)
    ssq6 = _seg_ssq(y6, bd64)
    z = gate_group(7)
    finish_d_q(y6, ssq6)
    gate(7, z)
    assert N_GATE_GROUPS == 8


def _front(x2d, layer, vec, win, wgate, wuq, wukvk, wukvv, bd64, bdb, cosb, sinb, cosd, sind,
           seq, tm):
    t_tokens = x2d.shape[0]
    n_pos = seq // tm
    widths = (512, 512, 512, 512, 1024, 1024, 512, 512, 512, 512, 512, 512, 256, 4096)
    t_rows = (A_HEADS * A_DV, B_HEADS * B_DV, D_KV_HEADS * D_DH, A_HEADS * LANES,
              B_HEADS * LANES, D_HEADS * D_DH)

    def tile(w):
        return pl.BlockSpec((tm, w), lambda i: (i, 0))

    def pos(w):
        return pl.BlockSpec((tm, w), lambda i: (i % n_pos, 0))

    def tile_t(r):
        return pl.BlockSpec((None, r, tm), lambda i: (i // n_pos, 0, i % n_pos))

    per_layer = (vec, win, wgate, wuq, wukvk, wukvv)
    consts = per_layer + (bd64, bdb)
    in_specs = ([tile(D_MODEL)] + [_layer_spec(c, layer) for c in per_layer]
                + [_const_spec(bd64.shape), _const_spec(bdb.shape)] + [pos(LANES)] * 4)
    limit = _vmem_limit(sum(_nbytes(c.shape[-2:], c.dtype) for c in consts),
                        2 * tm * (D_MODEL * 4 + 4 * LANES * 4 + (sum(widths) + sum(t_rows)) * 2),
                        12 * tm * D_MODEL * 4)
    return pl.pallas_call(
        _front_kernel,
        grid=(t_tokens // tm,),
        in_specs=in_specs,
        out_specs=[tile(w) for w in widths] + [tile_t(r) for r in t_rows],
        out_shape=([jax.ShapeDtypeStruct((t_tokens, w), BF16) for w in widths]
                   + [jax.ShapeDtypeStruct((t_tokens // seq, r, seq), BF16) for r in t_rows]),
        compiler_params=pltpu.CompilerParams(dimension_semantics=("arbitrary",),
                                             vmem_limit_bytes=limit),
        name="front",
    )(x2d, *consts, cosb, sinb, cosd, sind)


def _t5_bucket(rel):
    nb = REL_BUCKETS // 2
    max_exact = nb // 2
    n = jnp.abs(rel)
    nf = jnp.maximum(n, 1).astype(F32)
    large = max_exact + (jnp.log(nf / max_exact) / math.log(REL_MAX_DIST / max_exact)
                         * (nb - max_exact)).astype(jnp.int32)
    large = jnp.minimum(large, nb - 1)
    return jnp.where(rel > 0, nb, 0) + jnp.where(n < max_exact, n, large)


def _table_lookup(bucket, tbl_ref, col):
    acc = jnp.zeros(bucket.shape, F32)
    for b in range(REL_BUCKETS):
        acc = jnp.where(bucket == b, tbl_ref[b, col], acc)
    return acc


def _bias_dense_kernel(tbl_ref, o_ref, *, tk, key_axis):
    h = pl.program_id(0)
    d = pl.program_id(1) - 2
    last = pl.num_programs(1) - 3
    half = REL_BUCKETS // 2

    @pl.when(d == -2)
    def _():
        o_ref[...] = jnp.full(o_ref.shape, tbl_ref[half - 1, h] * LOG2E, F32)

    @pl.when(d == last)
    def _():
        o_ref[...] = jnp.full(o_ref.shape, tbl_ref[REL_BUCKETS - 1, h] * LOG2E, F32)

    @pl.when((d > -2) & (d < last))
    def _():
        rel = (d * tk + lax.broadcasted_iota(jnp.int32, o_ref.shape, key_axis)
               - lax.broadcasted_iota(jnp.int32, o_ref.shape, 1 - key_axis))
        o_ref[...] = _table_lookup(_t5_bucket(rel), tbl_ref, h) * LOG2E


def _bias_window_kernel(tbl_ref, o_ref):
    h = pl.program_id(0)
    shape = (LANES, 3 * LANES)
    rel = (lax.broadcasted_iota(jnp.int32, shape, 1) - LANES
           - lax.broadcasted_iota(jnp.int32, shape, 0))
    bias = _table_lookup(_t5_bucket(rel), tbl_ref, A_HEADS + h) * LOG2E
    o_ref[...] = jnp.where(jnp.abs(rel) <= WINDOW, bias, NEG_INF)


def _bias_dense(rel_bias, tq, tk, key_axis):
    assert tk >= LANES and tq % tk == 0
    n_tiles = tq // tk + 4
    shape = (tk, tq) if key_axis == 0 else (tq, tk)
    return pl.pallas_call(
        functools.partial(_bias_dense_kernel, tk=tk, key_axis=key_axis),
        grid=(A_HEADS, n_tiles),
        in_specs=[pl.BlockSpec(memory_space=pltpu.SMEM)],
        out_specs=pl.BlockSpec((None, None) + shape, lambda h, d: (h, d, 0, 0)),
        out_shape=jax.ShapeDtypeStruct((A_HEADS, n_tiles) + shape, F32),
        name="bias_dense",
    )(rel_bias)


def _bias_window(rel_bias):
    return pl.pallas_call(
        _bias_window_kernel,
        grid=(C_HEADS,),
        in_specs=[pl.BlockSpec(memory_space=pltpu.SMEM)],
        out_specs=pl.BlockSpec((None, LANES, 3 * LANES), lambda h: (h, 0, 0)),
        out_shape=jax.ShapeDtypeStruct((C_HEADS, LANES, 3 * LANES), F32),
        name="bias_window",
    )(rel_bias)


def _pair_attn_kernel(*refs, t, n_chunks, mode, lam_init):
    if mode == "diff":
        (qa_ref, qb_ref, ka_ref, kb_ref, va_ref, vb_ref, bias_ref, lam_ref, gsub_ref, o_ref,
         ma, la, acca, mb, lb, accb) = refs
    else:
        (qa_ref, qb_ref, ka_ref, kb_ref, va_ref, vb_ref, o_ref,
         ma, la, acca, mb, lb, accb) = refs
        bias_ref = None
    i = pl.program_id(2)
    streams = ((qa_ref[...], ka_ref, va_ref, ma, la, acca),
               (qb_ref[...], kb_ref, vb_ref, mb, lb, accb))
    for _, _, _, m, l, acc in streams:
        m[...] = jnp.full(m.shape, NEG_INF, F32)
        l[...] = jnp.zeros(l.shape, F32)
        acc[...] = jnp.zeros(acc.shape, F32)

    def body(c, carry):
        off = pl.multiple_of(c * t, t)
        bias = None if bias_ref is None else bias_ref[jnp.clip(c - i, -2, 2) + 2]
        for q, k_ref, v_ref, m, l, acc in streams:
            s = _dot_nt(q, k_ref[pl.ds(off, t), :])
            if bias is not None:
                s = s + bias
            m_prev = m[...]
            m_new = jnp.maximum(m_prev, jnp.max(s, axis=1, keepdims=True))
            alpha = jnp.exp2(m_prev - m_new)
            p = jnp.exp2(s - m_new)
            l[...] = alpha * l[...] + jnp.sum(p, axis=1, keepdims=True)
            acc[...] = alpha * acc[...] + _dot(p.astype(BF16), v_ref[pl.ds(off, t), :])
            m[...] = m_new
        return carry

    lax.fori_loop(0, n_chunks, body, 0)
    _finish_pair(acca[...] / la[...], accb[...] / lb[...], mode, refs, lam_init)


def _write_diff(oa, ob, lam_ref, vec_ref, o_ref, lam_init):
    lam = lam_ref[...]
    lam_full = (jnp.exp(jnp.sum(lam[0:1] * lam[1:2], keepdims=True))
                - jnp.exp(jnp.sum(lam[2:3] * lam[3:4], keepdims=True)) + lam_init)
    o = oa - lam_full * ob
    gsub = vec_ref[V_GSUB:V_GSUB + 1, :A_DV]
    o_ref[...] = (_rms_rows(o, gsub) * (1.0 - lam_init)).astype(BF16)


def _finish_pair(oa, ob, mode, refs, lam_init):
    if mode == "diff":
        _write_diff(oa, ob, *refs[7:10], lam_init)
    else:
        o_ref = refs[6]
        lane = lax.broadcasted_iota(jnp.int32, oa.shape, 1)
        o_ref[...] = jnp.where(lane < HALF, oa, ob).astype(BF16)


def _pair_attn_fast_kernel(*refs, tq, tk, n_sub, n_chunks, mode, lam_init):
    qta_ref, qtb_ref, ka_ref, kb_ref, vta_ref, vtb_ref = refs[:6]
    bias_ref = refs[6] if mode == "diff" else None
    o_ref = refs[9] if mode == "diff" else refs[6]
    i = pl.program_id(2)
    ratio = tq // tk
    streams = ((qta_ref, ka_ref, vta_ref), (qtb_ref, kb_ref, vtb_ref))

    tasks = [(u, c, n_s) for u in range(n_sub) for c in range(n_chunks) for n_s in range(2)]

    def scores(u, c, n_s):
        qt_ref, k_ref, _ = streams[n_s]
        s = _dot(k_ref[c * tk:(c + 1) * tk, :], qt_ref[:, u * tq:(u + 1) * tq])
        if bias_ref is not None:
            s = s + bias_ref[jnp.clip(c - (i * n_sub + u) * ratio, -2, ratio + 1) + 2]
        return s

    def finish(u):
        oa = sums.pop((u, 0)) / jnp.sum(denoms.pop((u, 0)), axis=0, keepdims=True)
        ob = sums.pop((u, 1)) / jnp.sum(denoms.pop((u, 1)), axis=0, keepdims=True)
        out = o_ref.at[u * tq:(u + 1) * tq, :]
        if mode == "diff":
            _write_diff(oa.T, ob.T, refs[7], refs[8], out, lam_init)
        else:
            out[...] = jnp.concatenate([oa, ob], axis=0).T.astype(BF16)

    sums = {}
    denoms = {}
    s_live, p_live = {}, {}
    for j in range(len(tasks) + 2):
        if j < len(tasks):
            s_live[j] = scores(*tasks[j])
        if 0 <= j - 1 < len(tasks):
            u, _, n_s = tasks[j - 1]
            p = jnp.exp2(s_live.pop(j - 1))
            part = jnp.sum(p.reshape(tk // 8, 8, tq), axis=0)
            denoms[u, n_s] = part if (u, n_s) not in denoms else denoms[u, n_s] + part
            p_live[j - 1] = p.astype(BF16)
        if 0 <= j - 2 < len(tasks):
            u, c, n_s = tasks[j - 2]
            part = _dot(streams[n_s][2][:, c * tk:(c + 1) * tk], p_live.pop(j - 2))
            sums[u, n_s] = part if (u, n_s) not in sums else sums[u, n_s] + part
            if c == n_chunks - 1 and n_s == 1:
                finish(u)


def _pair_attention(q, k_a, k_b, v, cols, n_pairs, t, mode, fast, tk=None, n_sub=1, bias=None,
                    lam=None, vec=None, layer=None, lam_init=0.0):
    batch, seq = k_a.shape[:2]

    def kv_spec(which):
        return pl.BlockSpec((None, seq, LANES), lambda b, p, i: (b, 0, cols(p)[which]))

    if fast:
        v_rows = A_DV if mode == "diff" else HALF
        t_sub, t = t, t * n_sub

        def q_spec(which):
            return pl.BlockSpec((None, LANES, t), lambda b, p, i: (b, cols(p)[which], i))

        def v_spec(which):
            return pl.BlockSpec((None, v_rows, seq), lambda b, p, i: (b, cols(p)[which], 0))

        body = functools.partial(_pair_attn_fast_kernel, tq=t_sub, tk=tk, n_sub=n_sub,
                                 n_chunks=seq // tk)
        scratch = []
        v_bytes = v_rows * seq * 2
    else:
        tk = t

        def q_spec(which):
            return pl.BlockSpec((None, t, LANES), lambda b, p, i: (b, i, cols(p)[which]))

        v_spec = kv_spec
        body = functools.partial(_pair_attn_kernel, t=t, n_chunks=seq // t)
        scratch = [pltpu.VMEM((t, 1), F32), pltpu.VMEM((t, 1), F32),
                   pltpu.VMEM((t, LANES), F32)] * 2
        v_bytes = seq * LANES * 2
    in_specs = [q_spec(0), q_spec(1), kv_spec(2), kv_spec(3), v_spec(4), v_spec(5)]
    args = [q, q, k_a, k_b, v, v]
    resident = 2 * 2 * (seq * LANES * 2 + v_bytes)
    if mode == "diff":
        in_specs += [pl.BlockSpec((None,) + bias.shape[1:], lambda b, p, i: (p, 0, 0, 0)),
                     _layer_spec(lam, layer), _layer_spec(vec, layer)]
        args += [bias, lam, vec]
        resident += 2 * _nbytes(bias.shape[1:], F32)
    limit = _vmem_limit(resident, 6 * t * LANES * 2, 8 * t * LANES * 4, 8 * t * tk * 4)
    return pl.pallas_call(
        functools.partial(body, mode=mode, lam_init=lam_init),
        grid=(batch, n_pairs, seq // t),
        in_specs=in_specs,
        out_specs=pl.BlockSpec((None, t, LANES), lambda b, p, i: (b, i, p)),
        out_shape=jax.ShapeDtypeStruct((batch, seq, n_pairs * LANES), BF16),
        scratch_shapes=scratch,
        compiler_params=pltpu.CompilerParams(
            dimension_semantics=("arbitrary", "arbitrary", "arbitrary"),
            vmem_limit_bytes=limit),
        name=("attn_fast_" if fast else "attn_") + mode,
    )(*args)


def _bounded(score_bound, fast, robust):
    return lax.cond(score_bound <= SAFE_SCORE, fast, robust)


def _window_kernel(sink_ref, q_ref, kp_ref, km_ref, kn_ref, vp_ref, vm_ref, vn_ref, bias_ref,
                   o_ref, *, rows, fast, layer):
    i = pl.program_id(1)
    n_steps = pl.num_programs(1)
    nblk = rows // LANES
    lane = lax.broadcasted_iota(jnp.int32, (LANES, LANES), 1)
    zero_row = jnp.zeros((1, LANES), F32)

    def kv_block(j):
        if j < 0:
            return kp_ref[...], vp_ref[...]
        if j >= nblk:
            return kn_ref[...], vn_ref[...]
        return km_ref[j * LANES:(j + 1) * LANES, :], vm_ref[j * LANES:(j + 1) * LANES, :]

    kcat, vcat, edge = [], [], []
    for r in range(nblk):
        blocks = [kv_block(r + d) for d in (-1, 0, 1)]
        kcat.append(jnp.concatenate([blk[0] for blk in blocks], axis=0))
        vcat.append(jnp.concatenate([blk[1] for blk in blocks], axis=0))
        if r == 0 or r == nblk - 1:
            before = zero_row + jnp.where(i > 0, 0.0, NEG_INF) if r == 0 else zero_row
            after = zero_row + jnp.where(i < n_steps - 1, 0.0, NEG_INF) if r == nblk - 1 else zero_row
            edge.append(jnp.concatenate([before, zero_row, after], axis=1))
        else:
            edge.append(None)

    def col(p, e):
        g = p // (C_HEADS // C_KV_HEADS // 2)
        return slice((2 * g + e) * LANES, (2 * g + e + 1) * LANES)

    def scores(r, p, e):
        q = q_ref[r * LANES:(r + 1) * LANES, p * LANES:(p + 1) * LANES]
        s = _dot_nt(q, kcat[r][:, col(p, e)]) + bias_ref[2 * p + e]
        return s if edge[r] is None else s + edge[r]

    def store(r, p, outs):
        o_ref[r * LANES:(r + 1) * LANES, p * LANES:(p + 1) * LANES] = (
            jnp.where(lane < HALF, outs[0], outs[1]).astype(BF16))

    tasks = [(r, p, e) for r in range(nblk) for p in range(C_HEADS // 2) for e in range(2)]
    if fast:
        s_live, p_live, outs = {}, {}, []
        for j in range(len(tasks) + 2):
            if j < len(tasks):
                s_live[j] = scores(*tasks[j])
            if 0 <= j - 1 < len(tasks):
                p_live[j - 1] = jnp.exp2(s_live.pop(j - 1)).astype(BF16)
            if 0 <= j - 2 < len(tasks):
                r, p, e = tasks[j - 2]
                o = _dot(p_live.pop(j - 2), vcat[r][:, col(p, e)])
                sink = sink_ref[layer, 2 * p + e] * LOG2E
                outs.append(o / (pltpu.roll(o, HALF, 1) + jnp.exp2(zero_row + sink)))
                if e == 1:
                    store(r, p, outs)
                    outs = []
    else:
        outs = []
        for r, p, e in tasks:
            s = scores(r, p, e)
            sink = sink_ref[layer, 2 * p + e] * LOG2E
            m = jnp.maximum(jnp.max(s, axis=1, keepdims=True), sink)
            pr = jnp.exp2(s - m)
            denom = jnp.sum(pr, axis=1, keepdims=True) + jnp.exp2(sink - m)
            outs.append(_dot(pr.astype(BF16), vcat[r][:, col(p, e)]) / denom)
            if e == 1:
                store(r, p, outs)
                outs = []


def _window_attention(q, k, v, bias, sink, layer, rows, fast):
    batch, seq, _ = q.shape
    nblk = rows // LANES
    last = seq // LANES - 1
    kw, vw = k.shape[2], v.shape[2]

    def main(w):
        return pl.BlockSpec((None, rows, w), lambda b, i: (b, i, 0))

    def prev(w):
        return pl.BlockSpec((None, LANES, w), lambda b, i: (b, jnp.maximum(i * nblk - 1, 0), 0))

    def nxt(w):
        return pl.BlockSpec((None, LANES, w), lambda b, i: (b, jnp.minimum((i + 1) * nblk, last), 0))

    limit = _vmem_limit(_nbytes(bias.shape, F32) * 2,
                        2 * (rows + 2 * LANES) * (kw + vw) * 2 + 4 * rows * 512 * 2,
                        16 * LANES * 3 * LANES * 4)
    return pl.pallas_call(
        functools.partial(_window_kernel, rows=rows, fast=fast, layer=layer),
        grid=(batch, seq // rows),
        in_specs=[pl.BlockSpec(memory_space=pltpu.SMEM), main(q.shape[2]),
                  prev(kw), main(kw), nxt(kw), prev(vw), main(vw), nxt(vw),
                  _const_spec(bias.shape)],
        out_specs=main(q.shape[2]),
        out_shape=jax.ShapeDtypeStruct(q.shape, BF16),
        compiler_params=pltpu.CompilerParams(dimension_semantics=("arbitrary", "arbitrary"),
                                             vmem_limit_bytes=limit),
        name="attn_window_fast" if fast else "attn_window",
    )(sink, q, k, k, k, v, v, v, bias)


def _memkv_kernel(mem_ref, vec_ref, w_ref, k_ref, v_ref):
    hn = _rms_rows(mem_ref[...], vec_ref[V_GMEM:V_GMEM + 1, :]).astype(BF16)
    kv = _dot(hn, w_ref[...])
    width = X_HEADS * X_DH
    gk = vec_ref[V_GKX:V_GKX + 1, :X_DH]
    for h in range(X_HEADS):
        sl = slice(h * X_DH, (h + 1) * X_DH)
        k_ref[:, sl] = _rms_rows(kv[:, sl], gk).astype(BF16)
    v_ref[...] = kv[:, width:].astype(BF16)


def _memkv(mem, layer, vec, w):
    batch, n_mem, _ = mem.shape
    width = X_HEADS * X_DH
    out = pl.BlockSpec((None, n_mem, width), lambda b: (b, 0, 0))
    return pl.pallas_call(
        _memkv_kernel,
        grid=(batch,),
        in_specs=[pl.BlockSpec((None, n_mem, D_MODEL), lambda b: (b, 0, 0)),
                  _layer_spec(vec, layer), _layer_spec(w, layer)],
        out_specs=[out, out],
        out_shape=[jax.ShapeDtypeStruct((batch, n_mem, width), BF16)] * 2,
        compiler_params=pltpu.CompilerParams(
            dimension_semantics=("arbitrary",),
            vmem_limit_bytes=_vmem_limit(_nbytes(w.shape[1:], BF16), 4 * n_mem * D_MODEL * 4,
                                         4 * n_mem * 2 * width * 4)),
        name="memkv",
    )(mem, vec, w)


def _mix_kernel(oa_ref, ob_ref, oc_ref, od_ref, gate_ref, x_ref, wb_ref, wout_ref, vec_ref,
                wxq_ref, kx_ref, vx_ref, wxo_ref, o_ref):
    mix = None
    for m, o_m in enumerate((oa_ref, ob_ref, oc_ref, od_ref)):
        br = _dot(o_m[...], wb_ref[m])
        term = gate_ref[:, m * D_MODEL:(m + 1) * D_MODEL].astype(F32) * br
        mix = term if mix is None else mix + term
    x1 = x_ref[...] + _dot(mix.astype(BF16), wout_ref[...])
    hn = _rms_rows(x1, vec_ref[V_GX:V_GX + 1, :]).astype(BF16)
    qx = _dot(hn, wxq_ref[...])
    gq = vec_ref[V_GQX:V_GQX + 1, :X_DH]

    def head_cols(h):
        return slice(h * X_DH, (h + 1) * X_DH)

    s_live, p_live, heads = {}, {}, []
    for h in range(X_HEADS + 2):
        if h < X_HEADS:
            qh = _rms_rows(qx[:, head_cols(h)], gq).astype(BF16)
            s_live[h] = _dot_nt(qh, kx_ref[:, head_cols(h)])
        if 0 <= h - 1 < X_HEADS:
            s = s_live.pop(h - 1)
            p = jnp.exp2(s - jnp.max(s, axis=1, keepdims=True))
            p_live[h - 1] = (p.astype(BF16), jnp.sum(p, axis=1, keepdims=True))
        if 0 <= h - 2 < X_HEADS:
            p, denom = p_live.pop(h - 2)
            heads.append((_dot(p, vx_ref[:, head_cols(h - 2)]) / denom).astype(BF16))
    ox = jnp.concatenate(heads, axis=1)
    o_ref[...] = x1 + _dot(ox, wxo_ref[...])


def _mix(o_a, o_b, o_c, o_d, gates, x2d, layer, wb, wout, vec, wxq, kx, vx, wxo, seq, tm):
    t_tokens = x2d.shape[0]
    per_batch = seq // tm
    n_mem, width = kx.shape[1], kx.shape[2]

    def tile(w):
        return pl.BlockSpec((tm, w), lambda i: (i, 0))

    def per_layer(a):
        return _layer_spec(a, layer)

    mem_spec = pl.BlockSpec((None, n_mem, width), lambda i: (i // per_batch, 0, 0))
    limit = _vmem_limit(sum(_nbytes(c.shape[1:], c.dtype) for c in (wb, wout, vec, wxq, wxo)),
                        2 * tm * (4 * BRANCH_W * 2 + 4 * D_MODEL * 2 + 2 * D_MODEL * 4),
                        4 * n_mem * width * 2, 10 * tm * D_MODEL * 4)
    return pl.pallas_call(
        _mix_kernel,
        grid=(t_tokens // tm,),
        in_specs=[tile(BRANCH_W)] * 4 + [tile(N_BRANCH * D_MODEL), tile(D_MODEL),
                  per_layer(wb), per_layer(wout), per_layer(vec), per_layer(wxq),
                  mem_spec, mem_spec, per_layer(wxo)],
        out_specs=tile(D_MODEL),
        out_shape=jax.ShapeDtypeStruct((t_tokens, D_MODEL), F32),
        compiler_params=pltpu.CompilerParams(dimension_semantics=("arbitrary",),
                                             vmem_limit_bytes=limit),
        name="mix",
    )(o_a, o_b, o_c, o_d, gates, x2d, wb, wout, vec, wxq, kx, vx, wxo)


def _mlp_kernel(x_ref, vec_ref, wup_ref, wdown_ref, o_ref, *, chunk):
    x = x_ref[...]
    hn = _rms_rows(x, vec_ref[V_GMLP:V_GMLP + 1, :]).astype(BF16)
    acc = x
    for c in range(D_FF // chunk):
        u = jnp.maximum(_dot(hn, wup_ref[:, c * chunk:(c + 1) * chunk]), 0.0)
        acc = acc + _dot((u * u).astype(BF16), wdown_ref[c * chunk:(c + 1) * chunk, :])
    o_ref[...] = acc


def _mlp(x2d, layer, vec, wup, wdown, tm, chunk=512):
    t_tokens = x2d.shape[0]
    tile = pl.BlockSpec((tm, D_MODEL), lambda i: (i, 0))
    limit = _vmem_limit(_nbytes(wup.shape[1:], BF16) + _nbytes(wdown.shape[1:], BF16),
                        4 * tm * D_MODEL * 4, 8 * tm * max(chunk, D_MODEL) * 4)
    return pl.pallas_call(
        functools.partial(_mlp_kernel, chunk=chunk),
        grid=(t_tokens // tm,),
        in_specs=[tile, _layer_spec(vec, layer), _layer_spec(wup, layer),
                  _layer_spec(wdown, layer)],
        out_specs=tile,
        out_shape=jax.ShapeDtypeStruct((t_tokens, D_MODEL), F32),
        compiler_params=pltpu.CompilerParams(dimension_semantics=("arbitrary",),
                                             vmem_limit_bytes=limit),
        name="mlp",
    )(x2d, vec, wup, wdown)


def _block_diag_ones(sizes):
    m = np.zeros((SEG, SEG), np.float32)
    start = 0
    for size, on in sizes:
        if on:
            m[start:start + size, start:start + size] = 1.0
        start += size
    assert start == SEG
    return jnp.asarray(m, BF16)


def _rope_tables(seq):
    pos = jnp.arange(seq)
    inv = ROPE_THETA ** (-jnp.arange(0, B_ROPE, 2, dtype=F32) / B_ROPE)

    def cs(p):
        ang = inv[:, None] * p.astype(F32)[None, :]
        return (jnp.concatenate([jnp.cos(ang)] * 2, 0).T, jnp.concatenate([jnp.sin(ang)] * 2, 0).T)

    cos_p, sin_p = cs(pos)
    one = jnp.ones((seq, 1), F32)
    zero = jnp.zeros((seq, 1), F32)
    cosb = jnp.concatenate([one * jnp.ones((1, HALF)), cos_p, one * jnp.ones((1, 32))], 1)
    sinb = jnp.concatenate([zero * jnp.ones((1, HALF)), sin_p, zero * jnp.ones((1, 32))], 1)
    cos_r, sin_r = cs(pos // GRID_W)
    cos_c, sin_c = cs(pos % GRID_W)
    cosd = jnp.concatenate([cos_r, cos_c] * 2, 1)
    sind = jnp.concatenate([sin_r, sin_c] * 2, 1)
    return cosb, sinb, cosd, sind


def _layout_w_in(w):
    off = np.concatenate([[0], np.cumsum(IN_SIZES)])

    def seg(i):
        return w[:, off[i]:off[i + 1]]

    def zeros(n):
        return jnp.zeros((w.shape[0], n), w.dtype)

    cols = [seg(0), seg(1), seg(2), seg(3), seg(4), zeros(HALF), seg(5), zeros(32),
            seg(6), seg(7), seg(8), seg(10), seg(11), seg(9)]
    qkv = jnp.concatenate([c.astype(BF16) for c in cols], axis=1)
    assert qkv.shape[1] == N_QKV_GROUPS * GROUP_W
    return qkv, seg(12).astype(BF16)


def _pad_row(v):
    return jnp.pad(v, (0, D_MODEL - v.shape[0]))


def _vector_table(g_mix, g_qk_a, g_cq, g_ckv, g_qk_b, g_qk_c, g_qk_d, g_sub_a, g_x, g_qk_x,
                  g_mlp, g_mem):
    z32 = jnp.zeros((32,), F32)
    z64 = jnp.zeros((HALF,), F32)
    rows = [None] * N_VROWS
    rows[V_GMIX] = g_mix
    rows[V_AQ] = jnp.tile(g_qk_a[0], 8) * (A_DK ** -0.5 * LOG2E)
    rows[V_AK] = jnp.tile(g_qk_a[1], 8)
    rows[V_GCQ] = g_cq
    rows[V_GCKV] = g_ckv
    rows[V_QB] = jnp.tile(jnp.concatenate([g_qk_b[0], z32]), B_HEADS) * (
        (B_NOPE + B_ROPE) ** -0.5 * LOG2E)
    rows[V_QB_CNT] = jnp.tile(jnp.concatenate(
        [jnp.full((B_NOPE,), 1.0 / B_NOPE), jnp.full((B_ROPE,), 1.0 / B_ROPE), jnp.ones((32,))]),
        B_HEADS)
    rows[V_KB] = jnp.tile(jnp.concatenate([g_qk_b[1, :B_NOPE], z64]), B_HEADS)
    rows[V_KR] = jnp.concatenate([z64, g_qk_b[1, B_NOPE:], z32])
    rows[V_CQ] = jnp.tile(g_qk_c[0], 8) * (C_DH ** -0.5 * LOG2E)
    ones = jnp.ones((LANES,), F32)
    rows[V_G5] = jnp.concatenate([jnp.tile(g_qk_c[1], 2), ones, jnp.tile(g_qk_d[1], 2), ones])
    rows[V_DQ] = jnp.tile(g_qk_d[0], 8) * (D_DH ** -0.5 * LOG2E)
    rows[V_GSUB] = g_sub_a
    rows[V_GX] = g_x
    rows[V_GQX] = g_qk_x[0] * (X_DH ** -0.5 * LOG2E)
    rows[V_GMLP] = g_mlp
    rows[V_GMEM] = g_mem
    rows[V_GKX] = g_qk_x[1]
    return jnp.stack([_pad_row(r.astype(F32)) for r in rows])


def _layout_latent_weights(w_uq, w_ukv):
    wuq = jnp.pad(w_uq.reshape(B_Q_LORA, B_HEADS, B_NOPE + B_ROPE),
                  ((0, 0), (0, 0), (0, 32))).reshape(B_Q_LORA, B_HEADS * LANES)
    wukv = w_ukv.reshape(B_KV_LORA, B_HEADS, B_NOPE + B_DV)
    wukvk = jnp.pad(wukv[:, :, :B_NOPE], ((0, 0), (0, 0), (0, HALF))).reshape(
        B_KV_LORA, B_HEADS * LANES)
    wukvv = wukv[:, :, B_NOPE:].reshape(B_KV_LORA, B_HEADS * B_DV)
    return wuq.astype(BF16), wukvk.astype(BF16), wukvv.astype(BF16)


def _score_bounds(rel_bias, ga, gb, gc, gd, sink):
    def amax(v):
        return jnp.max(jnp.abs(v))

    bound_a = A_DK ** 0.5 * amax(ga[0]) * amax(ga[1]) + amax(rel_bias[:, :A_HEADS])
    norm_b = [jnp.sqrt(B_NOPE * amax(gb[s, :B_NOPE]) ** 2 + B_ROPE * amax(gb[s, B_NOPE:]) ** 2)
              for s in range(2)]
    bound_b = (B_NOPE + B_ROPE) ** -0.5 * norm_b[0] * norm_b[1]
    bound_c = jnp.maximum(
        C_DH ** 0.5 * amax(gc[0]) * amax(gc[1]) + amax(rel_bias[:, A_HEADS:]), amax(sink))
    bound_d = D_DH ** 0.5 * amax(gd[0]) * amax(gd[1])
    return LOG2E * jnp.stack([bound_a, bound_b, bound_c, bound_d])


def _forward(x, mem, rel_bias, g_mix, w_in, lam, g_qk_a, g_sub_a, g_cq, g_ckv, w_uq, w_ukv,
             g_qk_b, g_qk_c, sink_c, g_qk_d, w_branch, w_out, g_x, g_mem, w_xq, w_xkv,
             g_qk_x, w_xo, g_mlp, w_up, w_down, *, t_attn, tk_attn, n_sub, tm, tm_wide,
             rows_c):
    batch, seq, _ = x.shape
    depth = w_in.shape[0]
    assert seq % t_attn == 0 and seq % tm == 0 and seq % rows_c == 0 and seq % GRID_W == 0
    assert seq % tm_wide == 0 and seq % (t_attn * n_sub) == 0 and t_attn % tk_attn == 0

    bd64 = _block_diag_ones([(HALF, True)] * 4)
    bdb = _block_diag_ones([(B_NOPE, True), (B_ROPE, True), (32, False)] * 2)
    cosb, sinb, cosd, sind = _rope_tables(seq)
    t_robust = min(t_attn, 512)
    bias_a_t = _bias_dense(rel_bias, t_attn, tk_attn, key_axis=0)
    bias_c = _bias_window(rel_bias)

    def cols_a(p):
        return (p, p, p, p, p, p)

    def cols_b(p):
        return (2 * p, 2 * p + 1, 2 * p, 2 * p + 1, p, p)

    def cols_b_fast(p):
        return (2 * p, 2 * p + 1, 2 * p, 2 * p + 1, 2 * p, 2 * p + 1)

    def cols_d(p):
        g = p // 2
        return (p, p, 2 * g, 2 * g + 1, g, g)

    vec = jax.vmap(_vector_table)(g_mix, g_qk_a, g_cq, g_ckv, g_qk_b, g_qk_c, g_qk_d, g_sub_a,
                                  g_x, g_qk_x, g_mlp, g_mem)
    win, wgate = jax.vmap(_layout_w_in)(w_in)
    wuq, wukvk, wukvv = jax.vmap(_layout_latent_weights)(w_uq, w_ukv)
    wb, wout, wxq, wxkv, wxo, wup, wdown = (
        w.astype(BF16) for w in (w_branch, w_out, w_xq, w_xkv, w_xo, w_up, w_down))
    bounds = jax.vmap(functools.partial(_score_bounds, rel_bias))(
        g_qk_a, g_qk_b, g_qk_c, g_qk_d, sink_c)

    x2d = x.reshape(batch * seq, D_MODEL)
    for layer in range(depth):
        lam_init = 0.8 - 0.6 * math.exp(-0.3 * layer)
        (qa, ka1, ka2, va, qb, kb, vb, qc, kc, vc, qd, kd, vd, gates, vat, vbt, vdt,
         qat, qbt, qdt) = _front(
            x2d, layer, vec, win, wgate, wuq, wukvk, wukvv, bd64, bdb, cosb, sinb, cosd, sind,
            seq, tm)
        qa, ka1, ka2, va, qb, kb, vb, qc, kc, vc, qd, kd, vd = (
            a.reshape(batch, seq, a.shape[1])
            for a in (qa, ka1, ka2, va, qb, kb, vb, qc, kc, vc, qd, kd, vd))
        bound_a, bound_b, bound_c, bound_d = (bounds[layer, n] for n in range(4))

        diff_kw = dict(lam=lam, vec=vec, layer=layer, lam_init=lam_init)
        o_a = _bounded(
            bound_a,
            lambda: _pair_attention(qat, ka1, ka2, vat, cols_a, A_HEADS, t_attn, "diff", True,
                                    tk=tk_attn, n_sub=n_sub, bias=bias_a_t, **diff_kw),
            lambda: _pair_attention(qa, ka1, ka2, va, cols_a, A_HEADS, t_robust, "diff", False,
                                    bias=_bias_dense(rel_bias, t_robust, t_robust, key_axis=1),
                                    **diff_kw))
        o_b = _bounded(
            bound_b,
            lambda: _pair_attention(qbt, kb, kb, vbt, cols_b_fast, B_HEADS // 2, t_attn, "merge",
                                    True, tk=tk_attn, n_sub=n_sub),
            lambda: _pair_attention(qb, kb, kb, vb, cols_b, B_HEADS // 2, t_robust, "merge", False))
        o_c = _bounded(
            bound_c,
            lambda: _window_attention(qc, kc, vc, bias_c, sink_c, layer, rows_c, True),
            lambda: _window_attention(qc, kc, vc, bias_c, sink_c, layer, rows_c, False))
        o_d = _bounded(
            bound_d,
            lambda: _pair_attention(qdt, kd, kd, vdt, cols_d, D_HEADS // 2, t_attn, "merge", True,
                                    tk=tk_attn, n_sub=n_sub),
            lambda: _pair_attention(qd, kd, kd, vd, cols_d, D_HEADS // 2, t_robust, "merge", False))

        kx, vx = _memkv(mem, layer, vec, wxkv)
        t2 = batch * seq
        x2d = _mix(o_a.reshape(t2, -1), o_b.reshape(t2, -1), o_c.reshape(t2, -1),
                   o_d.reshape(t2, -1), gates, x2d, layer, wb, wout, vec, wxq, kx, vx, wxo,
                   seq, tm_wide)
        x2d = _mlp(x2d, layer, vec, wup, wdown, tm_wide)
    return x2d.reshape(batch, seq, D_MODEL)


def kernel(x, mem, rel_bias, g_mix, w_in, lam, g_qk_a, g_sub_a, g_cq, g_ckv, w_uq, w_ukv, g_qk_b, g_qk_c, sink_c, g_qk_d, w_branch, w_out, g_x, g_mem, w_xq, w_xkv, g_qk_x, w_xo, g_mlp, w_up, w_down):
    return _forward(x, mem, rel_bias, g_mix, w_in, lam, g_qk_a, g_sub_a, g_cq, g_ckv, w_uq, w_ukv,
                    g_qk_b, g_qk_c, sink_c, g_qk_d, w_branch, w_out, g_x, g_mem, w_xq, w_xkv,
                    g_qk_x, w_xo, g_mlp, w_up, w_down, t_attn=512, tk_attn=256, n_sub=4, tm=256,
                    tm_wide=512, rows_c=512)
```

```python
import functools
import math

import numpy as np
import jax
import jax.numpy as jnp
from jax import lax
from jax.experimental import pallas as pl
from jax.experimental.pallas import tpu as pltpu

F32 = jnp.float32
BF16 = jnp.bfloat16

D_MODEL = 1024
GRID_W = 64
EPS = 1e-6
ROPE_THETA = 10000.0
NEG_INF = -1e30
LOG2E = 1.4426950408889634
SAFE_SCORE = 60.0

A_HEADS, A_DK, A_DV = 4, 64, 128
B_HEADS, B_Q_LORA, B_KV_LORA, B_NOPE, B_ROPE, B_DV = 8, 256, 128, 64, 32, 64
C_HEADS, C_KV_HEADS, C_DH, WINDOW = 8, 2, 64, 128
D_HEADS, D_KV_HEADS, D_DH = 8, 2, 64
X_HEADS, X_DH = 4, 128
D_FF = 4 * D_MODEL
REL_BUCKETS, REL_MAX_DIST = 32, 128
N_BRANCH, BRANCH_W = 4, 512

IN_SIZES = (512, 512, 512, B_Q_LORA, B_KV_LORA, B_ROPE, 512, 128, 128, 512, 128, 128,
            N_BRANCH * D_MODEL)
GROUP_W = 512
N_QKV_GROUPS = 7
N_GATE_GROUPS = N_BRANCH * D_MODEL // GROUP_W

LANES = 128
HALF = 64
SEG = 256
VMEM_CAP = 60 * 1024 * 1024

(V_GMIX, V_AQ, V_AK, V_GCQ, V_GCKV, V_QB, V_QB_CNT, V_KB, V_KR, V_CQ, V_G5, V_DQ,
 V_GSUB, V_GX, V_GQX, V_GMLP, V_GMEM, V_GKX, N_VROWS) = range(19)


def _vmem_limit(*nbytes):
    return int(min(VMEM_CAP, sum(nbytes) + (6 << 20)))


def _const_spec(shape):
    zeros = (0,) * len(shape)
    return pl.BlockSpec(shape, lambda *_: zeros, pipeline_mode=pl.Buffered(1))


def _layer_spec(stacked, layer):
    index = (layer,) + (0,) * (stacked.ndim - 1)
    return pl.BlockSpec((None,) + stacked.shape[1:], lambda *_: index,
                        pipeline_mode=pl.Buffered(1))


def _nbytes(shape, dtype):
    return int(np.prod(shape)) * jnp.dtype(dtype).itemsize


def _dot(a, b):
    return jnp.dot(a, b, preferred_element_type=F32)


def _dot_nt(a, b):
    return lax.dot_general(a, b, (((1,), (1,)), ((), ())), preferred_element_type=F32)


def _rms_rows(x, g):
    return x * lax.rsqrt(jnp.mean(x * x, axis=-1, keepdims=True) + EPS) * g


def _seg_ssq(y, bd):
    y2 = (y * y).astype(BF16)
    parts = [_dot(y2[:, c * SEG:(c + 1) * SEG], bd) for c in range(y.shape[1] // SEG)]
    return parts[0] if len(parts) == 1 else jnp.concatenate(parts, axis=1)


def _seg_scale(y, ssq, inv_cnt, gain):
    return y * lax.rsqrt(ssq * inv_cnt + EPS) * gain


def _rope32(x, cos, sin):
    w = x.shape[1]
    lane = lax.broadcasted_iota(jnp.int32, x.shape, 1)
    first = (lane & 16) == 0
    rot = jnp.where(first, -pltpu.roll(x, w - 16, 1), pltpu.roll(x, 16, 1))
    return x * cos + rot * sin


def _tile_lanes(v, n):
    return v if n == 1 else jnp.concatenate([v] * n, axis=1)


def _pair_split(x, fill):
    lane = lax.broadcasted_iota(jnp.int32, x.shape, 1)
    low = lane < HALF
    sw = pltpu.roll(x, HALF, 1)
    return (jnp.where(low, x, fill), jnp.where(low, fill, sw), jnp.where(low, sw, fill),
            jnp.where(low, fill, x))


def _front_kernel(x_ref, vec_ref, win_ref, wgate_ref, wuq_ref, wukvk_ref, wukvv_ref, bd64_ref,
                  bdb_ref,
                  cosb_ref, sinb_ref, cosd_ref, sind_ref,
                  qa_ref, ka1_ref, ka2_ref, va_ref, qb_ref, kb_ref, vb_ref,
                  qc_ref, kc_ref, vc_ref, qd_ref, kd_ref, vd_ref, gate_ref,
                  vat_ref, vbt_ref, vdt_ref, qat_ref, qbt_ref, qdt_ref):
    def row(r, w):
        return vec_ref[r:r + 1, :w]

    inv64 = 1.0 / HALF
    bd64 = bd64_ref[...]
    h = _rms_rows(x_ref[...], row(V_GMIX, D_MODEL)).astype(BF16)

    def group(g):
        w_ref, g = (win_ref, g) if g < N_QKV_GROUPS else (wgate_ref, g - N_QKV_GROUPS)
        return _dot(h, w_ref[:, g * GROUP_W:(g + 1) * GROUP_W])

    lane512 = lax.broadcasted_iota(jnp.int32, (x_ref.shape[0], GROUP_W), 1)

    def gate(c, z):
        gate_ref[:, c * GROUP_W:(c + 1) * GROUP_W] = (1.0 / (1.0 + jnp.exp(-z))).astype(BF16)

    def gate_group(c):
        return group(N_QKV_GROUPS + c)

    def finish_a_q(y, ssq):
        qa = _seg_scale(y, ssq, inv64, row(V_AQ, GROUP_W))
        qa_ref[...] = qa.astype(BF16)
        qat_ref[...] = qa.T.astype(BF16)

    def finish_a_kv(yk, ssq, yv):
        ka = _seg_scale(yk, ssq, inv64, row(V_AK, GROUP_W))
        first_map = (lane512 & HALF) == 0
        ka1_ref[...] = jnp.where(first_map, ka, 0.0).astype(BF16)
        ka2_ref[...] = jnp.where(first_map, 0.0, ka).astype(BF16)
        va_ref[...] = yv.astype(BF16)
        vat_ref[...] = yv.T.astype(BF16)

    def finish_b(tq, ssq_tq, tk, ssq_tk, kr, vb):
        cosb = cosb_ref[...]
        sinb = sinb_ref[...]
        qb = _seg_scale(tq, ssq_tq, row(V_QB_CNT, D_MODEL), row(V_QB, D_MODEL))
        qb = _rope32(qb, _tile_lanes(cosb, B_HEADS), _tile_lanes(sinb, B_HEADS))
        qb_ref[...] = qb.astype(BF16)
        qbt_ref[...] = qb.T.astype(BF16)
        k_nope = _seg_scale(tk, ssq_tk, inv64, row(V_KB, D_MODEL))
        kr = kr * lax.rsqrt(jnp.sum(kr * kr, axis=-1, keepdims=True) * (1.0 / B_ROPE) + EPS)
        kr = _rope32(kr * row(V_KR, LANES), cosb, sinb)
        kb_ref[...] = (k_nope + _tile_lanes(kr, B_HEADS)).astype(BF16)
        vb_ref[...] = vb.astype(BF16)
        vbt_ref[...] = vb.T.astype(BF16)

    def finish_cd_kv(y, ssq):
        yn = _seg_scale(y, ssq, inv64, row(V_G5, GROUP_W))
        kc = yn[:, 0:LANES]
        vc = y[:, LANES:2 * LANES]
        kd = _rope32(yn[:, 2 * LANES:3 * LANES], cosd_ref[...], sind_ref[...])
        vd = y[:, 3 * LANES:]
        kc_ref[...] = jnp.concatenate(_pair_split(kc, 0.0), axis=1).astype(BF16)
        kd_ref[...] = jnp.concatenate(_pair_split(kd, 0.0), axis=1).astype(BF16)
        vc_ref[...] = jnp.concatenate(_pair_split(vc, 1.0), axis=1).astype(BF16)
        low = lax.broadcasted_iota(jnp.int32, vd.shape, 1) < HALF
        vd_sw = pltpu.roll(vd, HALF, 1)
        vd_ref[...] = jnp.concatenate([jnp.where(low, vd, vd_sw), jnp.where(low, vd_sw, vd)],
                                      axis=1).astype(BF16)
        vdt_ref[...] = vd.T.astype(BF16)

    def finish_c_q(y, ssq):
        qc_ref[...] = _seg_scale(y, ssq, inv64, row(V_CQ, GROUP_W)).astype(BF16)

    def finish_d_q(y, ssq):
        qd = _seg_scale(y, ssq, inv64, row(V_DQ, GROUP_W))
        qd = _rope32(qd, _tile_lanes(cosd_ref[...], 4), _tile_lanes(sind_ref[...], 4))
        qd_ref[...] = qd.astype(BF16)
        qdt_ref[...] = qd.T.astype(BF16)

    y0 = group(0)
    y1 = group(1)
    ssq0 = _seg_ssq(y0, bd64)
    y2 = group(2)
    finish_a_q(y0, ssq0)
    ssq1 = _seg_ssq(y1, bd64)
    y3 = group(3)
    finish_a_kv(y1, ssq1, y2)
    cq = _rms_rows(y3[:, :B_Q_LORA], row(V_GCQ, B_Q_LORA)).astype(BF16)
    ckv = _rms_rows(y3[:, B_Q_LORA:B_Q_LORA + B_KV_LORA], row(V_GCKV, B_KV_LORA)).astype(BF16)
    z = gate_group(0)
    tq = _dot(cq, wuq_ref[...])
    tk = _dot(ckv, wukvk_ref[...])
    vb = _dot(ckv, wukvv_ref[...])
    gate(0, z)
    y4 = group(4)
    ssq_tq = _seg_ssq(tq, bdb_ref[...])
    ssq_tk = _seg_ssq(tk, bd64)
    gate(1, gate_group(1))
    z = gate_group(2)
    finish_b(tq, ssq_tq, tk, ssq_tk, y3[:, B_Q_LORA + B_KV_LORA:], vb)
    gate(2, z)
    y5 = group(5)
    ssq4 = _seg_ssq(y4, bd64)
    gate(3, gate_group(3))
    z = gate_group(4)
    finish_c_q(y4, ssq4)
    gate(4, z)
    y6 = group(6)
    ssq5 = _seg_ssq(y5, bd64)
    gate(5, gate_group(5))
    z = gate_group(6)
    finish_cd_kv(y5, ssq5)
    gate(6, z)
    ssq6 = _seg_ssq(y6, bd64)
    z = gate_group(7)
    finish_d_q(y6, ssq6)
    gate(7, z)
    assert N_GATE_GROUPS == 8


def _front(x2d, layer, vec, win, wgate, wuq, wukvk, wukvv, bd64, bdb, cosb, sinb, cosd, sind,
           seq, tm):
    t_tokens = x2d.shape[0]
    n_pos = seq // tm
    widths = (512, 512, 512, 512, 1024, 1024, 512, 512, 512, 512, 512, 512, 256, 4096)
    t_rows = (A_HEADS * A_DV, B_HEADS * B_DV, D_KV_HEADS * D_DH, A_HEADS * LANES,
              B_HEADS * LANES, D_HEADS * D_DH)

    def tile(w):
        return pl.BlockSpec((tm, w), lambda i: (i, 0))

    def pos(w):
        return pl.BlockSpec((tm, w), lambda i: (i % n_pos, 0))

    def tile_t(r):
        return pl.BlockSpec((None, r, tm), lambda i: (i // n_pos, 0, i % n_pos))

    per_layer = (vec, win, wgate, wuq, wukvk, wukvv)
    consts = per_layer + (bd64, bdb)
    in_specs = ([tile(D_MODEL)] + [_layer_spec(c, layer) for c in per_layer]
                + [_const_spec(bd64.shape), _const_spec(bdb.shape)] + [pos(LANES)] * 4)
    limit = _vmem_limit(sum(_nbytes(c.shape[-2:], c.dtype) for c in consts),
                        2 * tm * (D_MODEL * 4 + 4 * LANES * 4 + (sum(widths) + sum(t_rows)) * 2),
                        12 * tm * D_MODEL * 4)
    return pl.pallas_call(
        _front_kernel,
        grid=(t_tokens // tm,),
        in_specs=in_specs,
        out_specs=[tile(w) for w in widths] + [tile_t(r) for r in t_rows],
        out_shape=([jax.ShapeDtypeStruct((t_tokens, w), BF16) for w in widths]
                   + [jax.ShapeDtypeStruct((t_tokens // seq, r, seq), BF16) for r in t_rows]),
        compiler_params=pltpu.CompilerParams(dimension_semantics=("arbitrary",),
                                             vmem_limit_bytes=limit),
        name="front",
    )(x2d, *consts, cosb, sinb, cosd, sind)


def _t5_bucket(rel):
    nb = REL_BUCKETS // 2
    max_exact = nb // 2
    n = jnp.abs(rel)
    nf = jnp.maximum(n, 1).astype(F32)
    large = max_exact + (jnp.log(nf / max_exact) / math.log(REL_MAX_DIST / max_exact)
                         * (nb - max_exact)).astype(jnp.int32)
    large = jnp.minimum(large, nb - 1)
    return jnp.where(rel > 0, nb, 0) + jnp.where(n < max_exact, n, large)


def _table_lookup(bucket, tbl_ref, col):
    acc = jnp.zeros(bucket.shape, F32)
    for b in range(REL_BUCKETS):
        acc = jnp.where(bucket == b, tbl_ref[b, col], acc)
    return acc


def _bias_dense_kernel(tbl_ref, o_ref, *, tk, key_axis):
    h = pl.program_id(0)
    d = pl.program_id(1) - 2
    last = pl.num_programs(1) - 3
    half = REL_BUCKETS // 2

    @pl.when(d == -2)
    def _():
        o_ref[...] = jnp.full(o_ref.shape, tbl_ref[half - 1, h] * LOG2E, F32)

    @pl.when(d == last)
    def _():
        o_ref[...] = jnp.full(o_ref.shape, tbl_ref[REL_BUCKETS - 1, h] * LOG2E, F32)

    @pl.when((d > -2) & (d < last))
    def _():
        rel = (d * tk + lax.broadcasted_iota(jnp.int32, o_ref.shape, key_axis)
               - lax.broadcasted_iota(jnp.int32, o_ref.shape, 1 - key_axis))
        o_ref[...] = _table_lookup(_t5_bucket(rel), tbl_ref, h) * LOG2E


def _bias_window_kernel(tbl_ref, o_ref):
    h = pl.program_id(0)
    shape = (LANES, 3 * LANES)
    rel = (lax.broadcasted_iota(jnp.int32, shape, 1) - LANES
           - lax.broadcasted_iota(jnp.int32, shape, 0))
    bias = _table_lookup(_t5_bucket(rel), tbl_ref, A_HEADS + h) * LOG2E
    o_ref[...] = jnp.where(jnp.abs(rel) <= WINDOW, bias, NEG_INF)


def _bias_dense(rel_bias, tq, tk, key_axis):
    assert tk >= LANES and tq % tk == 0
    n_tiles = tq // tk + 4
    shape = (tk, tq) if key_axis == 0 else (tq, tk)
    return pl.pallas_call(
        functools.partial(_bias_dense_kernel, tk=tk, key_axis=key_axis),
        grid=(A_HEADS, n_tiles),
        in_specs=[pl.BlockSpec(memory_space=pltpu.SMEM)],
        out_specs=pl.BlockSpec((None, None) + shape, lambda h, d: (h, d, 0, 0)),
        out_shape=jax.ShapeDtypeStruct((A_HEADS, n_tiles) + shape, F32),
        name="bias_dense",
    )(rel_bias)


def _bias_window(rel_bias):
    return pl.pallas_call(
        _bias_window_kernel,
        grid=(C_HEADS,),
        in_specs=[pl.BlockSpec(memory_space=pltpu.SMEM)],
        out_specs=pl.BlockSpec((None, LANES, 3 * LANES), lambda h: (h, 0, 0)),
        out_shape=jax.ShapeDtypeStruct((C_HEADS, LANES, 3 * LANES), F32),
        name="bias_window",
    )(rel_bias)


def _pair_attn_kernel(*refs, t, n_chunks, mode, lam_init):
    if mode == "diff":
        (qa_ref, qb_ref, ka_ref, kb_ref, va_ref, vb_ref, bias_ref, lam_ref, gsub_ref, o_ref,
         ma, la, acca, mb, lb, accb) = refs
    else:
        (qa_ref, qb_ref, ka_ref, kb_ref, va_ref, vb_ref, o_ref,
         ma, la, acca, mb, lb, accb) = refs
        bias_ref = None
    i = pl.program_id(2)
    streams = ((qa_ref[...], ka_ref, va_ref, ma, la, acca),
               (qb_ref[...], kb_ref, vb_ref, mb, lb, accb))
    for _, _, _, m, l, acc in streams:
        m[...] = jnp.full(m.shape, NEG_INF, F32)
        l[...] = jnp.zeros(l.shape, F32)
        acc[...] = jnp.zeros(acc.shape, F32)

    def body(c, carry):
        off = pl.multiple_of(c * t, t)
        bias = None if bias_ref is None else bias_ref[jnp.clip(c - i, -2, 2) + 2]
        for q, k_ref, v_ref, m, l, acc in streams:
            s = _dot_nt(q, k_ref[pl.ds(off, t), :])
            if bias is not None:
                s = s + bias
            m_prev = m[...]
            m_new = jnp.maximum(m_prev, jnp.max(s, axis=1, keepdims=True))
            alpha = jnp.exp2(m_prev - m_new)
            p = jnp.exp2(s - m_new)
            l[...] = alpha * l[...] + jnp.sum(p, axis=1, keepdims=True)
            acc[...] = alpha * acc[...] + _dot(p.astype(BF16), v_ref[pl.ds(off, t), :])
            m[...] = m_new
        return carry

    lax.fori_loop(0, n_chunks, body, 0)
    _finish_pair(acca[...] / la[...], accb[...] / lb[...], mode, refs, lam_init)


def _write_diff(oa, ob, lam_ref, vec_ref, o_ref, lam_init):
    lam = lam_ref[...]
    lam_full = (jnp.exp(jnp.sum(lam[0:1] * lam[1:2], keepdims=True))
                - jnp.exp(jnp.sum(lam[2:3] * lam[3:4], keepdims=True)) + lam_init)
    o = oa - lam_full * ob
    gsub = vec_ref[V_GSUB:V_GSUB + 1, :A_DV]
    o_ref[...] = (_rms_rows(o, gsub) * (1.0 - lam_init)).astype(BF16)


def _finish_pair(oa, ob, mode, refs, lam_init):
    if mode == "diff":
        _write_diff(oa, ob, *refs[7:10], lam_init)
    else:
        o_ref = refs[6]
        lane = lax.broadcasted_iota(jnp.int32, oa.shape, 1)
        o_ref[...] = jnp.where(lane < HALF, oa, ob).astype(BF16)


def _pair_attn_fast_kernel(*refs, tq, tk, n_sub, n_chunks, mode, lam_init):
    qta_ref, qtb_ref, ka_ref, kb_ref, vta_ref, vtb_ref = refs[:6]
    bias_ref = refs[6] if mode == "diff" else None
    o_ref = refs[9] if mode == "diff" else refs[6]
    i = pl.program_id(2)
    ratio = tq // tk
    streams = ((qta_ref, ka_ref, vta_ref), (qtb_ref, kb_ref, vtb_ref))

    tasks = [(u, c, n_s) for u in range(n_sub) for c in range(n_chunks) for n_s in range(2)]

    def scores(u, c, n_s):
        qt_ref, k_ref, _ = streams[n_s]
        s = _dot(k_ref[c * tk:(c + 1) * tk, :], qt_ref[:, u * tq:(u + 1) * tq])
        if bias_ref is not None:
            s = s + bias_ref[jnp.clip(c - (i * n_sub + u) * ratio, -2, ratio + 1) + 2]
        return s

    def finish(u):
        oa = sums.pop((u, 0)) / jnp.sum(denoms.pop((u, 0)), axis=0, keepdims=True)
        ob = sums.pop((u, 1)) / jnp.sum(denoms.pop((u, 1)), axis=0, keepdims=True)
        out = o_ref.at[u * tq:(u + 1) * tq, :]
        if mode == "diff":
            _write_diff(oa.T, ob.T, refs[7], refs[8], out, lam_init)
        else:
            out[...] = jnp.concatenate([oa, ob], axis=0).T.astype(BF16)

    sums = {}
    denoms = {}
    s_live, p_live = {}, {}
    for j in range(len(tasks) + 2):
        if j < len(tasks):
            s_live[j] = scores(*tasks[j])
        if 0 <= j - 1 < len(tasks):
            u, _, n_s = tasks[j - 1]
            p = jnp.exp2(s_live.pop(j - 1))
            part = jnp.sum(p.reshape(tk // 8, 8, tq), axis=0)
            denoms[u, n_s] = part if (u, n_s) not in denoms else denoms[u, n_s] + part
            p_live[j - 1] = p.astype(BF16)
        if 0 <= j - 2 < len(tasks):
            u, c, n_s = tasks[j - 2]
            part = _dot(streams[n_s][2][:, c * tk:(c + 1) * tk], p_live.pop(j - 2))
            sums[u, n_s] = part if (u, n_s) not in sums else sums[u, n_s] + part
            if c == n_chunks - 1 and n_s == 1:
                finish(u)


def _pair_attention(q, k_a, k_b, v, cols, n_pairs, t, mode, fast, tk=None, n_sub=1, bias=None,
                    lam=None, vec=None, layer=None, lam_init=0.0):
    batch, seq = k_a.shape[:2]

    def kv_spec(which):
        return pl.BlockSpec((None, seq, LANES), lambda b, p, i: (b, 0, cols(p)[which]))

    if fast:
        v_rows = A_DV if mode == "diff" else HALF
        t_sub, t = t, t * n_sub

        def q_spec(which):
            return pl.BlockSpec((None, LANES, t), lambda b, p, i: (b, cols(p)[which], i))

        def v_spec(which):
            return pl.BlockSpec((None, v_rows, seq), lambda b, p, i: (b, cols(p)[which], 0))

        body = functools.partial(_pair_attn_fast_kernel, tq=t_sub, tk=tk, n_sub=n_sub,
                                 n_chunks=seq // tk)
        scratch = []
        v_bytes = v_rows * seq * 2
    else:
        tk = t

        def q_spec(which):
            return pl.BlockSpec((None, t, LANES), lambda b, p, i: (b, i, cols(p)[which]))

        v_spec = kv_spec
        body = functools.partial(_pair_attn_kernel, t=t, n_chunks=seq // t)
        scratch = [pltpu.VMEM((t, 1), F32), pltpu.VMEM((t, 1), F32),
                   pltpu.VMEM((t, LANES), F32)] * 2
        v_bytes = seq * LANES * 2
    in_specs = [q_spec(0), q_spec(1), kv_spec(2), kv_spec(3), v_spec(4), v_spec(5)]
    args = [q, q, k_a, k_b, v, v]
    resident = 2 * 2 * (seq * LANES * 2 + v_bytes)
    if mode == "diff":
        in_specs += [pl.BlockSpec((None,) + bias.shape[1:], lambda b, p, i: (p, 0, 0, 0)),
                     _layer_spec(lam, layer), _layer_spec(vec, layer)]
        args += [bias, lam, vec]
        resident += 2 * _nbytes(bias.shape[1:], F32)
    limit = _vmem_limit(resident, 6 * t * LANES * 2, 8 * t * LANES * 4, 8 * t * tk * 4)
    return pl.pallas_call(
        functools.partial(body, mode=mode, lam_init=lam_init),
        grid=(batch, n_pairs, seq // t),
        in_specs=in_specs,
        out_specs=pl.BlockSpec((None, t, LANES), lambda b, p, i: (b, i, p)),
        out_shape=jax.ShapeDtypeStruct((batch, seq, n_pairs * LANES), BF16),
        scratch_shapes=scratch,
        compiler_params=pltpu.CompilerParams(
            dimension_semantics=("arbitrary", "arbitrary", "arbitrary"),
            vmem_limit_bytes=limit),
        name=("attn_fast_" if fast else "attn_") + mode,
    )(*args)


def _bounded(score_bound, fast, robust):
    return lax.cond(score_bound <= SAFE_SCORE, fast, robust)


def _window_kernel(sink_ref, q_ref, kp_ref, km_ref, kn_ref, vp_ref, vm_ref, vn_ref, bias_ref,
                   o_ref, *, rows, fast, layer):
    i = pl.program_id(1)
    n_steps = pl.num_programs(1)
    nblk = rows // LANES
    lane = lax.broadcasted_iota(jnp.int32, (LANES, LANES), 1)
    zero_row = jnp.zeros((1, LANES), F32)

    def kv_block(j):
        if j < 0:
            return kp_ref[...], vp_ref[...]
        if j >= nblk:
            return kn_ref[...], vn_ref[...]
        return km_ref[j * LANES:(j + 1) * LANES, :], vm_ref[j * LANES:(j + 1) * LANES, :]

    kcat, vcat, edge = [], [], []
    for r in range(nblk):
        blocks = [kv_block(r + d) for d in (-1, 0, 1)]
        kcat.append(jnp.concatenate([blk[0] for blk in blocks], axis=0))
        vcat.append(jnp.concatenate([blk[1] for blk in blocks], axis=0))
        if r == 0 or r == nblk - 1:
            before = zero_row + jnp.where(i > 0, 0.0, NEG_INF) if r == 0 else zero_row
            after = zero_row + jnp.where(i < n_steps - 1, 0.0, NEG_INF) if r == nblk - 1 else zero_row
            edge.append(jnp.concatenate([before, zero_row, after], axis=1))
        else:
            edge.append(None)

    def col(p, e):
        g = p // (C_HEADS // C_KV_HEADS // 2)
        return slice((2 * g + e) * LANES, (2 * g + e + 1) * LANES)

    def scores(r, p, e):
        q = q_ref[r * LANES:(r + 1) * LANES, p * LANES:(p + 1) * LANES]
        s = _dot_nt(q, kcat[r][:, col(p, e)]) + bias_ref[2 * p + e]
        return s if edge[r] is None else s + edge[r]

    def store(r, p, outs):
        o_ref[r * LANES:(r + 1) * LANES, p * LANES:(p + 1) * LANES] = (
            jnp.where(lane < HALF, outs[0], outs[1]).astype(BF16))

    tasks = [(r, p, e) for r in range(nblk) for p in range(C_HEADS // 2) for e in range(2)]
    if fast:
        s_live, p_live, outs = {}, {}, []
        for j in range(len(tasks) + 2):
            if j < len(tasks):
                s_live[j] = scores(*tasks[j])
            if 0 <= j - 1 < len(tasks):
                p_live[j - 1] = jnp.exp2(s_live.pop(j - 1)).astype(BF16)
            if 0 <= j - 2 < len(tasks):
                r, p, e = tasks[j - 2]
                o = _dot(p_live.pop(j - 2), vcat[r][:, col(p, e)])
                sink = sink_ref[layer, 2 * p + e] * LOG2E
                outs.append(o / (pltpu.roll(o, HALF, 1) + jnp.exp2(zero_row + sink)))
                if e == 1:
                    store(r, p, outs)
                    outs = []
    else:
        outs = []
        for r, p, e in tasks:
            s = scores(r, p, e)
            sink = sink_ref[layer, 2 * p + e] * LOG2E
            m = jnp.maximum(jnp.max(s, axis=1, keepdims=True), sink)
            pr = jnp.exp2(s - m)
            denom = jnp.sum(pr, axis=1, keepdims=True) + jnp.exp2(sink - m)
            outs.append(_dot(pr.astype(BF16), vcat[r][:, col(p, e)]) / denom)
            if e == 1:
                store(r, p, outs)
                outs = []


def _window_attention(q, k, v, bias, sink, layer, rows, fast):
    batch, seq, _ = q.shape
    nblk = rows // LANES
    last = seq // LANES - 1
    kw, vw = k.shape[2], v.shape[2]

    def main(w):
        return pl.BlockSpec((None, rows, w), lambda b, i: (b, i, 0))

    def prev(w):
        return pl.BlockSpec((None, LANES, w), lambda b, i: (b, jnp.maximum(i * nblk - 1, 0), 0))

    def nxt(w):
        return pl.BlockSpec((None, LANES, w), lambda b, i: (b, jnp.minimum((i + 1) * nblk, last), 0))

    limit = _vmem_limit(_nbytes(bias.shape, F32) * 2,
                        2 * (rows + 2 * LANES) * (kw + vw) * 2 + 4 * rows * 512 * 2,
                        16 * LANES * 3 * LANES * 4)
    return pl.pallas_call(
        functools.partial(_window_kernel, rows=rows, fast=fast, layer=layer),
        grid=(batch, seq // rows),
        in_specs=[pl.BlockSpec(memory_space=pltpu.SMEM), main(q.shape[2]),
                  prev(kw), main(kw), nxt(kw), prev(vw), main(vw), nxt(vw),
                  _const_spec(bias.shape)],
        out_specs=main(q.shape[2]),
        out_shape=jax.ShapeDtypeStruct(q.shape, BF16),
        compiler_params=pltpu.CompilerParams(dimension_semantics=("arbitrary", "arbitrary"),
                                             vmem_limit_bytes=limit),
        name="attn_window_fast" if fast else "attn_window",
    )(sink, q, k, k, k, v, v, v, bias)


def _memkv_kernel(mem_ref, vec_ref, w_ref, k_ref, v_ref):
    hn = _rms_rows(mem_ref[...], vec_ref[V_GMEM:V_GMEM + 1, :]).astype(BF16)
    kv = _dot(hn, w_ref[...])
    width = X_HEADS * X_DH
    gk = vec_ref[V_GKX:V_GKX + 1, :X_DH]
    for h in range(X_HEADS):
        sl = slice(h * X_DH, (h + 1) * X_DH)
        k_ref[:, sl] = _rms_rows(kv[:, sl], gk).astype(BF16)
    v_ref[...] = kv[:, width:].astype(BF16)


def _memkv(mem, layer, vec, w):
    batch, n_mem, _ = mem.shape
    width = X_HEADS * X_DH
    out = pl.BlockSpec((None, n_mem, width), lambda b: (b, 0, 0))
    return pl.pallas_call(
        _memkv_kernel,
        grid=(batch,),
        in_specs=[pl.BlockSpec((None, n_mem, D_MODEL), lambda b: (b, 0, 0)),
                  _layer_spec(vec, layer), _layer_spec(w, layer)],
        out_specs=[out, out],
        out_shape=[jax.ShapeDtypeStruct((batch, n_mem, width), BF16)] * 2,
        compiler_params=pltpu.CompilerParams(
            dimension_semantics=("arbitrary",),
            vmem_limit_bytes=_vmem_limit(_nbytes(w.shape[1:], BF16), 4 * n_mem * D_MODEL * 4,
                                         4 * n_mem * 2 * width * 4)),
        name="memkv",
    )(mem, vec, w)


def _mix_kernel(oa_ref, ob_ref, oc_ref, od_ref, gate_ref, x_ref, wb_ref, wout_ref, vec_ref,
                wxq_ref, kx_ref, vx_ref, wxo_ref, o_ref):
    mix = None
    for m, o_m in enumerate((oa_ref, ob_ref, oc_ref, od_ref)):
        br = _dot(o_m[...], wb_ref[m])
        term = gate_ref[:, m * D_MODEL:(m + 1) * D_MODEL].astype(F32) * br
        mix = term if mix is None else mix + term
    x1 = x_ref[...] + _dot(mix.astype(BF16), wout_ref[...])
    hn = _rms_rows(x1, vec_ref[V_GX:V_GX + 1, :]).astype(BF16)
    qx = _dot(hn, wxq_ref[...])
    gq = vec_ref[V_GQX:V_GQX + 1, :X_DH]

    def head_cols(h):
        return slice(h * X_DH, (h + 1) * X_DH)

    s_live, p_live, heads = {}, {}, []
    for h in range(X_HEADS + 2):
        if h < X_HEADS:
            qh = _rms_rows(qx[:, head_cols(h)], gq).astype(BF16)
            s_live[h] = _dot_nt(qh, kx_ref[:, head_cols(h)])
        if 0 <= h - 1 < X_HEADS:
            s = s_live.pop(h - 1)
            p = jnp.exp2(s - jnp.max(s, axis=1, keepdims=True))
            p_live[h - 1] = (p.astype(BF16), jnp.sum(p, axis=1, keepdims=True))
        if 0 <= h - 2 < X_HEADS:
            p, denom = p_live.pop(h - 2)
            heads.append((_dot(p, vx_ref[:, head_cols(h - 2)]) / denom).astype(BF16))
    ox = jnp.concatenate(heads, axis=1)
    o_ref[...] = x1 + _dot(ox, wxo_ref[...])


def _mix(o_a, o_b, o_c, o_d, gates, x2d, layer, wb, wout, vec, wxq, kx, vx, wxo, seq, tm):
    t_tokens = x2d.shape[0]
    per_batch = seq // tm
    n_mem, width = kx.shape[1], kx.shape[2]

    def tile(w):
        return pl.BlockSpec((tm, w), lambda i: (i, 0))

    def per_layer(a):
        return _layer_spec(a, layer)

    mem_spec = pl.BlockSpec((None, n_mem, width), lambda i: (i // per_batch, 0, 0))
    limit = _vmem_limit(sum(_nbytes(c.shape[1:], c.dtype) for c in (wb, wout, vec, wxq, wxo)),
                        2 * tm * (4 * BRANCH_W * 2 + 4 * D_MODEL * 2 + 2 * D_MODEL * 4),
                        4 * n_mem * width * 2, 10 * tm * D_MODEL * 4)
    return pl.pallas_call(
        _mix_kernel,
        grid=(t_tokens // tm,),
        in_specs=[tile(BRANCH_W)] * 4 + [tile(N_BRANCH * D_MODEL), tile(D_MODEL),
                  per_layer(wb), per_layer(wout), per_layer(vec), per_layer(wxq),
                  mem_spec, mem_spec, per_layer(wxo)],
        out_specs=tile(D_MODEL),
        out_shape=jax.ShapeDtypeStruct((t_tokens, D_MODEL), F32),
        compiler_params=pltpu.CompilerParams(dimension_semantics=("arbitrary",),
                                             vmem_limit_bytes=limit),
        name="mix",
    )(o_a, o_b, o_c, o_d, gates, x2d, wb, wout, vec, wxq, kx, vx, wxo)


def _mlp_kernel(x_ref, vec_ref, wup_ref, wdown_ref, o_ref, *, chunk):
    x = x_ref[...]
    hn = _rms_rows(x, vec_ref[V_GMLP:V_GMLP + 1, :]).astype(BF16)
    acc = x
    for c in range(D_FF // chunk):
        u = jnp.maximum(_dot(hn, wup_ref[:, c * chunk:(c + 1) * chunk]), 0.0)
        acc = acc + _dot((u * u).astype(BF16), wdown_ref[c * chunk:(c + 1) * chunk, :])
    o_ref[...] = acc


def _mlp(x2d, layer, vec, wup, wdown, tm, chunk=512):
    t_tokens = x2d.shape[0]
    tile = pl.BlockSpec((tm, D_MODEL), lambda i: (i, 0))
    limit = _vmem_limit(_nbytes(wup.shape[1:], BF16) + _nbytes(wdown.shape[1:], BF16),
                        4 * tm * D_MODEL * 4, 8 * tm * max(chunk, D_MODEL) * 4)
    return pl.pallas_call(
        functools.partial(_mlp_kernel, chunk=chunk),
        grid=(t_tokens // tm,),
        in_specs=[tile, _layer_spec(vec, layer), _layer_spec(wup, layer),
                  _layer_spec(wdown, layer)],
        out_specs=tile,
        out_shape=jax.ShapeDtypeStruct((t_tokens, D_MODEL), F32),
        compiler_params=pltpu.CompilerParams(dimension_semantics=("arbitrary",),
                                             vmem_limit_bytes=limit),
        name="mlp",
    )(x2d, vec, wup, wdown)


def _block_diag_ones(sizes):
    m = np.zeros((SEG, SEG), np.float32)
    start = 0
    for size, on in sizes:
        if on:
            m[start:start + size, start:start + size] = 1.0
        start += size
    assert start == SEG
    return jnp.asarray(m, BF16)


def _rope_tables(seq):
    pos = jnp.arange(seq)
    inv = ROPE_THETA ** (-jnp.arange(0, B_ROPE, 2, dtype=F32) / B_ROPE)

    def cs(p):
        ang = inv[:, None] * p.astype(F32)[None, :]
        return (jnp.concatenate([jnp.cos(ang)] * 2, 0).T, jnp.concatenate([jnp.sin(ang)] * 2, 0).T)

    cos_p, sin_p = cs(pos)
    one = jnp.ones((seq, 1), F32)
    zero = jnp.zeros((seq, 1), F32)
    cosb = jnp.concatenate([one * jnp.ones((1, HALF)), cos_p, one * jnp.ones((1, 32))], 1)
    sinb = jnp.concatenate([zero * jnp.ones((1, HALF)), sin_p, zero * jnp.ones((1, 32))], 1)
    cos_r, sin_r = cs(pos // GRID_W)
    cos_c, sin_c = cs(pos % GRID_W)
    cosd = jnp.concatenate([cos_r, cos_c] * 2, 1)
    sind = jnp.concatenate([sin_r, sin_c] * 2, 1)
    return cosb, sinb, cosd, sind


def _layout_w_in(w):
    off = np.concatenate([[0], np.cumsum(IN_SIZES)])

    def seg(i):
        return w[:, off[i]:off[i + 1]]

    def zeros(n):
        return jnp.zeros((w.shape[0], n), w.dtype)

    w = w.astype(BF16)
    cols = [seg(0), seg(1), seg(2), seg(3), seg(4), zeros(HALF), seg(5), zeros(32),
            seg(6), seg(7), seg(8), seg(10), seg(11), seg(9)]
    qkv = jnp.concatenate(cols, axis=1)
    assert qkv.shape[1] == N_QKV_GROUPS * GROUP_W
    return qkv, seg(12)


def _pad_row(v):
    return jnp.pad(v, (0, D_MODEL - v.shape[0]))


def _vector_table(g_mix, g_qk_a, g_cq, g_ckv, g_qk_b, g_qk_c, g_qk_d, g_sub_a, g_x, g_qk_x,
                  g_mlp, g_mem):
    z32 = jnp.zeros((32,), F32)
    z64 = jnp.zeros((HALF,), F32)
    rows = [None] * N_VROWS
    rows[V_GMIX] = g_mix
    rows[V_AQ] = jnp.tile(g_qk_a[0], 8) * (A_DK ** -0.5 * LOG2E)
    rows[V_AK] = jnp.tile(g_qk_a[1], 8)
    rows[V_GCQ] = g_cq
    rows[V_GCKV] = g_ckv
    rows[V_QB] = jnp.tile(jnp.concatenate([g_qk_b[0], z32]), B_HEADS) * (
        (B_NOPE + B_ROPE) ** -0.5 * LOG2E)
    rows[V_QB_CNT] = jnp.tile(jnp.concatenate(
        [jnp.full((B_NOPE,), 1.0 / B_NOPE), jnp.full((B_ROPE,), 1.0 / B_ROPE), jnp.ones((32,))]),
        B_HEADS)
    rows[V_KB] = jnp.tile(jnp.concatenate([g_qk_b[1, :B_NOPE], z64]), B_HEADS)
    rows[V_KR] = jnp.concatenate([z64, g_qk_b[1, B_NOPE:], z32])
    rows[V_CQ] = jnp.tile(g_qk_c[0], 8) * (C_DH ** -0.5 * LOG2E)
    ones = jnp.ones((LANES,), F32)
    rows[V_G5] = jnp.concatenate([jnp.tile(g_qk_c[1], 2), ones, jnp.tile(g_qk_d[1], 2), ones])
    rows[V_DQ] = jnp.tile(g_qk_d[0], 8) * (D_DH ** -0.5 * LOG2E)
    rows[V_GSUB] = g_sub_a
    rows[V_GX] = g_x
    rows[V_GQX] = g_qk_x[0] * (X_DH ** -0.5 * LOG2E)
    rows[V_GMLP] = g_mlp
    rows[V_GMEM] = g_mem
    rows[V_GKX] = g_qk_x[1]
    return jnp.stack([_pad_row(r.astype(F32)) for r in rows])


def _layout_latent_weights(w_uq, w_ukv):
    wuq = jnp.pad(w_uq.reshape(B_Q_LORA, B_HEADS, B_NOPE + B_ROPE),
                  ((0, 0), (0, 0), (0, 32))).reshape(B_Q_LORA, B_HEADS * LANES)
    wukv = w_ukv.reshape(B_KV_LORA, B_HEADS, B_NOPE + B_DV)
    wukvk = jnp.pad(wukv[:, :, :B_NOPE], ((0, 0), (0, 0), (0, HALF))).reshape(
        B_KV_LORA, B_HEADS * LANES)
    wukvv = wukv[:, :, B_NOPE:].reshape(B_KV_LORA, B_HEADS * B_DV)
    return wuq.astype(BF16), wukvk.astype(BF16), wukvv.astype(BF16)


def _score_bounds(rel_bias, ga, gb, gc, gd, sink):
    def amax(v):
        return jnp.max(jnp.abs(v))

    bound_a = A_DK ** 0.5 * amax(ga[0]) * amax(ga[1]) + amax(rel_bias[:, :A_HEADS])
    norm_b = [jnp.sqrt(B_NOPE * amax(gb[s, :B_NOPE]) ** 2 + B_ROPE * amax(gb[s, B_NOPE:]) ** 2)
              for s in range(2)]
    bound_b = (B_NOPE + B_ROPE) ** -0.5 * norm_b[0] * norm_b[1]
    bound_c = jnp.maximum(
        C_DH ** 0.5 * amax(gc[0]) * amax(gc[1]) + amax(rel_bias[:, A_HEADS:]), amax(sink))
    bound_d = D_DH ** 0.5 * amax(gd[0]) * amax(gd[1])
    return LOG2E * jnp.stack([bound_a, bound_b, bound_c, bound_d])


def _forward(x, mem, rel_bias, g_mix, w_in, lam, g_qk_a, g_sub_a, g_cq, g_ckv, w_uq, w_ukv,
             g_qk_b, g_qk_c, sink_c, g_qk_d, w_branch, w_out, g_x, g_mem, w_xq, w_xkv,
             g_qk_x, w_xo, g_mlp, w_up, w_down, *, t_attn, tk_attn, n_sub, tm, tm_wide,
             rows_c):
    batch, seq, _ = x.shape
    depth = w_in.shape[0]
    assert seq % t_attn == 0 and seq % tm == 0 and seq % rows_c == 0 and seq % GRID_W == 0
    assert seq % tm_wide == 0 and seq % (t_attn * n_sub) == 0 and t_attn % tk_attn == 0

    bd64 = _block_diag_ones([(HALF, True)] * 4)
    bdb = _block_diag_ones([(B_NOPE, True), (B_ROPE, True), (32, False)] * 2)
    cosb, sinb, cosd, sind = _rope_tables(seq)
    t_robust = min(t_attn, 512)
    bias_a_t = _bias_dense(rel_bias, t_attn, tk_attn, key_axis=0)
    bias_c = _bias_window(rel_bias)

    def cols_a(p):
        return (p, p, p, p, p, p)

    def cols_b(p):
        return (2 * p, 2 * p + 1, 2 * p, 2 * p + 1, p, p)

    def cols_b_fast(p):
        return (2 * p, 2 * p + 1, 2 * p, 2 * p + 1, 2 * p, 2 * p + 1)

    def cols_d(p):
        g = p // 2
        return (p, p, 2 * g, 2 * g + 1, g, g)

    vec = jax.vmap(_vector_table)(g_mix, g_qk_a, g_cq, g_ckv, g_qk_b, g_qk_c, g_qk_d, g_sub_a,
                                  g_x, g_qk_x, g_mlp, g_mem)
    win, wgate = jax.vmap(_layout_w_in)(w_in)
    wuq, wukvk, wukvv = jax.vmap(_layout_latent_weights)(w_uq, w_ukv)
    wb, wout, wxq, wxkv, wxo, wup, wdown = (
        w.astype(BF16) for w in (w_branch, w_out, w_xq, w_xkv, w_xo, w_up, w_down))
    bounds = jax.vmap(functools.partial(_score_bounds, rel_bias))(
        g_qk_a, g_qk_b, g_qk_c, g_qk_d, sink_c)

    x2d = x.reshape(batch * seq, D_MODEL)
    for layer in range(depth):
        lam_init = 0.8 - 0.6 * math.exp(-0.3 * layer)
        (qa, ka1, ka2, va, qb, kb, vb, qc, kc, vc, qd, kd, vd, gates, vat, vbt, vdt,
         qat, qbt, qdt) = _front(
            x2d, layer, vec, win, wgate, wuq, wukvk, wukvv, bd64, bdb, cosb, sinb, cosd, sind,
            seq, tm)
        qa, ka1, ka2, va, qb, kb, vb, qc, kc, vc, qd, kd, vd = (
            a.reshape(batch, seq, a.shape[1])
            for a in (qa, ka1, ka2, va, qb, kb, vb, qc, kc, vc, qd, kd, vd))
        bound_a, bound_b, bound_c, bound_d = (bounds[layer, n] for n in range(4))

        diff_kw = dict(lam=lam, vec=vec, layer=layer, lam_init=lam_init)
        o_a = _bounded(
            bound_a,
            lambda: _pair_attention(qat, ka1, ka2, vat, cols_a, A_HEADS, t_attn, "diff", True,
                                    tk=tk_attn, n_sub=n_sub, bias=bias_a_t, **diff_kw),
            lambda: _pair_attention(qa, ka1, ka2, va, cols_a, A_HEADS, t_robust, "diff", False,
                                    bias=_bias_dense(rel_bias, t_robust, t_robust, key_axis=1),
                                    **diff_kw))
        o_b = _bounded(
            bound_b,
            lambda: _pair_attention(qbt, kb, kb, vbt, cols_b_fast, B_HEADS // 2, t_attn, "merge",
                                    True, tk=tk_attn, n_sub=n_sub),
            lambda: _pair_attention(qb, kb, kb, vb, cols_b, B_HEADS // 2, t_robust, "merge", False))
        o_c = _bounded(
            bound_c,
            lambda: _window_attention(qc, kc, vc, bias_c, sink_c, layer, rows_c, True),
            lambda: _window_attention(qc, kc, vc, bias_c, sink_c, layer, rows_c, False))
        o_d = _bounded(
            bound_d,
            lambda: _pair_attention(qdt, kd, kd, vdt, cols_d, D_HEADS // 2, t_attn, "merge", True,
                                    tk=tk_attn, n_sub=n_sub),
            lambda: _pair_attention(qd, kd, kd, vd, cols_d, D_HEADS // 2, t_robust, "merge", False))

        kx, vx = _memkv(mem, layer, vec, wxkv)
        t2 = batch * seq
        x2d = _mix(o_a.reshape(t2, -1), o_b.reshape(t2, -1), o_c.reshape(t2, -1),
                   o_d.reshape(t2, -1), gates, x2d, layer, wb, wout, vec, wxq, kx, vx, wxo,
                   seq, tm_wide)
        x2d = _mlp(x2d, layer, vec, wup, wdown, tm_wide)
    return x2d.reshape(batch, seq, D_MODEL)


def kernel(x, mem, rel_bias, g_mix, w_in, lam, g_qk_a, g_sub_a, g_cq, g_ckv, w_uq, w_ukv, g_qk_b, g_qk_c, sink_c, g_qk_d, w_branch, w_out, g_x, g_mem, w_xq, w_xkv, g_qk_x, w_xo, g_mlp, w_up, w_down):
    return _forward(x, mem, rel_bias, g_mix, w_in, lam, g_qk_a, g_sub_a, g_cq, g_ckv, w_uq, w_ukv,
                    g_qk_b, g_qk_c, sink_c, g_qk_d, w_branch, w_out, g_x, g_mem, w_xq, w_xkv,
                    g_qk_x, w_xo, g_mlp, w_up, w_down, t_attn=512, tk_attn=256, n_sub=4, tm=256,
                    tm_wide=512, rows_c=512)
```

```python
import functools
import math

import numpy as np
import jax
import jax.numpy as jnp
from jax import lax
from jax.experimental import pallas as pl
from jax.experimental.pallas import tpu as pltpu

F32 = jnp.float32
BF16 = jnp.bfloat16

D_MODEL = 1024
GRID_W = 64
EPS = 1e-6
ROPE_THETA = 10000.0
NEG_INF = -1e30
LOG2E = 1.4426950408889634
SAFE_SCORE = 60.0

A_HEADS, A_DK, A_DV = 4, 64, 128
B_HEADS, B_Q_LORA, B_KV_LORA, B_NOPE, B_ROPE, B_DV = 8, 256, 128, 64, 32, 64
C_HEADS, C_KV_HEADS, C_DH, WINDOW = 8, 2, 64, 128
D_HEADS, D_KV_HEADS, D_DH = 8, 2, 64
X_HEADS, X_DH = 4, 128
D_FF = 4 * D_MODEL
REL_BUCKETS, REL_MAX_DIST = 32, 128
N_BRANCH, BRANCH_W = 4, 512

IN_SIZES = (512, 512, 512, B_Q_LORA, B_KV_LORA, B_ROPE, 512, 128, 128, 512, 128, 128,
            N_BRANCH * D_MODEL)
GROUP_W = 512
N_QKV_GROUPS = 7
N_GATE_GROUPS = N_BRANCH * D_MODEL // GROUP_W

LANES = 128
HALF = 64
SEG = 256
VMEM_CAP = 60 * 1024 * 1024

(V_GMIX, V_AQ, V_AK, V_GCQ, V_GCKV, V_QB, V_QB_CNT, V_KB, V_KR, V_CQ, V_G5, V_DQ,
 V_GSUB, V_GX, V_GQX, V_GMLP, V_GMEM, V_GKX, N_VROWS) = range(19)


def _vmem_limit(*nbytes):
    return int(min(VMEM_CAP, sum(nbytes) + (6 << 20)))


def _const_spec(shape):
    zeros = (0,) * len(shape)
    return pl.BlockSpec(shape, lambda *_: zeros, pipeline_mode=pl.Buffered(1))


def _layer_spec(stacked, layer):
    index = (layer,) + (0,) * (stacked.ndim - 1)
    return pl.BlockSpec((None,) + stacked.shape[1:], lambda *_: index,
                        pipeline_mode=pl.Buffered(1))


def _nbytes(shape, dtype):
    return int(np.prod(shape)) * jnp.dtype(dtype).itemsize


def _dot(a, b):
    return jnp.dot(a, b, preferred_element_type=F32)


def _dot_nt(a, b):
    return lax.dot_general(a, b, (((1,), (1,)), ((), ())), preferred_element_type=F32)


def _rms_rows(x, g):
    return x * lax.rsqrt(jnp.mean(x * x, axis=-1, keepdims=True) + EPS) * g


def _seg_ssq(y, bd):
    y2 = (y * y).astype(BF16)
    parts = [_dot(y2[:, c * SEG:(c + 1) * SEG], bd) for c in range(y.shape[1] // SEG)]
    return parts[0] if len(parts) == 1 else jnp.concatenate(parts, axis=1)


def _seg_scale(y, ssq, inv_cnt, gain):
    return y * lax.rsqrt(ssq * inv_cnt + EPS) * gain


def _rope32(x, cos, sin):
    w = x.shape[1]
    lane = lax.broadcasted_iota(jnp.int32, x.shape, 1)
    first = (lane & 16) == 0
    rot = jnp.where(first, -pltpu.roll(x, w - 16, 1), pltpu.roll(x, 16, 1))
    return x * cos + rot * sin


def _tile_lanes(v, n):
    return v if n == 1 else jnp.concatenate([v] * n, axis=1)


def _pair_split(x, fill):
    lane = lax.broadcasted_iota(jnp.int32, x.shape, 1)
    low = lane < HALF
    sw = pltpu.roll(x, HALF, 1)
    return (jnp.where(low, x, fill), jnp.where(low, fill, sw), jnp.where(low, sw, fill),
            jnp.where(low, fill, x))


def _front_kernel(x_ref, vec_ref, win_ref, wgate_ref, wuq_ref, wukvk_ref, wukvv_ref, bd64_ref,
                  bdb_ref,
                  cosb_ref, sinb_ref, cosd_ref, sind_ref,
                  qa_ref, ka1_ref, ka2_ref, va_ref, qb_ref, kb_ref, vb_ref,
                  qc_ref, kc_ref, vc_ref, qd_ref, kd_ref, vd_ref, gate_ref,
                  vat_ref, vbt_ref, vdt_ref, qat_ref, qbt_ref, qdt_ref):
    def row(r, w):
        return vec_ref[r:r + 1, :w]

    inv64 = 1.0 / HALF
    bd64 = bd64_ref[...]
    h = _rms_rows(x_ref[...], row(V_GMIX, D_MODEL)).astype(BF16)

    def group(g):
        w_ref, g = (win_ref, g) if g < N_QKV_GROUPS else (wgate_ref, g - N_QKV_GROUPS)
        return _dot(h, w_ref[:, g * GROUP_W:(g + 1) * GROUP_W])

    lane512 = lax.broadcasted_iota(jnp.int32, (x_ref.shape[0], GROUP_W), 1)

    def gate(c, z):
        gate_ref[:, c * GROUP_W:(c + 1) * GROUP_W] = (1.0 / (1.0 + jnp.exp(-z))).astype(BF16)

    def gate_group(c):
        return group(N_QKV_GROUPS + c)

    def finish_a_q(y, ssq):
        qa = _seg_scale(y, ssq, inv64, row(V_AQ, GROUP_W))
        qa_ref[...] = qa.astype(BF16)
        qat_ref[...] = qa.T.astype(BF16)

    def finish_a_kv(yk, ssq, yv):
        ka = _seg_scale(yk, ssq, inv64, row(V_AK, GROUP_W))
        first_map = (lane512 & HALF) == 0
        ka1_ref[...] = jnp.where(first_map, ka, 0.0).astype(BF16)
        ka2_ref[...] = jnp.where(first_map, 0.0, ka).astype(BF16)
        va_ref[...] = yv.astype(BF16)
        vat_ref[...] = yv.T.astype(BF16)

    def finish_b(tq, ssq_tq, tk, ssq_tk, kr, vb):
        cosb = cosb_ref[...]
        sinb = sinb_ref[...]
        qb = _seg_scale(tq, ssq_tq, row(V_QB_CNT, D_MODEL), row(V_QB, D_MODEL))
        qb = _rope32(qb, _tile_lanes(cosb, B_HEADS), _tile_lanes(sinb, B_HEADS))
        qb_ref[...] = qb.astype(BF16)
        qbt_ref[...] = qb.T.astype(BF16)
        k_nope = _seg_scale(tk, ssq_tk, inv64, row(V_KB, D_MODEL))
        kr = kr * lax.rsqrt(jnp.sum(kr * kr, axis=-1, keepdims=True) * (1.0 / B_ROPE) + EPS)
        kr = _rope32(kr * row(V_KR, LANES), cosb, sinb)
        kb_ref[...] = (k_nope + _tile_lanes(kr, B_HEADS)).astype(BF16)
        vb_ref[...] = vb.astype(BF16)
        vbt_ref[...] = vb.T.astype(BF16)

    def finish_cd_kv(y, ssq):
        yn = _seg_scale(y, ssq, inv64, row(V_G5, GROUP_W))
        kc = yn[:, 0:LANES]
        vc = y[:, LANES:2 * LANES]
        kd = _rope32(yn[:, 2 * LANES:3 * LANES], cosd_ref[...], sind_ref[...])
        vd = y[:, 3 * LANES:]
        kc_ref[...] = jnp.concatenate(_pair_split(kc, 0.0), axis=1).astype(BF16)
        kd_ref[...] = jnp.concatenate(_pair_split(kd, 0.0), axis=1).astype(BF16)
        vc_ref[...] = jnp.concatenate(_pair_split(vc, 1.0), axis=1).astype(BF16)
        low = lax.broadcasted_iota(jnp.int32, vd.shape, 1) < HALF
        vd_sw = pltpu.roll(vd, HALF, 1)
        vd_ref[...] = jnp.concatenate([jnp.where(low, vd, vd_sw), jnp.where(low, vd_sw, vd)],
                                      axis=1).astype(BF16)
        vdt_ref[...] = vd.T.astype(BF16)

    def finish_c_q(y, ssq):
        qc_ref[...] = _seg_scale(y, ssq, inv64, row(V_CQ, GROUP_W)).astype(BF16)

    def finish_d_q(y, ssq):
        qd = _seg_scale(y, ssq, inv64, row(V_DQ, GROUP_W))
        qd = _rope32(qd, _tile_lanes(cosd_ref[...], 4), _tile_lanes(sind_ref[...], 4))
        qd_ref[...] = qd.astype(BF16)
        qdt_ref[...] = qd.T.astype(BF16)

    y0 = group(0)
    y1 = group(1)
    ssq0 = _seg_ssq(y0, bd64)
    y2 = group(2)
    finish_a_q(y0, ssq0)
    ssq1 = _seg_ssq(y1, bd64)
    y3 = group(3)
    finish_a_kv(y1, ssq1, y2)
    cq = _rms_rows(y3[:, :B_Q_LORA], row(V_GCQ, B_Q_LORA)).astype(BF16)
    ckv = _rms_rows(y3[:, B_Q_LORA:B_Q_LORA + B_KV_LORA], row(V_GCKV, B_KV_LORA)).astype(BF16)
    z = gate_group(0)
    tq = _dot(cq, wuq_ref[...])
    tk = _dot(ckv, wukvk_ref[...])
    vb = _dot(ckv, wukvv_ref[...])
    gate(0, z)
    y4 = group(4)
    ssq_tq = _seg_ssq(tq, bdb_ref[...])
    ssq_tk = _seg_ssq(tk, bd64)
    gate(1, gate_group(1))
    z = gate_group(2)
    finish_b(tq, ssq_tq, tk, ssq_tk, y3[:, B_Q_LORA + B_KV_LORA:], vb)
    gate(2, z)
    y5 = group(5)
    ssq4 = _seg_ssq(y4, bd64)
    gate(3, gate_group(3))
    z = gate_group(4)
    finish_c_q(y4, ssq4)
    gate(4, z)
    y6 = group(6)
    ssq5 = _seg_ssq(y5, bd64)
    gate(5, gate_group(5))
    z = gate_group(6)
    finish_cd_kv(y5, ssq5)
    gate(6, z)
    ssq6 = _seg_ssq(y6, bd64)
    z = gate_group(7)
    finish_d_q(y6, ssq6)
    gate(7, z)
    assert N_GATE_GROUPS == 8


def _front(x2d, layer, vec, win, wgate, wuq, wukvk, wukvv, bd64, bdb, cosb, sinb, cosd, sind,
           seq, tm):
    t_tokens = x2d.shape[0]
    n_pos = seq // tm
    widths = (512, 512, 512, 512, 1024, 1024, 512, 512, 512, 512, 512, 512, 256, 4096)
    t_rows = (A_HEADS * A_DV, B_HEADS * B_DV, D_KV_HEADS * D_DH, A_HEADS * LANES,
              B_HEADS * LANES, D_HEADS * D_DH)

    def tile(w):
        return pl.BlockSpec((tm, w), lambda i: (i, 0))

    def pos(w):
        return pl.BlockSpec((tm, w), lambda i: (i % n_pos, 0))

    def tile_t(r):
        return pl.BlockSpec((None, r, tm), lambda i: (i // n_pos, 0, i % n_pos))

    per_layer = (vec, win, wgate, wuq, wukvk, wukvv)
    consts = per_layer + (bd64, bdb)
    in_specs = ([tile(D_MODEL)] + [_layer_spec(c, layer) for c in per_layer]
                + [_const_spec(bd64.shape), _const_spec(bdb.shape)] + [pos(LANES)] * 4)
    limit = _vmem_limit(sum(_nbytes(c.shape[-2:], c.dtype) for c in consts),
                        2 * tm * (D_MODEL * 4 + 4 * LANES * 4 + (sum(widths) + sum(t_rows)) * 2),
                        12 * tm * D_MODEL * 4)
    return pl.pallas_call(
        _front_kernel,
        grid=(t_tokens // tm,),
        in_specs=in_specs,
        out_specs=[tile(w) for w in widths] + [tile_t(r) for r in t_rows],
        out_shape=([jax.ShapeDtypeStruct((t_tokens, w), BF16) for w in widths]
                   + [jax.ShapeDtypeStruct((t_tokens // seq, r, seq), BF16) for r in t_rows]),
        compiler_params=pltpu.CompilerParams(dimension_semantics=("arbitrary",),
                                             vmem_limit_bytes=limit),
        name="front",
    )(x2d, *consts, cosb, sinb, cosd, sind)


def _t5_bucket(rel):
    nb = REL_BUCKETS // 2
    max_exact = nb // 2
    n = jnp.abs(rel)
    nf = jnp.maximum(n, 1).astype(F32)
    large = max_exact + (jnp.log(nf / max_exact) / math.log(REL_MAX_DIST / max_exact)
                         * (nb - max_exact)).astype(jnp.int32)
    large = jnp.minimum(large, nb - 1)
    return jnp.where(rel > 0, nb, 0) + jnp.where(n < max_exact, n, large)


def _table_lookup(bucket, tbl_ref, col):
    acc = jnp.zeros(bucket.shape, F32)
    for b in range(REL_BUCKETS):
        acc = jnp.where(bucket == b, tbl_ref[b, col], acc)
    return acc


def _bias_dense_kernel(tbl_ref, o_ref, *, tk, key_axis):
    h = pl.program_id(0)
    d = pl.program_id(1) - 2
    last = pl.num_programs(1) - 3
    half = REL_BUCKETS // 2

    @pl.when(d == -2)
    def _():
        o_ref[...] = jnp.full(o_ref.shape, tbl_ref[half - 1, h] * LOG2E, F32)

    @pl.when(d == last)
    def _():
        o_ref[...] = jnp.full(o_ref.shape, tbl_ref[REL_BUCKETS - 1, h] * LOG2E, F32)

    @pl.when((d > -2) & (d < last))
    def _():
        rel = (d * tk + lax.broadcasted_iota(jnp.int32, o_ref.shape, key_axis)
               - lax.broadcasted_iota(jnp.int32, o_ref.shape, 1 - key_axis))
        o_ref[...] = _table_lookup(_t5_bucket(rel), tbl_ref, h) * LOG2E


def _bias_window_kernel(tbl_ref, o_ref):
    h = pl.program_id(0)
    shape = (LANES, 3 * LANES)
    rel = (lax.broadcasted_iota(jnp.int32, shape, 1) - LANES
           - lax.broadcasted_iota(jnp.int32, shape, 0))
    bias = _table_lookup(_t5_bucket(rel), tbl_ref, A_HEADS + h) * LOG2E
    o_ref[...] = jnp.where(jnp.abs(rel) <= WINDOW, bias, NEG_INF)


def _bias_dense(rel_bias, tq, tk, key_axis):
    assert tk >= LANES and tq % tk == 0
    n_tiles = tq // tk + 4
    shape = (tk, tq) if key_axis == 0 else (tq, tk)
    return pl.pallas_call(
        functools.partial(_bias_dense_kernel, tk=tk, key_axis=key_axis),
        grid=(A_HEADS, n_tiles),
        in_specs=[pl.BlockSpec(memory_space=pltpu.SMEM)],
        out_specs=pl.BlockSpec((None, None) + shape, lambda h, d: (h, d, 0, 0)),
        out_shape=jax.ShapeDtypeStruct((A_HEADS, n_tiles) + shape, F32),
        name="bias_dense",
    )(rel_bias)


def _bias_window(rel_bias):
    return pl.pallas_call(
        _bias_window_kernel,
        grid=(C_HEADS,),
        in_specs=[pl.BlockSpec(memory_space=pltpu.SMEM)],
        out_specs=pl.BlockSpec((None, LANES, 3 * LANES), lambda h: (h, 0, 0)),
        out_shape=jax.ShapeDtypeStruct((C_HEADS, LANES, 3 * LANES), F32),
        name="bias_window",
    )(rel_bias)


def _pair_attn_kernel(*refs, t, n_chunks, mode, lam_init):
    if mode == "diff":
        (qa_ref, qb_ref, ka_ref, kb_ref, va_ref, vb_ref, bias_ref, lam_ref, gsub_ref, o_ref,
         ma, la, acca, mb, lb, accb) = refs
    else:
        (qa_ref, qb_ref, ka_ref, kb_ref, va_ref, vb_ref, o_ref,
         ma, la, acca, mb, lb, accb) = refs
        bias_ref = None
    i = pl.program_id(2)
    streams = ((qa_ref[...], ka_ref, va_ref, ma, la, acca),
               (qb_ref[...], kb_ref, vb_ref, mb, lb, accb))
    for _, _, _, m, l, acc in streams:
        m[...] = jnp.full(m.shape, NEG_INF, F32)
        l[...] = jnp.zeros(l.shape, F32)
        acc[...] = jnp.zeros(acc.shape, F32)

    def body(c, carry):
        off = pl.multiple_of(c * t, t)
        bias = None if bias_ref is None else bias_ref[jnp.clip(c - i, -2, 2) + 2]
        for q, k_ref, v_ref, m, l, acc in streams:
            s = _dot_nt(q, k_ref[pl.ds(off, t), :])
            if bias is not None:
                s = s + bias
            m_prev = m[...]
            m_new = jnp.maximum(m_prev, jnp.max(s, axis=1, keepdims=True))
            alpha = jnp.exp2(m_prev - m_new)
            p = jnp.exp2(s - m_new)
            l[...] = alpha * l[...] + jnp.sum(p, axis=1, keepdims=True)
            acc[...] = alpha * acc[...] + _dot(p.astype(BF16), v_ref[pl.ds(off, t), :])
            m[...] = m_new
        return carry

    lax.fori_loop(0, n_chunks, body, 0)
    _finish_pair(acca[...] / la[...], accb[...] / lb[...], mode, refs, lam_init)


def _write_diff(oa, ob, lam_ref, vec_ref, o_ref, lam_init):
    lam = lam_ref[...]
    lam_full = (jnp.exp(jnp.sum(lam[0:1] * lam[1:2], keepdims=True))
                - jnp.exp(jnp.sum(lam[2:3] * lam[3:4], keepdims=True)) + lam_init)
    o = oa - lam_full * ob
    gsub = vec_ref[V_GSUB:V_GSUB + 1, :A_DV]
    o_ref[...] = (_rms_rows(o, gsub) * (1.0 - lam_init)).astype(BF16)


def _finish_pair(oa, ob, mode, refs, lam_init):
    if mode == "diff":
        _write_diff(oa, ob, *refs[7:10], lam_init)
    else:
        o_ref = refs[6]
        lane = lax.broadcasted_iota(jnp.int32, oa.shape, 1)
        o_ref[...] = jnp.where(lane < HALF, oa, ob).astype(BF16)


def _pair_attn_fast_kernel(*refs, tq, tk, n_sub, n_chunks, mode, lam_init):
    qta_ref, qtb_ref, ka_ref, kb_ref, vta_ref, vtb_ref = refs[:6]
    bias_ref = refs[6] if mode == "diff" else None
    o_ref = refs[9] if mode == "diff" else refs[6]
    i = pl.program_id(2)
    ratio = tq // tk
    streams = ((qta_ref, ka_ref, vta_ref), (qtb_ref, kb_ref, vtb_ref))

    tasks = [(u, c, n_s) for u in range(n_sub) for c in range(n_chunks) for n_s in range(2)]

    def scores(u, c, n_s):
        qt_ref, k_ref, _ = streams[n_s]
        s = _dot(k_ref[c * tk:(c + 1) * tk, :], qt_ref[:, u * tq:(u + 1) * tq])
        if bias_ref is not None:
            s = s + bias_ref[jnp.clip(c - (i * n_sub + u) * ratio, -2, ratio + 1) + 2]
        return s

    def finish(u):
        oa = sums.pop((u, 0)) / jnp.sum(denoms.pop((u, 0)), axis=0, keepdims=True)
        ob = sums.pop((u, 1)) / jnp.sum(denoms.pop((u, 1)), axis=0, keepdims=True)
        out = o_ref.at[u * tq:(u + 1) * tq, :]
        if mode == "diff":
            _write_diff(oa.T, ob.T, refs[7], refs[8], out, lam_init)
        else:
            out[...] = jnp.concatenate([oa, ob], axis=0).T.astype(BF16)

    sums = {}
    denoms = {}
    s_live, p_live = {}, {}
    for j in range(len(tasks) + 2):
        if j < len(tasks):
            s_live[j] = scores(*tasks[j])
        if 0 <= j - 1 < len(tasks):
            u, _, n_s = tasks[j - 1]
            p = jnp.exp2(s_live.pop(j - 1))
            part = jnp.sum(p.reshape(tk // 8, 8, tq), axis=0)
            denoms[u, n_s] = part if (u, n_s) not in denoms else denoms[u, n_s] + part
            p_live[j - 1] = p.astype(BF16)
        if 0 <= j - 2 < len(tasks):
            u, c, n_s = tasks[j - 2]
            part = _dot(streams[n_s][2][:, c * tk:(c + 1) * tk], p_live.pop(j - 2))
            sums[u, n_s] = part if (u, n_s) not in sums else sums[u, n_s] + part
            if c == n_chunks - 1 and n_s == 1:
                finish(u)


def _pair_attention(q, k_a, k_b, v, cols, n_pairs, t, mode, fast, tk=None, n_sub=1, bias=None,
                    lam=None, vec=None, layer=None, lam_init=0.0):
    batch, seq = k_a.shape[:2]

    def kv_spec(which):
        return pl.BlockSpec((None, seq, LANES), lambda b, p, i: (b, 0, cols(p)[which]))

    if fast:
        v_rows = A_DV if mode == "diff" else HALF
        t_sub, t = t, t * n_sub

        def q_spec(which):
            return pl.BlockSpec((None, LANES, t), lambda b, p, i: (b, cols(p)[which], i))

        def v_spec(which):
            return pl.BlockSpec((None, v_rows, seq), lambda b, p, i: (b, cols(p)[which], 0))

        body = functools.partial(_pair_attn_fast_kernel, tq=t_sub, tk=tk, n_sub=n_sub,
                                 n_chunks=seq // tk)
        scratch = []
        v_bytes = v_rows * seq * 2
    else:
        tk = t

        def q_spec(which):
            return pl.BlockSpec((None, t, LANES), lambda b, p, i: (b, i, cols(p)[which]))

        v_spec = kv_spec
        body = functools.partial(_pair_attn_kernel, t=t, n_chunks=seq // t)
        scratch = [pltpu.VMEM((t, 1), F32), pltpu.VMEM((t, 1), F32),
                   pltpu.VMEM((t, LANES), F32)] * 2
        v_bytes = seq * LANES * 2
    in_specs = [q_spec(0), q_spec(1), kv_spec(2), kv_spec(3), v_spec(4), v_spec(5)]
    args = [q, q, k_a, k_b, v, v]
    resident = 2 * 2 * (seq * LANES * 2 + v_bytes)
    if mode == "diff":
        in_specs += [pl.BlockSpec((None,) + bias.shape[1:], lambda b, p, i: (p, 0, 0, 0)),
                     _layer_spec(lam, layer), _layer_spec(vec, layer)]
        args += [bias, lam, vec]
        resident += 2 * _nbytes(bias.shape[1:], F32)
    limit = _vmem_limit(resident, 6 * t * LANES * 2, 8 * t * LANES * 4, 8 * t * tk * 4)
    return pl.pallas_call(
        functools.partial(body, mode=mode, lam_init=lam_init),
        grid=(batch, n_pairs, seq // t),
        in_specs=in_specs,
        out_specs=pl.BlockSpec((None, t, LANES), lambda b, p, i: (b, i, p)),
        out_shape=jax.ShapeDtypeStruct((batch, seq, n_pairs * LANES), BF16),
        scratch_shapes=scratch,
        compiler_params=pltpu.CompilerParams(
            dimension_semantics=("arbitrary", "arbitrary", "arbitrary"),
            vmem_limit_bytes=limit),
        name=("attn_fast_" if fast else "attn_") + mode,
    )(*args)


def _bounded(score_bound, fast, robust):
    return lax.cond(score_bound <= SAFE_SCORE, fast, robust)


def _window_kernel(sink_ref, q_ref, kp_ref, km_ref, kn_ref, vp_ref, vm_ref, vn_ref, bias_ref,
                   o_ref, *, rows, fast, layer):
    i = pl.program_id(1)
    n_steps = pl.num_programs(1)
    nblk = rows // LANES
    lane = lax.broadcasted_iota(jnp.int32, (LANES, LANES), 1)
    zero_row = jnp.zeros((1, LANES), F32)

    def kv_block(j):
        if j < 0:
            return kp_ref[...], vp_ref[...]
        if j >= nblk:
            return kn_ref[...], vn_ref[...]
        return km_ref[j * LANES:(j + 1) * LANES, :], vm_ref[j * LANES:(j + 1) * LANES, :]

    kcat, vcat, edge = [], [], []
    for r in range(nblk):
        blocks = [kv_block(r + d) for d in (-1, 0, 1)]
        kcat.append(jnp.concatenate([blk[0] for blk in blocks], axis=0))
        vcat.append(jnp.concatenate([blk[1] for blk in blocks], axis=0))
        if r == 0 or r == nblk - 1:
            before = zero_row + jnp.where(i > 0, 0.0, NEG_INF) if r == 0 else zero_row
            after = zero_row + jnp.where(i < n_steps - 1, 0.0, NEG_INF) if r == nblk - 1 else zero_row
            edge.append(jnp.concatenate([before, zero_row, after], axis=1))
        else:
            edge.append(None)

    def col(p, e):
        g = p // (C_HEADS // C_KV_HEADS // 2)
        return slice((2 * g + e) * LANES, (2 * g + e + 1) * LANES)

    def scores(r, p, e):
        q = q_ref[r * LANES:(r + 1) * LANES, p * LANES:(p + 1) * LANES]
        s = _dot_nt(q, kcat[r][:, col(p, e)]) + bias_ref[2 * p + e]
        return s if edge[r] is None else s + edge[r]

    def store(r, p, outs):
        o_ref[r * LANES:(r + 1) * LANES, p * LANES:(p + 1) * LANES] = (
            jnp.where(lane < HALF, outs[0], outs[1]).astype(BF16))

    tasks = [(r, p, e) for r in range(nblk) for p in range(C_HEADS // 2) for e in range(2)]
    if fast:
        s_live, p_live, outs = {}, {}, []
        for j in range(len(tasks) + 2):
            if j < len(tasks):
                s_live[j] = scores(*tasks[j])
            if 0 <= j - 1 < len(tasks):
                p_live[j - 1] = jnp.exp2(s_live.pop(j - 1)).astype(BF16)
            if 0 <= j - 2 < len(tasks):
                r, p, e = tasks[j - 2]
                o = _dot(p_live.pop(j - 2), vcat[r][:, col(p, e)])
                sink = sink_ref[layer, 2 * p + e] * LOG2E
                outs.append(o / (pltpu.roll(o, HALF, 1) + jnp.exp2(zero_row + sink)))
                if e == 1:
                    store(r, p, outs)
                    outs = []
    else:
        outs = []
        for r, p, e in tasks:
            s = scores(r, p, e)
            sink = sink_ref[layer, 2 * p + e] * LOG2E
            m = jnp.maximum(jnp.max(s, axis=1, keepdims=True), sink)
            pr = jnp.exp2(s - m)
            denom = jnp.sum(pr, axis=1, keepdims=True) + jnp.exp2(sink - m)
            outs.append(_dot(pr.astype(BF16), vcat[r][:, col(p, e)]) / denom)
            if e == 1:
                store(r, p, outs)
                outs = []


def _window_attention(q, k, v, bias, sink, layer, rows, fast):
    batch, seq, _ = q.shape
    nblk = rows // LANES
    last = seq // LANES - 1
    kw, vw = k.shape[2], v.shape[2]

    def main(w):
        return pl.BlockSpec((None, rows, w), lambda b, i: (b, i, 0))

    def prev(w):
        return pl.BlockSpec((None, LANES, w), lambda b, i: (b, jnp.maximum(i * nblk - 1, 0), 0))

    def nxt(w):
        return pl.BlockSpec((None, LANES, w), lambda b, i: (b, jnp.minimum((i + 1) * nblk, last), 0))

    limit = _vmem_limit(_nbytes(bias.shape, F32) * 2,
                        2 * (rows + 2 * LANES) * (kw + vw) * 2 + 4 * rows * 512 * 2,
                        16 * LANES * 3 * LANES * 4)
    return pl.pallas_call(
        functools.partial(_window_kernel, rows=rows, fast=fast, layer=layer),
        grid=(batch, seq // rows),
        in_specs=[pl.BlockSpec(memory_space=pltpu.SMEM), main(q.shape[2]),
                  prev(kw), main(kw), nxt(kw), prev(vw), main(vw), nxt(vw),
                  _const_spec(bias.shape)],
        out_specs=main(q.shape[2]),
        out_shape=jax.ShapeDtypeStruct(q.shape, BF16),
        compiler_params=pltpu.CompilerParams(dimension_semantics=("arbitrary", "arbitrary"),
                                             vmem_limit_bytes=limit),
        name="attn_window_fast" if fast else "attn_window",
    )(sink, q, k, k, k, v, v, v, bias)


def _memkv_kernel(mem_ref, vec_ref, w_ref, k_ref, v_ref):
    hn = _rms_rows(mem_ref[...], vec_ref[V_GMEM:V_GMEM + 1, :]).astype(BF16)
    kv = _dot(hn, w_ref[...])
    width = X_HEADS * X_DH
    gk = vec_ref[V_GKX:V_GKX + 1, :X_DH]
    for h in range(X_HEADS):
        sl = slice(h * X_DH, (h + 1) * X_DH)
        k_ref[:, sl] = _rms_rows(kv[:, sl], gk).astype(BF16)
    v_ref[...] = kv[:, width:].astype(BF16)


def _memkv(mem, layer, vec, w):
    batch, n_mem, _ = mem.shape
    width = X_HEADS * X_DH
    out = pl.BlockSpec((None, n_mem, width), lambda b: (b, 0, 0))
    return pl.pallas_call(
        _memkv_kernel,
        grid=(batch,),
        in_specs=[pl.BlockSpec((None, n_mem, D_MODEL), lambda b: (b, 0, 0)),
                  _layer_spec(vec, layer), _layer_spec(w, layer)],
        out_specs=[out, out],
        out_shape=[jax.ShapeDtypeStruct((batch, n_mem, width), BF16)] * 2,
        compiler_params=pltpu.CompilerParams(
            dimension_semantics=("arbitrary",),
            vmem_limit_bytes=_vmem_limit(_nbytes(w.shape[1:], BF16), 4 * n_mem * D_MODEL * 4,
                                         4 * n_mem * 2 * width * 4)),
        name="memkv",
    )(mem, vec, w)


def _mix_kernel(oa_ref, ob_ref, oc_ref, od_ref, gate_ref, x_ref, wb_ref, wout_ref, vec_ref,
                wxq_ref, kx_ref, vx_ref, wxo_ref, o_ref):
    gq = vec_ref[V_GQX:V_GQX + 1, :X_DH]
    n_half = 2
    rows = x_ref.shape[0] // n_half
    halves = [slice(u * rows, (u + 1) * rows) for u in range(n_half)]

    def head_cols(h):
        return slice(h * X_DH, (h + 1) * X_DH)

    def gated_mix(r):
        mix = None
        for m, o_m in enumerate((oa_ref, ob_ref, oc_ref, od_ref)):
            br = _dot(o_m[r, :], wb_ref[m])
            term = gate_ref[r, m * D_MODEL:(m + 1) * D_MODEL].astype(F32) * br
            mix = term if mix is None else mix + term
        return mix.astype(BF16)

    mixes = [gated_mix(r) for r in halves]
    x1 = [x_ref[r, :] + _dot(mix, wout_ref[...]) for r, mix in zip(halves, mixes)]
    qx = [_dot(_rms_rows(x1_u, vec_ref[V_GX:V_GX + 1, :]).astype(BF16), wxq_ref[...])
          for x1_u in x1]
    scores = [[_dot_nt(_rms_rows(qx_u[:, head_cols(h)], gq).astype(BF16), kx_ref[:, head_cols(h)])
               for h in range(X_HEADS)] for qx_u in qx]
    probs = []
    for s_u in scores:
        p_u = [jnp.exp2(s - jnp.max(s, axis=1, keepdims=True)) for s in s_u]
        probs.append([(p.astype(BF16), jnp.sum(p, axis=1, keepdims=True)) for p in p_u])
    ox = [jnp.concatenate([(_dot(p, vx_ref[:, head_cols(h)]) / denom).astype(BF16)
                           for h, (p, denom) in enumerate(p_u)], axis=1) for p_u in probs]
    for r, x1_u, ox_u in zip(halves, x1, ox):
        o_ref[r, :] = x1_u + _dot(ox_u, wxo_ref[...])


def _mix(o_a, o_b, o_c, o_d, gates, x2d, layer, wb, wout, vec, wxq, kx, vx, wxo, seq, tm):
    t_tokens = x2d.shape[0]
    per_batch = seq // tm
    n_mem, width = kx.shape[1], kx.shape[2]

    def tile(w):
        return pl.BlockSpec((tm, w), lambda i: (i, 0))

    def per_layer(a):
        return _layer_spec(a, layer)

    mem_spec = pl.BlockSpec((None, n_mem, width), lambda i: (i // per_batch, 0, 0))
    limit = _vmem_limit(sum(_nbytes(c.shape[1:], c.dtype) for c in (wb, wout, vec, wxq, wxo)),
                        2 * tm * (4 * BRANCH_W * 2 + 4 * D_MODEL * 2 + 2 * D_MODEL * 4),
                        4 * n_mem * width * 2, 10 * tm * D_MODEL * 4)
    return pl.pallas_call(
        _mix_kernel,
        grid=(t_tokens // tm,),
        in_specs=[tile(BRANCH_W)] * 4 + [tile(N_BRANCH * D_MODEL), tile(D_MODEL),
                  per_layer(wb), per_layer(wout), per_layer(vec), per_layer(wxq),
                  mem_spec, mem_spec, per_layer(wxo)],
        out_specs=tile(D_MODEL),
        out_shape=jax.ShapeDtypeStruct((t_tokens, D_MODEL), F32),
        compiler_params=pltpu.CompilerParams(dimension_semantics=("arbitrary",),
                                             vmem_limit_bytes=limit),
        name="mix",
    )(o_a, o_b, o_c, o_d, gates, x2d, wb, wout, vec, wxq, kx, vx, wxo)


def _mlp_kernel(x_ref, vec_ref, wup_ref, wdown_ref, o_ref, *, chunk):
    x = x_ref[...]
    hn = _rms_rows(x, vec_ref[V_GMLP:V_GMLP + 1, :]).astype(BF16)
    acc = x
    for c in range(D_FF // chunk):
        u = jnp.maximum(_dot(hn, wup_ref[:, c * chunk:(c + 1) * chunk]), 0.0)
        acc = acc + _dot((u * u).astype(BF16), wdown_ref[c * chunk:(c + 1) * chunk, :])
    o_ref[...] = acc


def _mlp(x2d, layer, vec, wup, wdown, tm, chunk=512):
    t_tokens = x2d.shape[0]
    tile = pl.BlockSpec((tm, D_MODEL), lambda i: (i, 0))
    limit = _vmem_limit(_nbytes(wup.shape[1:], BF16) + _nbytes(wdown.shape[1:], BF16),
                        4 * tm * D_MODEL * 4, 8 * tm * max(chunk, D_MODEL) * 4)
    return pl.pallas_call(
        functools.partial(_mlp_kernel, chunk=chunk),
        grid=(t_tokens // tm,),
        in_specs=[tile, _layer_spec(vec, layer), _layer_spec(wup, layer),
                  _layer_spec(wdown, layer)],
        out_specs=tile,
        out_shape=jax.ShapeDtypeStruct((t_tokens, D_MODEL), F32),
        compiler_params=pltpu.CompilerParams(dimension_semantics=("arbitrary",),
                                             vmem_limit_bytes=limit),
        name="mlp",
    )(x2d, vec, wup, wdown)


def _block_diag_ones(sizes):
    m = np.zeros((SEG, SEG), np.float32)
    start = 0
    for size, on in sizes:
        if on:
            m[start:start + size, start:start + size] = 1.0
        start += size
    assert start == SEG
    return jnp.asarray(m, BF16)


def _rope_tables(seq):
    pos = jnp.arange(seq)
    inv = ROPE_THETA ** (-jnp.arange(0, B_ROPE, 2, dtype=F32) / B_ROPE)

    def cs(p):
        ang = inv[:, None] * p.astype(F32)[None, :]
        return (jnp.concatenate([jnp.cos(ang)] * 2, 0).T, jnp.concatenate([jnp.sin(ang)] * 2, 0).T)

    cos_p, sin_p = cs(pos)
    one = jnp.ones((seq, 1), F32)
    zero = jnp.zeros((seq, 1), F32)
    cosb = jnp.concatenate([one * jnp.ones((1, HALF)), cos_p, one * jnp.ones((1, 32))], 1)
    sinb = jnp.concatenate([zero * jnp.ones((1, HALF)), sin_p, zero * jnp.ones((1, 32))], 1)
    cos_r, sin_r = cs(pos // GRID_W)
    cos_c, sin_c = cs(pos % GRID_W)
    cosd = jnp.concatenate([cos_r, cos_c] * 2, 1)
    sind = jnp.concatenate([sin_r, sin_c] * 2, 1)
    return cosb, sinb, cosd, sind


def _layout_w_in(w):
    off = np.concatenate([[0], np.cumsum(IN_SIZES)])

    def seg(i):
        return w[:, off[i]:off[i + 1]]

    def zeros(n):
        return jnp.zeros((w.shape[0], n), w.dtype)

    w = w.astype(BF16)
    cols = [seg(0), seg(1), seg(2), seg(3), seg(4), zeros(HALF), seg(5), zeros(32),
            seg(6), seg(7), seg(8), seg(10), seg(11), seg(9)]
    qkv = jnp.concatenate(cols, axis=1)
    assert qkv.shape[1] == N_QKV_GROUPS * GROUP_W
    return qkv, seg(12)


def _pad_row(v):
    return jnp.pad(v, (0, D_MODEL - v.shape[0]))


def _vector_table(g_mix, g_qk_a, g_cq, g_ckv, g_qk_b, g_qk_c, g_qk_d, g_sub_a, g_x, g_qk_x,
                  g_mlp, g_mem):
    z32 = jnp.zeros((32,), F32)
    z64 = jnp.zeros((HALF,), F32)
    rows = [None] * N_VROWS
    rows[V_GMIX] = g_mix
    rows[V_AQ] = jnp.tile(g_qk_a[0], 8) * (A_DK ** -0.5 * LOG2E)
    rows[V_AK] = jnp.tile(g_qk_a[1], 8)
    rows[V_GCQ] = g_cq
    rows[V_GCKV] = g_ckv
    rows[V_QB] = jnp.tile(jnp.concatenate([g_qk_b[0], z32]), B_HEADS) * (
        (B_NOPE + B_ROPE) ** -0.5 * LOG2E)
    rows[V_QB_CNT] = jnp.tile(jnp.concatenate(
        [jnp.full((B_NOPE,), 1.0 / B_NOPE), jnp.full((B_ROPE,), 1.0 / B_ROPE), jnp.ones((32,))]),
        B_HEADS)
    rows[V_KB] = jnp.tile(jnp.concatenate([g_qk_b[1, :B_NOPE], z64]), B_HEADS)
    rows[V_KR] = jnp.concatenate([z64, g_qk_b[1, B_NOPE:], z32])
    rows[V_CQ] = jnp.tile(g_qk_c[0], 8) * (C_DH ** -0.5 * LOG2E)
    ones = jnp.ones((LANES,), F32)
    rows[V_G5] = jnp.concatenate([jnp.tile(g_qk_c[1], 2), ones, jnp.tile(g_qk_d[1], 2), ones])
    rows[V_DQ] = jnp.tile(g_qk_d[0], 8) * (D_DH ** -0.5 * LOG2E)
    rows[V_GSUB] = g_sub_a
    rows[V_GX] = g_x
    rows[V_GQX] = g_qk_x[0] * (X_DH ** -0.5 * LOG2E)
    rows[V_GMLP] = g_mlp
    rows[V_GMEM] = g_mem
    rows[V_GKX] = g_qk_x[1]
    return jnp.stack([_pad_row(r.astype(F32)) for r in rows])


def _layout_latent_weights(w_uq, w_ukv):
    wuq = jnp.pad(w_uq.reshape(B_Q_LORA, B_HEADS, B_NOPE + B_ROPE),
                  ((0, 0), (0, 0), (0, 32))).reshape(B_Q_LORA, B_HEADS * LANES)
    wukv = w_ukv.reshape(B_KV_LORA, B_HEADS, B_NOPE + B_DV)
    wukvk = jnp.pad(wukv[:, :, :B_NOPE], ((0, 0), (0, 0), (0, HALF))).reshape(
        B_KV_LORA, B_HEADS * LANES)
    wukvv = wukv[:, :, B_NOPE:].reshape(B_KV_LORA, B_HEADS * B_DV)
    return wuq.astype(BF16), wukvk.astype(BF16), wukvv.astype(BF16)


def _score_bounds(rel_bias, ga, gb, gc, gd, sink):
    def amax(v):
        return jnp.max(jnp.abs(v))

    bound_a = A_DK ** 0.5 * amax(ga[0]) * amax(ga[1]) + amax(rel_bias[:, :A_HEADS])
    norm_b = [jnp.sqrt(B_NOPE * amax(gb[s, :B_NOPE]) ** 2 + B_ROPE * amax(gb[s, B_NOPE:]) ** 2)
              for s in range(2)]
    bound_b = (B_NOPE + B_ROPE) ** -0.5 * norm_b[0] * norm_b[1]
    bound_c = jnp.maximum(
        C_DH ** 0.5 * amax(gc[0]) * amax(gc[1]) + amax(rel_bias[:, A_HEADS:]), amax(sink))
    bound_d = D_DH ** 0.5 * amax(gd[0]) * amax(gd[1])
    return LOG2E * jnp.stack([bound_a, bound_b, bound_c, bound_d])


def _forward(x, mem, rel_bias, g_mix, w_in, lam, g_qk_a, g_sub_a, g_cq, g_ckv, w_uq, w_ukv,
             g_qk_b, g_qk_c, sink_c, g_qk_d, w_branch, w_out, g_x, g_mem, w_xq, w_xkv,
             g_qk_x, w_xo, g_mlp, w_up, w_down, *, t_attn, tk_attn, n_sub, tm, tm_wide,
             rows_c):
    batch, seq, _ = x.shape
    depth = w_in.shape[0]
    assert seq % t_attn == 0 and seq % tm == 0 and seq % rows_c == 0 and seq % GRID_W == 0
    assert seq % tm_wide == 0 and seq % (t_attn * n_sub) == 0 and t_attn % tk_attn == 0

    bd64 = _block_diag_ones([(HALF, True)] * 4)
    bdb = _block_diag_ones([(B_NOPE, True), (B_ROPE, True), (32, False)] * 2)
    cosb, sinb, cosd, sind = _rope_tables(seq)
    t_robust = min(t_attn, 512)
    bias_a_t = _bias_dense(rel_bias, t_attn, tk_attn, key_axis=0)
    bias_c = _bias_window(rel_bias)

    def cols_a(p):
        return (p, p, p, p, p, p)

    def cols_b(p):
        return (2 * p, 2 * p + 1, 2 * p, 2 * p + 1, p, p)

    def cols_b_fast(p):
        return (2 * p, 2 * p + 1, 2 * p, 2 * p + 1, 2 * p, 2 * p + 1)

    def cols_d(p):
        g = p // 2
        return (p, p, 2 * g, 2 * g + 1, g, g)

    vec = jax.vmap(_vector_table)(g_mix, g_qk_a, g_cq, g_ckv, g_qk_b, g_qk_c, g_qk_d, g_sub_a,
                                  g_x, g_qk_x, g_mlp, g_mem)
    win, wgate = jax.vmap(_layout_w_in)(w_in)
    wuq, wukvk, wukvv = jax.vmap(_layout_latent_weights)(w_uq, w_ukv)
    wb, wout, wxq, wxkv, wxo, wup, wdown = (
        w.astype(BF16) for w in (w_branch, w_out, w_xq, w_xkv, w_xo, w_up, w_down))
    bounds = jax.vmap(functools.partial(_score_bounds, rel_bias))(
        g_qk_a, g_qk_b, g_qk_c, g_qk_d, sink_c)

    x2d = x.reshape(batch * seq, D_MODEL)
    for layer in range(depth):
        lam_init = 0.8 - 0.6 * math.exp(-0.3 * layer)
        (qa, ka1, ka2, va, qb, kb, vb, qc, kc, vc, qd, kd, vd, gates, vat, vbt, vdt,
         qat, qbt, qdt) = _front(
            x2d, layer, vec, win, wgate, wuq, wukvk, wukvv, bd64, bdb, cosb, sinb, cosd, sind,
            seq, tm)
        qa, ka1, ka2, va, qb, kb, vb, qc, kc, vc, qd, kd, vd = (
            a.reshape(batch, seq, a.shape[1])
            for a in (qa, ka1, ka2, va, qb, kb, vb, qc, kc, vc, qd, kd, vd))
        bound_a, bound_b, bound_c, bound_d = (bounds[layer, n] for n in range(4))

        diff_kw = dict(lam=lam, vec=vec, layer=layer, lam_init=lam_init)
        o_a = _bounded(
            bound_a,
            lambda: _pair_attention(qat, ka1, ka2, vat, cols_a, A_HEADS, t_attn, "diff", True,
                                    tk=tk_attn, n_sub=n_sub, bias=bias_a_t, **diff_kw),
            lambda: _pair_attention(qa, ka1, ka2, va, cols_a, A_HEADS, t_robust, "diff", False,
                                    bias=_bias_dense(rel_bias, t_robust, t_robust, key_axis=1),
                                    **diff_kw))
        o_b = _bounded(
            bound_b,
            lambda: _pair_attention(qbt, kb, kb, vbt, cols_b_fast, B_HEADS // 2, t_attn, "merge",
                                    True, tk=tk_attn, n_sub=n_sub),
            lambda: _pair_attention(qb, kb, kb, vb, cols_b, B_HEADS // 2, t_robust, "merge", False))
        o_c = _bounded(
            bound_c,
            lambda: _window_attention(qc, kc, vc, bias_c, sink_c, layer, rows_c, True),
            lambda: _window_attention(qc, kc, vc, bias_c, sink_c, layer, rows_c, False))
        o_d = _bounded(
            bound_d,
            lambda: _pair_attention(qdt, kd, kd, vdt, cols_d, D_HEADS // 2, t_attn, "merge", True,
                                    tk=tk_attn, n_sub=n_sub),
            lambda: _pair_attention(qd, kd, kd, vd, cols_d, D_HEADS // 2, t_robust, "merge", False))

        kx, vx = _memkv(mem, layer, vec, wxkv)
        t2 = batch * seq
        x2d = _mix(o_a.reshape(t2, -1), o_b.reshape(t2, -1), o_c.reshape(t2, -1),
                   o_d.reshape(t2, -1), gates, x2d, layer, wb, wout, vec, wxq, kx, vx, wxo,
                   seq, tm_wide)
        x2d = _mlp(x2d, layer, vec, wup, wdown, tm_wide)
    return x2d.reshape(batch, seq, D_MODEL)


def kernel(x, mem, rel_bias, g_mix, w_in, lam, g_qk_a, g_sub_a, g_cq, g_ckv, w_uq, w_ukv, g_qk_b, g_qk_c, sink_c, g_qk_d, w_branch, w_out, g_x, g_mem, w_xq, w_xkv, g_qk_x, w_xo, g_mlp, w_up, w_down):
    return _forward(x, mem, rel_bias, g_mix, w_in, lam, g_qk_a, g_sub_a, g_cq, g_ckv, w_uq, w_ukv,
                    g_qk_b, g_qk_c, sink_c, g_qk_d, w_branch, w_out, g_x, g_mem, w_xq, w_xkv,
                    g_qk_x, w_xo, g_mlp, w_up, w_down, t_attn=512, tk_attn=256, n_sub=4, tm=256,
                    tm_wide=512, rows_c=512)
```
